```python
import jax, jax.numpy as jnp
from jax import lax
import numpy as np

D_MODEL = 1024
BATCH = 4
SEQ = 8192
DEPTH = 1
DEC_BATCH = 128
DEC_SEQ = 4
PAST_LEN = 8192
PAGE_SIZE = 128

CONV_CH = 512
CONV_WIDTH = 31
FOX_HEADS = 8
FOX_HEAD_DIM = 64
FOX_WIDTH = FOX_HEADS * FOX_HEAD_DIM
MEM_HEADS = 4
MEM_HEAD_DIM = 128
MEM_WIDTH = MEM_HEADS * MEM_HEAD_DIM
MEM_TOKENS = 256
N_BRANCH = 3
N_EXPERTS = 32
TOP_K = 4
D_FF = D_MODEL
SWIGLU_LIMIT = 7.0
SWIGLU_ALPHA = 1.702
Q_BLOCK = 128
MOE_BLOCK = 128
FORGET_BIAS_INIT = 3.0
LN_EPS = 1e-5
DEEPNORM_ALPHA = (2 * DEPTH) ** 0.25
DEEPNORM_BETA = (8 * DEPTH) ** -0.25
IN_WIDTH = 2 * CONV_CH + 3 * FOX_WIDTH + FOX_HEADS + MEM_WIDTH + N_BRANCH * D_MODEL

kernel_name = 'fox_conformer_memory_moe_step'


def layernorm(x, g, b):
    xf = x.astype(jnp.float32)
    mu = jnp.mean(xf, axis=-1, keepdims=True)
    var = jnp.mean(jnp.square(xf - mu), axis=-1, keepdims=True)
    y = (xf - mu) * lax.rsqrt(var + LN_EPS) * g.astype(jnp.float32) + b.astype(jnp.float32)
    return y.astype(x.dtype)


def project_inputs(x, w_in, b_in, b_forget):
    z = x @ w_in + b_in
    sizes = [2 * CONV_CH, FOX_WIDTH, FOX_WIDTH, FOX_WIDTH, FOX_HEADS, MEM_WIDTH]
    offs = [int(o) for o in np.cumsum(sizes)]
    glu, q, k, v, f_logit, q_mem, g = jnp.split(z, offs, axis=-1)
    lead = x.shape[:-1]
    u = glu[..., :CONV_CH] * jax.nn.sigmoid(glu[..., CONV_CH:])
    logf = jax.nn.log_sigmoid((f_logit + b_forget).astype(jnp.float32))
    heads = lambda t: t.reshape(*lead, FOX_HEADS, FOX_HEAD_DIM)
    gates = jax.nn.sigmoid(g).reshape(*lead, N_BRANCH, D_MODEL)
    q_mem = q_mem.reshape(*lead, MEM_HEADS, MEM_HEAD_DIM)
    return u, heads(q), heads(k), heads(v), logf, q_mem, gates


def conv_branch(u_ext, conv_w, conv_b, ln_g, ln_b, w_conv_out, b_conv_out):
    y = lax.conv_general_dilated(u_ext, conv_w[:, None, :], window_strides=(1,), padding='VALID',
                                 dimension_numbers=('NWC', 'WIO', 'NWC'), feature_group_count=CONV_CH)
    y = jax.nn.silu(layernorm(y + conv_b, ln_g, ln_b))
    return y @ w_conv_out + b_conv_out


def fox_prompt(q, k, v, logf):
    n, s = q.shape[0], q.shape[1]
    nq = s // Q_BLOCK
    F = jnp.cumsum(logf, axis=1)
    Fk = F.transpose(0, 2, 1)
    kpos = jnp.arange(s)
    qb = q.reshape(n, nq, Q_BLOCK, FOX_HEADS, FOX_HEAD_DIM).transpose(1, 0, 2, 3, 4)
    Fqb = F.reshape(n, nq, Q_BLOCK, FOX_HEADS).transpose(1, 0, 3, 2)
    qpos = kpos.reshape(nq, Q_BLOCK)
    scale = FOX_HEAD_DIM ** -0.5

    def block(args):
        qi, Fqi, pi = args
        sc = jnp.einsum('bqhd,bkhd->bhqk', qi, k, preferred_element_type=jnp.float32) * scale
        sc = sc + Fqi[..., None] - Fk[:, :, None, :]
        sc = jnp.where(kpos[None, :] <= pi[:, None], sc, -jnp.inf)
        p = jax.nn.softmax(sc, axis=-1).astype(v.dtype)
        return jnp.einsum('bhqk,bkhd->bqhd', p, v)

    o = lax.map(block, (qb, Fqb, qpos))
    return o.transpose(1, 0, 2, 3, 4).reshape(n, s, FOX_WIDTH)


def fox_sample(q, k_new, v_new, logf_new, cache_k, cache_v, cache_logf, page_table):
    n, t = q.shape[0], q.shape[1]
    n_pages = page_table.shape[1]
    past = n_pages * PAGE_SIZE
    logf_past = cache_logf[page_table].reshape(n, past, FOX_HEADS)
    F_past = jnp.cumsum(logf_past, axis=1)
    F_new = F_past[:, -1:, :] + jnp.cumsum(logf_new, axis=1)
    Fq = F_new.transpose(0, 2, 1)
    Fp = F_past.reshape(n, n_pages, PAGE_SIZE, FOX_HEADS).transpose(1, 0, 3, 2)
    scale = FOX_HEAD_DIM ** -0.5

    def update(carry, sc, vals):
        m, l, acc = carry
        m_new = jnp.maximum(m, sc.max(-1))
        corr = jnp.exp(m - m_new)
        p = jnp.exp(sc - m_new[..., None])
        l = l * corr + p.sum(-1)
        acc = acc * corr[..., None] + jnp.einsum('bhqk,bkhd->bhqd', p, vals.astype(jnp.float32))
        return (m_new, l, acc)

    def step(carry, xs):
        phys, Fpage = xs
        kp = cache_k[phys]
        vp = cache_v[phys]
        sc = jnp.einsum('bqhd,bkhd->bhqk', q, kp, preferred_element_type=jnp.float32) * scale
        sc = sc + Fq[..., None] - Fpage[:, :, None, :]
        return update(carry, sc, vp), None

    init = (jnp.full((n, FOX_HEADS, t), -jnp.inf, jnp.float32),
            jnp.zeros((n, FOX_HEADS, t), jnp.float32),
            jnp.zeros((n, FOX_HEADS, t, FOX_HEAD_DIM), jnp.float32))
    carry, _ = lax.scan(step, init, (page_table.T, Fp))
    sc = jnp.einsum('bqhd,bkhd->bhqk', q, k_new, preferred_element_type=jnp.float32) * scale
    sc = sc + Fq[..., None] - Fq[:, :, None, :]
    causal = jnp.tril(jnp.ones((t, t), dtype=bool))
    sc = jnp.where(causal, sc, -jnp.inf)
    m, l, acc = update(carry, sc, v_new)
    o = (acc / l[..., None]).transpose(0, 2, 1, 3)
    return o.reshape(n, t, FOX_WIDTH).astype(q.dtype)


def mem_kv(mem, w_mem_kv):
    n, m = mem.shape[0], mem.shape[1]
    kv = mem @ w_mem_kv
    mk = kv[..., :MEM_WIDTH].reshape(n, m, MEM_HEADS, MEM_HEAD_DIM)
    mv = kv[..., MEM_WIDTH:].reshape(n, m, MEM_HEADS, MEM_HEAD_DIM)
    return mk, mv


def mem_attend(q, mk, mv):
    n, L = q.shape[0], q.shape[1]
    sc = jnp.einsum('bqhd,bmhd->bhqm', q, mk, preferred_element_type=jnp.float32) * MEM_HEAD_DIM ** -0.5
    p = jax.nn.softmax(sc, axis=-1).astype(mv.dtype)
    return jnp.einsum('bhqm,bmhd->bqhd', p, mv).reshape(n, L, MEM_WIDTH)


def moe(h, w_router, b_router, w_gate, b_gate, w_up, b_up, w_down, b_down):
    n, d = h.shape
    logits = jnp.dot(h, w_router, preferred_element_type=jnp.float32) + b_router.astype(jnp.float32)
    top_v, top_e = lax.top_k(logits, TOP_K)
    gate = jax.nn.softmax(top_v, axis=-1)
    flat_e = top_e.reshape(-1).astype(jnp.int32)
    flat_tok = jnp.repeat(jnp.arange(n, dtype=jnp.int32), TOP_K)
    flat_g = gate.reshape(-1)
    order = jnp.argsort(flat_e)
    e_sorted = flat_e[order]
    counts = jnp.zeros((N_EXPERTS,), jnp.int32).at[flat_e].add(1)
    starts = jnp.cumsum(counts) - counts
    padded = (counts + MOE_BLOCK - 1) // MOE_BLOCK * MOE_BLOCK
    pad_ends = jnp.cumsum(padded)
    pad_starts = pad_ends - padded
    rank = jnp.arange(n * TOP_K, dtype=jnp.int32) - starts[e_sorted]
    dest = pad_starts[e_sorted] + rank
    n_blocks = -(-(n * TOP_K) // MOE_BLOCK) + N_EXPERTS
    rows = n_blocks * MOE_BLOCK
    row_tok = jnp.full((rows,), n, jnp.int32).at[dest].set(flat_tok[order])
    row_g = jnp.zeros((rows,), jnp.float32).at[dest].set(flat_g[order])
    blk_start = jnp.arange(n_blocks, dtype=jnp.int32) * MOE_BLOCK
    blk_e = jnp.minimum(jnp.searchsorted(pad_ends, blk_start, side='right'), N_EXPERTS - 1)
    h_pad = jnp.concatenate([h, jnp.zeros((1, d), h.dtype)], axis=0)

    def step(acc, xs):
        e, tok, g = xs
        xb = h_pad[tok]
        a = jnp.minimum(xb @ w_gate[e] + b_gate[e], SWIGLU_LIMIT)
        u = jnp.clip(xb @ w_up[e] + b_up[e], -SWIGLU_LIMIT, SWIGLU_LIMIT)
        hid = (u + 1.0) * a * jax.nn.sigmoid(SWIGLU_ALPHA * a)
        yb = (hid @ w_down[e] + b_down[e]).astype(jnp.float32) * g[:, None]
        return acc.at[tok].add(yb), None

    acc, _ = lax.scan(step, jnp.zeros((n + 1, d), jnp.float32),
                      (blk_e, row_tok.reshape(n_blocks, MOE_BLOCK), row_g.reshape(n_blocks, MOE_BLOCK)))
    return acc[:n].astype(h.dtype)


def merge_and_channel(x, gates, y_conv, y_fox, y_mem, w_fox_out, w_mem_out, w_out, b_out, ln1_g, ln1_b,
                      w_router, b_router, w_gate, b_gate, w_up, b_up, w_down, b_down, ln2_g, ln2_b):
    mix = gates[..., 0, :] * y_conv + gates[..., 1, :] * (y_fox @ w_fox_out) + gates[..., 2, :] * (y_mem @ w_mem_out)
    h = layernorm(DEEPNORM_ALPHA * x + (mix @ w_out + b_out), ln1_g, ln1_b)
    n, L, d = h.shape
    f = moe(h.reshape(n * L, d), w_router, b_router, w_gate, b_gate, w_up, b_up, w_down, b_down).reshape(n, L, d)
    return layernorm(DEEPNORM_ALPHA * h + f, ln2_g, ln2_b)


def setup_inputs(seed: int = 0) -> dict:
    key = jax.random.key(seed)
    ks = iter(jax.random.split(key, 40))
    nrm = lambda shape, scale: jax.random.normal(next(ks), shape, jnp.float32) * scale
    n_pages = PAST_LEN // PAGE_SIZE
    n_used = DEC_BATCH * n_pages
    n_phys = n_used + n_used // 4
    x_prompt = nrm((BATCH, SEQ, D_MODEL), 1.0)
    x_sample = nrm((DEC_BATCH, DEC_SEQ, D_MODEL), 1.0)
    mem_prompt = nrm((BATCH, MEM_TOKENS, D_MODEL), 1.0)
    cache_k = nrm((n_phys, PAGE_SIZE, FOX_HEADS, FOX_HEAD_DIM), 1.0)
    cache_v = nrm((n_phys, PAGE_SIZE, FOX_HEADS, FOX_HEAD_DIM), DEEPNORM_BETA)
    cache_logf = jax.nn.log_sigmoid(FORGET_BIAS_INIT + nrm((n_phys, PAGE_SIZE, FOX_HEADS), 1.0))
    page_table = jax.random.permutation(next(ks), n_phys)[:n_used].reshape(DEC_BATCH, n_pages).astype(jnp.int32)
    cache_mem_k = nrm((DEC_BATCH, MEM_TOKENS, MEM_HEADS, MEM_HEAD_DIM), 1.0)
    cache_mem_v = nrm((DEC_BATCH, MEM_TOKENS, MEM_HEADS, MEM_HEAD_DIM), DEEPNORM_BETA)
    state_conv = nrm((DEC_BATCH, CONV_WIDTH - 1, CONV_CH), 0.5)
    in_scale = np.ones((IN_WIDTH,), np.float32)
    v0 = 2 * CONV_CH + 2 * FOX_WIDTH
    in_scale[v0:v0 + FOX_WIDTH] = DEEPNORM_BETA
    w_in = nrm((D_MODEL, IN_WIDTH), D_MODEL ** -0.5) * jnp.asarray(in_scale)
    b_in = nrm((IN_WIDTH,), 0.02)
    b_forget = FORGET_BIAS_INIT + nrm((FOX_HEADS,), 0.1)
    conv_w = nrm((CONV_WIDTH, CONV_CH), CONV_WIDTH ** -0.5)
    conv_b = nrm((CONV_CH,), 0.02)
    conv_ln_g = 1.0 + nrm((CONV_CH,), 0.02)
    conv_ln_b = nrm((CONV_CH,), 0.02)
    w_conv_out = nrm((CONV_CH, D_MODEL), CONV_CH ** -0.5 * DEEPNORM_BETA)
    b_conv_out = nrm((D_MODEL,), 0.02)
    w_fox_out = nrm((FOX_WIDTH, D_MODEL), FOX_WIDTH ** -0.5 * DEEPNORM_BETA)
    kv_scale = np.ones((2 * MEM_WIDTH,), np.float32)
    kv_scale[MEM_WIDTH:] = DEEPNORM_BETA
    w_mem_kv = nrm((D_MODEL, 2 * MEM_WIDTH), D_MODEL ** -0.5) * jnp.asarray(kv_scale)
    w_mem_out = nrm((MEM_WIDTH, D_MODEL), MEM_WIDTH ** -0.5 * DEEPNORM_BETA)
    w_out = nrm((D_MODEL, D_MODEL), D_MODEL ** -0.5 * DEEPNORM_BETA)
    b_out = nrm((D_MODEL,), 0.02)
    ln1_g = 1.0 + nrm((D_MODEL,), 0.02)
    ln1_b = nrm((D_MODEL,), 0.02)
    w_router = nrm((D_MODEL, N_EXPERTS), D_MODEL ** -0.5)
    b_router = nrm((N_EXPERTS,), 0.01)
    w_gate = nrm((N_EXPERTS, D_MODEL, D_FF), D_MODEL ** -0.5)
    b_gate = nrm((N_EXPERTS, D_FF), 0.02)
    w_up = nrm((N_EXPERTS, D_MODEL, D_FF), D_MODEL ** -0.5)
    b_up = nrm((N_EXPERTS, D_FF), 0.02)
    w_down = nrm((N_EXPERTS, D_FF, D_MODEL), D_FF ** -0.5 * DEEPNORM_BETA)
    b_down = nrm((N_EXPERTS, D_MODEL), 0.02)
    ln2_g = 1.0 + nrm((D_MODEL,), 0.02)
    ln2_b = nrm((D_MODEL,), 0.02)
    return {'x_prompt': x_prompt, 'x_sample': x_sample, 'mem_prompt': mem_prompt,
            'cache_k': cache_k, 'cache_v': cache_v, 'cache_logf': cache_logf, 'page_table': page_table,
            'cache_mem_k': cache_mem_k, 'cache_mem_v': cache_mem_v, 'state_conv': state_conv,
            'w_in': w_in, 'b_in': b_in, 'b_forget': b_forget, 'conv_w': conv_w, 'conv_b': conv_b,
            'conv_ln_g': conv_ln_g, 'conv_ln_b': conv_ln_b, 'w_conv_out': w_conv_out, 'b_conv_out': b_conv_out,
            'w_fox_out': w_fox_out, 'w_mem_kv': w_mem_kv, 'w_mem_out': w_mem_out, 'w_out': w_out, 'b_out': b_out,
            'ln1_g': ln1_g, 'ln1_b': ln1_b, 'w_router': w_router, 'b_router': b_router,
            'w_gate': w_gate, 'b_gate': b_gate, 'w_up': w_up, 'b_up': b_up, 'w_down': w_down, 'b_down': b_down,
            'ln2_g': ln2_g, 'ln2_b': ln2_b}


def reference(x_prompt, x_sample, mem_prompt, cache_k, cache_v, cache_logf, page_table, cache_mem_k, cache_mem_v,
              state_conv, w_in, b_in, b_forget, conv_w, conv_b, conv_ln_g, conv_ln_b, w_conv_out, b_conv_out,
              w_fox_out, w_mem_kv, w_mem_out, w_out, b_out, ln1_g, ln1_b, w_router, b_router,
              w_gate, b_gate, w_up, b_up, w_down, b_down, ln2_g, ln2_b):
    xp, xs = x_prompt, x_sample
    for _ in range(DEPTH):
        u, q, k, v, logf, qm, gates = project_inputs(xp, w_in, b_in, b_forget)
        u_ext = jnp.pad(u, ((0, 0), (CONV_WIDTH - 1, 0), (0, 0)))
        y_conv = conv_branch(u_ext, conv_w, conv_b, conv_ln_g, conv_ln_b, w_conv_out, b_conv_out)
        y_fox = fox_prompt(q, k, v, logf)
        mk, mv = mem_kv(mem_prompt, w_mem_kv)
        y_mem = mem_attend(qm, mk, mv)
        k_prompt, v_prompt, logf_prompt = k, v, logf
        mem_k_prompt, mem_v_prompt = mk, mv
        conv_prompt = u[:, u.shape[1] - (CONV_WIDTH - 1):, :]
        xp = merge_and_channel(xp, gates, y_conv, y_fox, y_mem, w_fox_out, w_mem_out, w_out, b_out, ln1_g, ln1_b,
                               w_router, b_router, w_gate, b_gate, w_up, b_up, w_down, b_down, ln2_g, ln2_b)
        us, qs, ks_, vs, logfs, qms, gates_s = project_inputs(xs, w_in, b_in, b_forget)
        us_ext = jnp.concatenate([state_conv.astype(us.dtype), us], axis=1)
        y_conv_s = conv_branch(us_ext, conv_w, conv_b, conv_ln_g, conv_ln_b, w_conv_out, b_conv_out)
        y_fox_s = fox_sample(qs, ks_, vs, logfs, cache_k, cache_v, cache_logf, page_table)
        y_mem_s = mem_attend(qms, cache_mem_k, cache_mem_v)
        k_sample, v_sample, logf_sample = ks_, vs, logfs
        conv_sample = us_ext[:, us_ext.shape[1] - (CONV_WIDTH - 1):, :]
        xs = merge_and_channel(xs, gates_s, y_conv_s, y_fox_s, y_mem_s, w_fox_out, w_mem_out, w_out, b_out, ln1_g, ln1_b,
                               w_router, b_router, w_gate, b_gate, w_up, b_up, w_down, b_down, ln2_g, ln2_b)
    y_prompt, y_sample = xp, xs
    return (y_prompt, y_sample, k_prompt, v_prompt, logf_prompt, mem_k_prompt, mem_v_prompt, conv_prompt,
            k_sample, v_sample, logf_sample, conv_sample)
```

```python
import functools

import numpy as np
import jax
import jax.numpy as jnp
from jax import lax
from jax.experimental import pallas as pl
from jax.experimental.pallas import tpu as pltpu

D_MODEL = 1024
CONV_CH = 512
CONV_WIDTH = 31
FOX_HEADS = 8
FOX_HEAD_DIM = 64
FOX_WIDTH = FOX_HEADS * FOX_HEAD_DIM
MEM_HEADS = 4
MEM_HEAD_DIM = 128
MEM_WIDTH = MEM_HEADS * MEM_HEAD_DIM
MEM_TOKENS = 256
N_EXPERTS = 32
TOP_K = 4
PAGE_SIZE = 128
SWIGLU_LIMIT = 7.0
SWIGLU_ALPHA = 1.702
LN_EPS = 1e-5
DEEPNORM_ALPHA = 2.0 ** 0.25

LANES = 128
SUBLANES = 8
HIST_ROWS = 32
MIB = 1024 * 1024

BF = jnp.bfloat16
F32 = jnp.float32
NEG_INF = float("-inf")


def _dot(a, b):
    return jnp.dot(a, b, preferred_element_type=F32)


def _dot_nt(a, b):
    return lax.dot_general(a, b, (((1,), (1,)), ((), ())), preferred_element_type=F32)


def _params(vmem_mib, n_axes, **kw):
    return pltpu.CompilerParams(dimension_semantics=("arbitrary",) * n_axes,
                                vmem_limit_bytes=vmem_mib * MIB, **kw)


def _full(shape):
    nd = len(shape)
    return pl.BlockSpec(shape, lambda *_: (0,) * nd)


def _split3(x):
    hi = x.astype(BF)
    r1 = x - hi.astype(F32)
    mid = r1.astype(BF)
    lo = (r1 - mid.astype(F32)).astype(BF)
    return hi, mid, lo


def _layernorm(x, g, b):
    mu = jnp.mean(x, axis=-1, keepdims=True)
    xc = x - mu
    var = jnp.mean(xc * xc, axis=-1, keepdims=True)
    return xc * lax.rsqrt(var + LN_EPS) * g + b


def _in_proj_kernel(x_ref, tri_ref, wglu_ref, wqkv_ref, wf_ref, wqm_ref, bglu_ref, bqkv_ref, bf_ref, bfg_ref,
                    bqm_ref, u_ref, q_ref, k_ref, v_ref, kb_ref, vb_ref, logf_ref, cum_ref, qm_ref, carry_ref,
                    *, tm, seq_len):
    i = pl.program_id(0)
    xb = x_ref[...].astype(BF)
    glu = _dot(xb, wglu_ref[...]) + bglu_ref[...]
    u_ref[...] = glu[:, :CONV_CH] * jax.nn.sigmoid(glu[:, CONV_CH:])
    qkv = _dot(xb, wqkv_ref[...]) + bqkv_ref[...]
    q_ref[...] = (qkv[:, :FOX_WIDTH] * (FOX_HEAD_DIM ** -0.5)).astype(BF)
    k = qkv[:, FOX_WIDTH:2 * FOX_WIDTH]
    v = qkv[:, 2 * FOX_WIDTH:]
    k_ref[...] = k
    v_ref[...] = v
    kb_ref[...] = k.astype(BF)
    vb_ref[...] = v.astype(BF)
    qm_ref[...] = (_dot(xb, wqm_ref[...]) + bqm_ref[...]).astype(BF)
    f = (_dot(xb, wf_ref[...]) + bf_ref[...]) + bfg_ref[...]
    lf = jnp.minimum(f, 0.0) - jnp.log1p(jnp.exp(-jnp.abs(f)))
    logf_ref[...] = lf[:, :FOX_HEADS]
    hi, mid, lo = _split3(lf)
    tri = tri_ref[...]
    cum = _dot(tri, hi) + _dot(tri, mid) + _dot(tri, lo)
    if seq_len > tm:
        @pl.when(i % (seq_len // tm) == 0)
        def _():
            carry_ref[...] = jnp.zeros_like(carry_ref)
        cum = cum + carry_ref[...]
        carry_ref[...] = cum[tm - 1:tm, :]
    cum_ref[...] = cum[:, :FOX_HEADS]


def _in_proj(x2d, seq_len, w):
    n = x2d.shape[0]
    tm = min(512, n)
    lc = min(seq_len, tm)
    r = np.arange(tm)
    tri = jnp.asarray(((r[None, :] <= r[:, None]) & (r[None, :] // lc == r[:, None] // lc)).astype(np.float32), BF)
    tok = lambda width: pl.BlockSpec((tm, width), lambda i: (i, 0))
    out_shape = (jax.ShapeDtypeStruct((n, CONV_CH), F32), jax.ShapeDtypeStruct((n, FOX_WIDTH), BF),
                 jax.ShapeDtypeStruct((n, FOX_WIDTH), F32), jax.ShapeDtypeStruct((n, FOX_WIDTH), F32),
                 jax.ShapeDtypeStruct((n, FOX_WIDTH), BF), jax.ShapeDtypeStruct((n, FOX_WIDTH), BF),
                 jax.ShapeDtypeStruct((n, FOX_HEADS), F32), jax.ShapeDtypeStruct((n, FOX_HEADS), F32),
                 jax.ShapeDtypeStruct((n, MEM_WIDTH), BF))
    ins = (x2d, tri, w["wglu"], w["wqkv"], w["wf"], w["wqm"], w["bglu"], w["bqkv"], w["bf"], w["bfg"], w["bqm"])
    return pl.pallas_call(
        functools.partial(_in_proj_kernel, tm=tm, seq_len=seq_len),
        grid=(n // tm,),
        in_specs=[tok(D_MODEL)] + [_full(a.shape) for a in ins[1:]],
        out_specs=(tok(CONV_CH), tok(FOX_WIDTH), tok(FOX_WIDTH), tok(FOX_WIDTH), tok(FOX_WIDTH), tok(FOX_WIDTH),
                   tok(FOX_HEADS), tok(FOX_HEADS), tok(MEM_WIDTH)),
        out_shape=out_shape,
        scratch_shapes=[pltpu.VMEM((1, LANES), F32)],
        compiler_params=_params(56, 1),
    )(*ins)


def _conv_post(y, cb_ref, g_ref, b_ref):
    y = _layernorm(y + cb_ref[...], g_ref[...], b_ref[...])
    return (y * jax.nn.sigmoid(y)).astype(BF)


def _conv_prompt_kernel(u_ref, prev_ref, hist_ref, w_ref, cb_ref, g_ref, b_ref, o_ref, win_ref, y_ref, *, tm):
    i = pl.program_id(1)
    win_ref[0:HIST_ROWS, :] = jnp.where(i == 0, hist_ref[0], prev_ref[0])
    win_ref[HIST_ROWS:, :] = u_ref[0]
    first = HIST_ROWS - (CONV_WIDTH - 1)
    for c in range(CONV_CH // LANES):
        cs = slice(c * LANES, (c + 1) * LANES)
        acc = jnp.zeros((tm, LANES), F32)
        for j in range(CONV_WIDTH):
            acc = acc + w_ref[j:j + 1, cs] * win_ref[first + j:first + j + tm, cs]
        y_ref[:, cs] = acc
    o_ref[0] = _conv_post(y_ref[...], cb_ref, g_ref, b_ref)


def _conv_prompt(u3, hist, cw):
    b, l, _ = u3.shape
    tm = 256
    per = tm // HIST_ROWS
    vec = _full((1, CONV_CH))
    return pl.pallas_call(
        functools.partial(_conv_prompt_kernel, tm=tm),
        grid=(b, l // tm),
        in_specs=[pl.BlockSpec((1, tm, CONV_CH), lambda bi, i: (bi, i, 0)),
                  pl.BlockSpec((1, HIST_ROWS, CONV_CH), lambda bi, i: (bi, jnp.maximum(i * per - 1, 0), 0)),
                  pl.BlockSpec((1, HIST_ROWS, CONV_CH), lambda bi, i: (bi, 0, 0)),
                  _full((HIST_ROWS, CONV_CH)), vec, vec, vec],
        out_specs=pl.BlockSpec((1, tm, CONV_CH), lambda bi, i: (bi, i, 0)),
        out_shape=jax.ShapeDtypeStruct((b, l, CONV_CH), BF),
        scratch_shapes=[pltpu.VMEM((tm + HIST_ROWS, CONV_CH), F32), pltpu.VMEM((tm, CONV_CH), F32)],
        compiler_params=_params(32, 2),
    )(u3, u3, hist, cw["w"], cw["cb"], cw["g"], cw["b"])


def _conv_sample_kernel(x_ref, w_ref, cb_ref, g_ref, b_ref, o_ref, *, steps):
    for t in range(steps):
        acc = jnp.zeros(x_ref.shape[1:], F32)
        for j in range(CONV_WIDTH):
            acc = acc + w_ref[j:j + 1, :] * x_ref[t + j]
        o_ref[t] = _conv_post(acc, cb_ref, g_ref, b_ref)


def _conv_sample(u_ext_t, cw):
    rows, b, _ = u_ext_t.shape
    steps = rows - (CONV_WIDTH - 1)
    bb = min(64, b)
    vec = _full((1, CONV_CH))
    return pl.pallas_call(
        functools.partial(_conv_sample_kernel, steps=steps),
        grid=(b // bb,),
        in_specs=[pl.BlockSpec((rows, bb, CONV_CH), lambda i: (0, i, 0)), _full((HIST_ROWS, CONV_CH)), vec, vec, vec],
        out_specs=pl.BlockSpec((steps, bb, CONV_CH), lambda i: (0, i, 0)),
        out_shape=jax.ShapeDtypeStruct((steps, b, CONV_CH), BF),
        compiler_params=_params(32, 1),
    )(u_ext_t, cw["w"], cw["cb"], cw["g"], cw["b"])


def _fox_prompt_kernel(qi_ref, kj_ref, q_ref, k_ref, v_ref, fq_ref, fk_ref, o_ref, m_ref, l_ref, acc_ref,
                       *, tile, sub):
    t = pl.program_id(2)
    qi = qi_ref[t]
    kj = kj_ref[t]
    nsub = tile // sub
    lane = lax.broadcasted_iota(jnp.int32, (1, LANES), 1)

    @pl.when(kj == 0)
    def _():
        m_ref[...] = jnp.full_like(m_ref, NEG_INF)
        l_ref[...] = jnp.zeros_like(l_ref)
        acc_ref[...] = jnp.zeros_like(acc_ref)

    def attend(hh, i2, j2, masked):
        rows = slice(i2 * sub, (i2 + 1) * sub)
        cols = slice(j2 * sub, (j2 + 1) * sub)
        in_head = (lane >= hh * FOX_HEAD_DIM) & (lane < (hh + 1) * FOX_HEAD_DIM)
        q = q_ref[0, rows, :]
        qh = jnp.where(in_head, q, jnp.zeros_like(q))
        s = _dot_nt(qh, k_ref[0, cols, :])
        s = s + (fq_ref[0, 0, rows, hh:hh + 1] - fk_ref[0, 0, hh:hh + 1, cols])
        if masked:
            r_id = lax.broadcasted_iota(jnp.int32, (sub, sub), 0)
            c_id = lax.broadcasted_iota(jnp.int32, (sub, sub), 1)
            s = jnp.where(c_id <= r_id, s, NEG_INF)
        m_prev = m_ref[hh, rows, :]
        m_next = jnp.maximum(m_prev, jnp.max(s, axis=1, keepdims=True))
        alpha = jnp.exp(m_prev - m_next)
        p = jnp.exp(s - jnp.concatenate([m_next] * (sub // LANES), axis=1))
        l_ref[hh, rows, :] = alpha * l_ref[hh, rows, :] + jnp.sum(p, axis=1, keepdims=True)
        acc_ref[hh, rows, :] = alpha * acc_ref[hh, rows, :] + _dot(p.astype(BF), v_ref[0, cols, :])
        m_ref[hh, rows, :] = m_next

    @pl.when(kj < qi)
    def _():
        for hh in range(2):
            for i2 in range(nsub):
                for j2 in range(nsub):
                    attend(hh, i2, j2, False)

    @pl.when(kj == qi)
    def _():
        for hh in range(2):
            for i2 in range(nsub):
                for j2 in range(i2 + 1):
                    attend(hh, i2, j2, j2 == i2)
        out0 = acc_ref[0] / l_ref[0]
        out1 = acc_ref[1] / l_ref[1]
        o_ref[0] = jnp.where(lane < FOX_HEAD_DIM, out0, out1).astype(BF)


def _fox_prompt(q3, k3, v3, cum3):
    b, l, _ = q3.shape
    tile = min(1024, l)
    sub = min(512, tile)
    nq = l // tile
    pairs = FOX_HEADS // 2
    qi = np.concatenate([np.full(i + 1, i) for i in range(nq)]).astype(np.int32)
    kj = np.concatenate([np.arange(i + 1) for i in range(nq)]).astype(np.int32)
    fq = cum3.reshape(b, l, pairs, 2).transpose(0, 2, 1, 3)
    fk = cum3.reshape(b, l, pairs, 2).transpose(0, 2, 3, 1)
    grid_spec = pltpu.PrefetchScalarGridSpec(
        num_scalar_prefetch=2,
        grid=(b, pairs, len(qi)),
        in_specs=[pl.BlockSpec((1, tile, LANES), lambda bi, p, t, qi_r, kj_r: (bi, qi_r[t], p)),
                  pl.BlockSpec((1, tile, LANES), lambda bi, p, t, qi_r, kj_r: (bi, kj_r[t], p)),
                  pl.BlockSpec((1, tile, LANES), lambda bi, p, t, qi_r, kj_r: (bi, kj_r[t], p)),
                  pl.BlockSpec((1, 1, tile, 2), lambda bi, p, t, qi_r, kj_r: (bi, p, qi_r[t], 0)),
                  pl.BlockSpec((1, 1, 2, tile), lambda bi, p, t, qi_r, kj_r: (bi, p, 0, kj_r[t]))],
        out_specs=pl.BlockSpec((1, tile, LANES), lambda bi, p, t, qi_r, kj_r: (bi, qi_r[t], p)),
        scratch_shapes=[pltpu.VMEM((2, tile, LANES), F32)] * 3)
    return pl.pallas_call(
        functools.partial(_fox_prompt_kernel, tile=tile, sub=sub),
        grid_spec=grid_spec,
        out_shape=jax.ShapeDtypeStruct((b, l, FOX_WIDTH), BF),
        compiler_params=_params(40, 3),
    )(jnp.asarray(qi), jnp.asarray(kj), q3, k3, v3, fq, fk)


def _page_suffix_kernel(x_ref, upper_ref, ones_ref, suf_ref, tot_ref):
    hi, mid, lo = _split3(x_ref[...])
    up = upper_ref[...]
    on = ones_ref[...]
    suf_ref[...] = _dot(hi, up) + _dot(mid, up) + _dot(lo, up)
    tot_ref[...] = _dot(hi, on) + _dot(mid, on) + _dot(lo, on)


def _page_suffix(logf_t):
    rows = logf_t.shape[0]
    tr = min(2048, rows)
    kk = np.arange(PAGE_SIZE)
    upper = jnp.asarray((kk[:, None] > kk[None, :]).astype(np.float32), BF)
    ones = jnp.ones((PAGE_SIZE, PAGE_SIZE), BF)
    spec = pl.BlockSpec((tr, PAGE_SIZE), lambda i: (i, 0))
    return pl.pallas_call(
        _page_suffix_kernel,
        grid=(rows // tr,),
        in_specs=[spec, _full((PAGE_SIZE, PAGE_SIZE)), _full((PAGE_SIZE, PAGE_SIZE))],
        out_specs=(spec, spec),
        out_shape=(jax.ShapeDtypeStruct((rows, PAGE_SIZE), F32),) * 2,
        compiler_params=_params(32, 1),
    )(logf_t, upper, ones)


def _fox_sample_kernel(pt_ref, qrep_ref, cq_ref, ct_ref, kn_ref, vn_ref, *rest, pages, steps):
    k_refs = rest[:pages]
    v_refs = rest[pages:2 * pages]
    suf_refs = rest[2 * pages:3 * pages]
    tot_refs = rest[3 * pages:4 * pages]
    o_ref, m_ref, l_ref, acc_ref, carry_ref = rest[4 * pages:]
    j = pl.program_id(1)
    rows = steps * FOX_HEADS
    row_id = lax.broadcasted_iota(jnp.int32, (rows, FOX_WIDTH), 0)
    col_id = lax.broadcasted_iota(jnp.int32, (rows, FOX_WIDTH), 1)
    head_mask = (col_id // FOX_HEAD_DIM) == (row_id % FOX_HEADS)
    qrep = qrep_ref[0]
    qe = jnp.where(head_mask, qrep, jnp.zeros_like(qrep))
    cq = cq_ref[0]

    def update(s, vals):
        m_prev = m_ref[...]
        m_next = jnp.maximum(m_prev, jnp.max(s, axis=1, keepdims=True))
        alpha = jnp.exp(m_prev - m_next)
        p = jnp.exp(s - jnp.concatenate([m_next] * (s.shape[1] // LANES), axis=1))
        l_ref[...] = alpha * l_ref[...] + jnp.sum(p, axis=1, keepdims=True)
        pv = None
        for r, vr in enumerate(vals):
            term = _dot(p[:, r * LANES:(r + 1) * LANES].astype(BF), vr)
            pv = term if pv is None else pv + term
        acc_ref[...] = jnp.concatenate([alpha] * (FOX_WIDTH // LANES), axis=1) * acc_ref[...] + pv
        m_ref[...] = m_next

    @pl.when(j == 0)
    def _():
        m_ref[...] = jnp.full_like(m_ref, NEG_INF)
        l_ref[...] = jnp.zeros_like(l_ref)
        acc_ref[...] = jnp.zeros_like(acc_ref)
        carry_ref[...] = jnp.zeros_like(carry_ref)
        s = _dot_nt(qe, kn_ref[0]) + (cq - jnp.concatenate([ct_ref[0]] * steps, axis=0))
        r_id = lax.broadcasted_iota(jnp.int32, (rows, LANES), 0)
        c_id = lax.broadcasted_iota(jnp.int32, (rows, LANES), 1)
        s = jnp.where(c_id <= r_id // FOX_HEADS, s, NEG_INF)
        update(s, [vn_ref[0]])

    carry = carry_ref[...]
    scores = [None] * pages
    for r in reversed(range(pages)):
        bias = cq + carry + jnp.concatenate([suf_refs[r][0]] * steps, axis=0)
        scores[r] = _dot_nt(qe, k_refs[r][0].astype(BF)) + bias
        carry = carry + jnp.concatenate([tot_refs[r][0]] * steps, axis=0)
    carry_ref[...] = carry
    update(jnp.concatenate(scores, axis=1), [vr[0].astype(BF) for vr in v_refs])

    @pl.when(j == pl.num_programs(1) - 1)
    def _():
        o = jnp.where(head_mask, acc_ref[...] / jnp.concatenate([l_ref[...]] * (FOX_WIDTH // LANES), axis=1), 0.0)
        o_ref[0] = jnp.sum(o.reshape(steps, FOX_HEADS, FOX_WIDTH), axis=1).astype(BF)


def _fox_sample(q3, kn3, vn3, cum3, cache_k, cache_v, suf, tot, page_table):
    db, steps, _ = q3.shape
    n_phys = cache_k.shape[0]
    n_pages = page_table.shape[1]
    pages = 8
    rows = steps * FOX_HEADS
    qrep = jnp.repeat(q3, FOX_HEADS, axis=1)
    cq = jnp.broadcast_to(cum3.reshape(db, rows, 1), (db, rows, LANES))
    ct = jnp.pad(cum3.transpose(0, 2, 1), ((0, 0), (0, 0), (0, LANES - steps)))
    kn = jnp.pad(kn3, ((0, 0), (0, LANES - steps), (0, 0)))
    vn = jnp.pad(vn3, ((0, 0), (0, LANES - steps), (0, 0)))
    ck = cache_k.reshape(n_phys, PAGE_SIZE, FOX_WIDTH)
    cv = cache_v.reshape(n_phys, PAGE_SIZE, FOX_WIDTH)
    n_steps = n_pages // pages

    def page_map(r):
        return lambda bi, j, pt: (pt[bi, (n_steps - 1 - j) * pages + r], 0, 0)

    seq = lambda shape: pl.BlockSpec((1,) + shape, lambda bi, j, pt: (bi, 0, 0))
    in_specs = [seq((rows, FOX_WIDTH)), seq((rows, LANES)), seq((FOX_HEADS, LANES)),
                seq((LANES, FOX_WIDTH)), seq((LANES, FOX_WIDTH))]
    in_specs += [pl.BlockSpec((1, PAGE_SIZE, FOX_WIDTH), page_map(r)) for r in range(pages)] * 2
    in_specs += [pl.BlockSpec((1, FOX_HEADS, PAGE_SIZE), page_map(r)) for r in range(pages)] * 2
    grid_spec = pltpu.PrefetchScalarGridSpec(
        num_scalar_prefetch=1,
        grid=(db, n_steps),
        in_specs=in_specs,
        out_specs=pl.BlockSpec((1, steps, FOX_WIDTH), lambda bi, j, pt: (bi, 0, 0)),
        scratch_shapes=[pltpu.VMEM((rows, LANES), F32), pltpu.VMEM((rows, LANES), F32),
                        pltpu.VMEM((rows, FOX_WIDTH), F32), pltpu.VMEM((rows, LANES), F32)])
    return pl.pallas_call(
        functools.partial(_fox_sample_kernel, pages=pages, steps=steps),
        grid_spec=grid_spec,
        out_shape=jax.ShapeDtypeStruct((db, steps, FOX_WIDTH), BF),
        compiler_params=_params(40, 2),
    )(page_table, qrep, cq, ct, kn, vn, *([ck] * pages), *([cv] * pages), *([suf] * pages), *([tot] * pages))


def _mem_kv_kernel(m_ref, w_ref, k_ref, v_ref):
    kv = _dot(m_ref[...].astype(BF), w_ref[...])
    k_ref[...] = kv[:, :MEM_WIDTH]
    v_ref[...] = kv[:, MEM_WIDTH:]


def _mem_kv(mem2d, w_bf):
    n = mem2d.shape[0]
    tm = min(256, n)
    out = pl.BlockSpec((tm, MEM_WIDTH), lambda i: (i, 0))
    return pl.pallas_call(
        _mem_kv_kernel,
        grid=(n // tm,),
        in_specs=[pl.BlockSpec((tm, D_MODEL), lambda i: (i, 0)), _full(w_bf.shape)],
        out_specs=(out, out),
        out_shape=(jax.ShapeDtypeStruct((n, MEM_WIDTH), F32),) * 2,
        compiler_params=_params(32, 1),
    )(mem2d, w_bf)


def _mem_attend_kernel(q_ref, k_ref, v_ref, o_ref, *, bb):
    for b in range(bb):
        for h in range(MEM_HEADS):
            hs = slice(h * MEM_HEAD_DIM, (h + 1) * MEM_HEAD_DIM)
            s = _dot_nt(q_ref[b, :, hs], k_ref[b, :, hs].astype(BF)) * (MEM_HEAD_DIM ** -0.5)
            p = jnp.exp(s - jnp.max(s, axis=1, keepdims=True))
            den = jnp.sum(p, axis=1, keepdims=True)
            o_ref[b, :, hs] = (_dot(p.astype(BF), v_ref[b, :, hs].astype(BF)) / den).astype(BF)


def _mem_attend(qm3, mk3, mv3, bb, tq):
    b, l, _ = qm3.shape
    kv = pl.BlockSpec((bb, MEM_TOKENS, MEM_WIDTH), lambda bi, i: (bi, 0, 0))
    qs = pl.BlockSpec((bb, tq, MEM_WIDTH), lambda bi, i: (bi, i, 0))
    return pl.pallas_call(
        functools.partial(_mem_attend_kernel, bb=bb),
        grid=(b // bb, l // tq),
        in_specs=[qs, kv, kv],
        out_specs=qs,
        out_shape=jax.ShapeDtypeStruct((b, l, MEM_WIDTH), BF),
        compiler_params=_params(40, 2),
    )(qm3, mk3, mv3)


def _merge_kernel(x_ref, yc_ref, yf_ref, ym_ref, wg_ref, bg_ref, wc_ref, bc_ref, wfo_ref, wmo_ref, wo_ref, bo_ref,
                  g1_ref, b1_ref, wrh_ref, wrl_ref, br_ref, h_ref, e_ref, gate_ref, *, tm):
    x = x_ref[...]
    xb = x.astype(BF)

    def gate(c):
        cs = slice(c * D_MODEL, (c + 1) * D_MODEL)
        return jax.nn.sigmoid(_dot(xb, wg_ref[:, cs]) + bg_ref[:, cs])

    mix = gate(0) * (_dot(yc_ref[...], wc_ref[...]) + bc_ref[...])
    mix = mix + gate(1) * _dot(yf_ref[...], wfo_ref[...])
    mix = mix + gate(2) * _dot(ym_ref[...], wmo_ref[...])
    pre = DEEPNORM_ALPHA * x + (_dot(mix.astype(BF), wo_ref[...]) + bo_ref[...])
    h = _layernorm(pre, g1_ref[...], b1_ref[...])
    for j in range(D_MODEL // LANES):
        h_ref[pl.ds(j, tm, stride=SUBLANES), :] = h[:, j * LANES:(j + 1) * LANES]

    h_hi = h.astype(BF)
    h_lo = (h - h_hi.astype(F32)).astype(BF)
    logits = _dot(h_hi, wrh_ref[...]) + _dot(h_hi, wrl_ref[...]) + _dot(h_lo, wrh_ref[...]) + br_ref[...]
    lane = lax.broadcasted_iota(jnp.int32, (tm, LANES), 1)
    lane_f = lane.astype(F32)
    work = jnp.where(lane < N_EXPERTS, logits, NEG_INF)
    vals, idxs = [], []
    for _ in range(TOP_K):
        mx = jnp.max(work, axis=1, keepdims=True)
        idx = jnp.min(jnp.where(work == mx, lane_f, float(LANES)), axis=1, keepdims=True)
        vals.append(mx)
        idxs.append(idx)
        work = jnp.where(lane_f == idx, NEG_INF, work)
    exps = [jnp.exp(v - vals[0]) for v in vals]
    den = exps[0] + exps[1] + exps[2] + exps[3]
    e_out = jnp.zeros((tm, LANES), F32)
    g_out = jnp.zeros((tm, LANES), F32)
    for kk in range(TOP_K):
        e_out = jnp.where(lane == kk, idxs[kk], e_out)
        g_out = jnp.where(lane == kk, exps[kk] / den, g_out)
    e_ref[...] = e_out[:, :TOP_K].astype(jnp.int32)
    gate_ref[...] = g_out[:, :TOP_K]


def _merge(x2d, yc, yf, ym, w):
    n = x2d.shape[0]
    tm = min(256, n)
    tok = lambda width: pl.BlockSpec((tm, width), lambda i: (i, 0))
    ws = (w["wg"], w["bg"], w["wc"], w["bc"], w["wfo"], w["wmo"], w["wo"], w["bo"], w["g1"], w["b1"],
          w["wrh"], w["wrl"], w["br"])
    return pl.pallas_call(
        functools.partial(_merge_kernel, tm=tm),
        grid=(n // tm,),
        in_specs=[tok(D_MODEL), tok(CONV_CH), tok(FOX_WIDTH), tok(MEM_WIDTH)] + [_full(a.shape) for a in ws],
        out_specs=(pl.BlockSpec((tm * SUBLANES, LANES), lambda i: (i, 0)), tok(TOP_K), tok(TOP_K)),
        out_shape=(jax.ShapeDtypeStruct((n * SUBLANES, LANES), F32), jax.ShapeDtypeStruct((n, TOP_K), jnp.int32),
                   jax.ShapeDtypeStruct((n, TOP_K), F32)),
        compiler_params=_params(56, 1),
    )(x2d, yc, yf, ym, *ws)


MOE_ROWS = 256
IDX_SLOTS = 3


def _moe_kernel(blk_e_ref, nvalid_ref, nreal_ref, rows_hbm, h_hbm, wg_ref, bg_ref, wu_ref, bu_ref, wd_ref, bd_ref,
                y_hbm, idx_ref, xbuf, ybuf, wgb, wub, wdb, idx_sem, in_sem, out_sem):
    i = pl.program_id(0)
    nreal = nreal_ref[0]
    tile = SUBLANES

    def idx_copy(blk):
        slot = blk % IDX_SLOTS
        return pltpu.make_async_copy(rows_hbm.at[blk], idx_ref.at[slot], idx_sem.at[slot])

    def row_index(blk, r):
        return idx_ref[blk % IDX_SLOTS, r // LANES, r % LANES]

    def gather_row(blk, r):
        tok = row_index(blk, r) // TOP_K
        return pltpu.make_async_copy(h_hbm.at[pl.ds(pl.multiple_of(tok * tile, tile), tile)],
                                     xbuf.at[blk % 2, pl.ds(pl.multiple_of(r * tile, tile), tile)],
                                     in_sem.at[blk % 2])

    def scatter_row(blk, r):
        dst = row_index(blk, r)
        return pltpu.make_async_copy(ybuf.at[blk % 2, pl.ds(pl.multiple_of(r * tile, tile), tile)],
                                     y_hbm.at[pl.ds(pl.multiple_of(dst * tile, tile), tile)],
                                     out_sem.at[blk % 2])

    def start_gather(blk):
        def body(r, c):
            gather_row(blk, r).start()
            return c
        lax.fori_loop(0, MOE_ROWS, body, 0)

    def wait_rows(sem, slot):
        pltpu.make_async_copy(xbuf.at[slot], ybuf.at[slot], sem.at[slot]).wait()

    def wait_scatter(blk):
        slot = blk % 2
        nv = nvalid_ref[blk]

        @pl.when(nv == MOE_ROWS)
        def _():
            wait_rows(out_sem, slot)

        @pl.when(nv < MOE_ROWS)
        def _():
            def body(r, c):
                pltpu.make_async_copy(ybuf.at[slot, pl.ds(0, tile)], y_hbm.at[pl.ds(0, tile)],
                                      out_sem.at[slot]).wait()
                return c
            lax.fori_loop(0, nv, body, 0)

    @pl.when((i == 0) & (nreal > 0))
    def _():
        idx_copy(0).start()
        idx_copy(0).wait()
        start_gather(0)

        @pl.when(nreal > 1)
        def _():
            idx_copy(1).start()

    @pl.when(i + 1 < nreal)
    def _():
        idx_copy(i + 1).wait()
        start_gather(i + 1)

        @pl.when(i + 2 < nreal)
        def _():
            idx_copy(i + 2).start()

    changed = (i == 0) | (blk_e_ref[i] != blk_e_ref[jnp.maximum(i - 1, 0)])

    @pl.when(changed & (i < nreal))
    def _():
        wgb[...] = wg_ref[0].astype(BF)
        wub[...] = wu_ref[0].astype(BF)
        wdb[...] = wd_ref[0].astype(BF)

    @pl.when(i < nreal)
    def _():
        slot = i % 2
        wait_rows(in_sem, slot)

        @pl.when(i >= 2)
        def _():
            wait_scatter(i - 2)

        x = jnp.concatenate([xbuf[slot, pl.ds(j, MOE_ROWS, stride=SUBLANES), :] for j in range(D_MODEL // LANES)],
                            axis=1).astype(BF)
        a = jnp.minimum(_dot(x, wgb[...]) + bg_ref[0], SWIGLU_LIMIT)
        u = jnp.clip(_dot(x, wub[...]) + bu_ref[0], -SWIGLU_LIMIT, SWIGLU_LIMIT)
        hid = (u + 1.0) * a * jax.nn.sigmoid(SWIGLU_ALPHA * a)
        y = _dot(hid.astype(BF), wdb[...]) + bd_ref[0]
        for j in range(D_MODEL // LANES):
            ybuf[slot, pl.ds(j, MOE_ROWS, stride=SUBLANES), :] = y[:, j * LANES:(j + 1) * LANES]

        def body(r, c):
            scatter_row(i, r).start()
            return c
        lax.fori_loop(0, nvalid_ref[i], body, 0)

        @pl.when(i == nreal - 1)
        def _():
            @pl.when(i >= 1)
            def _():
                wait_scatter(i - 1)
            wait_scatter(i)


def _moe(h_rows, top_e, w):
    n = top_e.shape[0]
    flat_e = top_e.reshape(-1)
    n_flat = n * TOP_K
    n_blocks = -(-n_flat // MOE_ROWS) + N_EXPERTS
    order = jnp.argsort(flat_e).astype(jnp.int32)
    counts = jnp.sum((flat_e[:, None] == jnp.arange(N_EXPERTS, dtype=jnp.int32)[None, :]).astype(jnp.int32), axis=0)
    starts = jnp.cumsum(counts) - counts
    padded = (counts + MOE_ROWS - 1) // MOE_ROWS * MOE_ROWS
    pad_ends = jnp.cumsum(padded)
    pad_starts = pad_ends - padded
    blk_start = jnp.arange(n_blocks, dtype=jnp.int32) * MOE_ROWS
    blk_e = jnp.minimum(jnp.sum((pad_ends[None, :] <= blk_start[:, None]).astype(jnp.int32), axis=1),
                        N_EXPERTS - 1).astype(jnp.int32)
    nreal = (pad_ends[-1] // MOE_ROWS).astype(jnp.int32).reshape(1)
    r = jnp.arange(n_blocks * MOE_ROWS, dtype=jnp.int32)
    e_r = jnp.repeat(blk_e, MOE_ROWS)
    rank = r - pad_starts[e_r]
    valid = (rank < counts[e_r]) & (r < pad_ends[-1])
    src = jnp.clip(starts[e_r] + rank, 0, n_flat - 1)
    rows = jnp.where(valid, order[src], 0).astype(jnp.int32)
    rows = jnp.pad(rows.reshape(n_blocks, MOE_ROWS // LANES, LANES),
                   ((0, 0), (0, SUBLANES - MOE_ROWS // LANES), (0, 0)))
    nvalid = jnp.sum(valid.reshape(n_blocks, MOE_ROWS).astype(jnp.int32), axis=1)
    out_rows = n_flat * SUBLANES
    wspec = pl.BlockSpec((1, D_MODEL, D_MODEL), lambda i, be, nv, nr: (be[i], 0, 0))
    bspec = pl.BlockSpec((1, 1, D_MODEL), lambda i, be, nv, nr: (be[i], 0, 0))
    any_spec = pl.BlockSpec(memory_space=pl.ANY)
    grid_spec = pltpu.PrefetchScalarGridSpec(
        num_scalar_prefetch=3,
        grid=(n_blocks,),
        in_specs=[any_spec, any_spec, wspec, bspec, wspec, bspec, wspec, bspec],
        out_specs=any_spec,
        scratch_shapes=[pltpu.SMEM((IDX_SLOTS, SUBLANES, LANES), jnp.int32),
                        pltpu.VMEM((2, MOE_ROWS * SUBLANES, LANES), F32),
                        pltpu.VMEM((2, MOE_ROWS * SUBLANES, LANES), F32),
                        pltpu.VMEM((D_MODEL, D_MODEL), BF), pltpu.VMEM((D_MODEL, D_MODEL), BF),
                        pltpu.VMEM((D_MODEL, D_MODEL), BF),
                        pltpu.SemaphoreType.DMA((IDX_SLOTS,)), pltpu.SemaphoreType.DMA((2,)),
                        pltpu.SemaphoreType.DMA((2,))])
    return pl.pallas_call(
        _moe_kernel,
        grid_spec=grid_spec,
        out_shape=jax.ShapeDtypeStruct((out_rows, LANES), F32),
        compiler_params=_params(56, 1),
    )(blk_e, nvalid, nreal, rows, h_rows, w["wgate"], w["bgate"], w["wup"], w["bup"], w["wdown"], w["bdown"])


def _combine_kernel(h_ref, y_ref, gate_ref, g2_ref, b2_ref, o_ref, *, tm):
    g = gate_ref[...]
    cols = []
    for j in range(D_MODEL // LANES):
        f = None
        for kk in range(TOP_K):
            term = g[:, kk:kk + 1] * y_ref[pl.ds(kk * SUBLANES + j, tm, stride=TOP_K * SUBLANES), :]
            f = term if f is None else f + term
        cols.append(DEEPNORM_ALPHA * h_ref[pl.ds(j, tm, stride=SUBLANES), :] + f)
    o_ref[...] = _layernorm(jnp.concatenate(cols, axis=1), g2_ref[...], b2_ref[...])


def _combine(h_rows, y_rows, gate, g2, b2):
    n = gate.shape[0]
    tm = min(256, n)
    return pl.pallas_call(
        functools.partial(_combine_kernel, tm=tm),
        grid=(n // tm,),
        in_specs=[pl.BlockSpec((tm * SUBLANES, LANES), lambda i: (i, 0)),
                  pl.BlockSpec((tm * TOP_K * SUBLANES, LANES), lambda i: (i, 0)),
                  pl.BlockSpec((tm, TOP_K), lambda i: (i, 0)), _full((1, D_MODEL)), _full((1, D_MODEL))],
        out_specs=pl.BlockSpec((tm, D_MODEL), lambda i: (i, 0)),
        out_shape=jax.ShapeDtypeStruct((n, D_MODEL), F32),
        compiler_params=_params(40, 1),
    )(h_rows, y_rows, gate, g2, b2)


def _row(v):
    return v.reshape(1, -1).astype(F32)


def _prep_weights(w_in, b_in, b_forget, conv_w, conv_b, conv_ln_g, conv_ln_b, w_conv_out, b_conv_out, w_fox_out,
                  w_mem_kv, w_mem_out, w_out, b_out, ln1_g, ln1_b, w_router, b_router, w_gate, b_gate, w_up, b_up,
                  w_down, b_down, ln2_g, ln2_b):
    o_q = 2 * CONV_CH
    o_f = o_q + 3 * FOX_WIDTH
    o_qm = o_f + FOX_HEADS
    o_g = o_qm + MEM_WIDTH
    pad_f = LANES - FOX_HEADS
    proj = dict(
        wglu=w_in[:, :o_q].astype(BF), bglu=_row(b_in[:o_q]),
        wqkv=w_in[:, o_q:o_f].astype(BF), bqkv=_row(b_in[o_q:o_f]),
        wf=jnp.pad(w_in[:, o_f:o_qm], ((0, 0), (0, pad_f))).astype(BF),
        bf=_row(jnp.pad(b_in[o_f:o_qm], (0, pad_f))), bfg=_row(jnp.pad(b_forget, (0, pad_f))),
        wqm=w_in[:, o_qm:o_g].astype(BF), bqm=_row(b_in[o_qm:o_g]))
    conv = dict(w=jnp.pad(conv_w, ((0, HIST_ROWS - CONV_WIDTH), (0, 0))).astype(F32), cb=_row(conv_b),
                g=_row(conv_ln_g), b=_row(conv_ln_b))
    pad_r = LANES - N_EXPERTS
    wr = jnp.pad(w_router, ((0, 0), (0, pad_r)))
    wr_hi = wr.astype(BF)
    merge = dict(
        wg=w_in[:, o_g:].astype(BF), bg=_row(b_in[o_g:]), wc=w_conv_out.astype(BF), bc=_row(b_conv_out),
        wfo=w_fox_out.astype(BF), wmo=w_mem_out.astype(BF), wo=w_out.astype(BF), bo=_row(b_out),
        g1=_row(ln1_g), b1=_row(ln1_b), wrh=wr_hi, wrl=(wr - wr_hi.astype(F32)).astype(BF),
        br=_row(jnp.pad(b_router, (0, pad_r))))
    moe = dict(wgate=w_gate, bgate=b_gate.reshape(N_EXPERTS, 1, D_MODEL), wup=w_up,
               bup=b_up.reshape(N_EXPERTS, 1, D_MODEL), wdown=w_down, bdown=b_down.reshape(N_EXPERTS, 1, D_MODEL))
    return proj, conv, merge, moe, w_mem_kv.astype(BF), _row(ln2_g), _row(ln2_b)


def _channel(x2d, yc, yf, ym, merge_w, moe_w, g2, b2):
    h_rows, top_e, gate = _merge(x2d, yc, yf, ym, merge_w)
    y_rows = _moe(h_rows, top_e, moe_w)
    return _combine(h_rows, y_rows, gate, g2, b2)


def kernel(x_prompt, x_sample, mem_prompt, cache_k, cache_v, cache_logf, page_table, cache_mem_k, cache_mem_v, state_conv, w_in, b_in, b_forget, conv_w, conv_b, conv_ln_g, conv_ln_b, w_conv_out, b_conv_out, w_fox_out, w_mem_kv, w_mem_out, w_out, b_out, ln1_g, ln1_b, w_router, b_router, w_gate, b_gate, w_up, b_up, w_down, b_down, ln2_g, ln2_b):
    proj_w, conv_w_, merge_w, moe_w, wkv, g2, b2 = _prep_weights(
        w_in, b_in, b_forget, conv_w, conv_b, conv_ln_g, conv_ln_b, w_conv_out, b_conv_out, w_fox_out, w_mem_kv,
        w_mem_out, w_out, b_out, ln1_g, ln1_b, w_router, b_router, w_gate, b_gate, w_up, b_up, w_down, b_down,
        ln2_g, ln2_b)
    b, l, d = x_prompt.shape
    db, t, _ = x_sample.shape
    hist_len = CONV_WIDTH - 1

    xp = x_prompt.reshape(b * l, d)
    u, q, k, v, kb, vb, logf, cum, qm = _in_proj(xp, l, proj_w)
    u3 = u.reshape(b, l, CONV_CH)
    yc = _conv_prompt(u3, jnp.zeros((b, HIST_ROWS, CONV_CH), F32), conv_w_)
    yf = _fox_prompt(q.reshape(b, l, FOX_WIDTH), kb.reshape(b, l, FOX_WIDTH), vb.reshape(b, l, FOX_WIDTH),
                     cum.reshape(b, l, FOX_HEADS))
    mk, mv = _mem_kv(mem_prompt.reshape(b * MEM_TOKENS, d), wkv)
    mk3 = mk.reshape(b, MEM_TOKENS, MEM_WIDTH)
    mv3 = mv.reshape(b, MEM_TOKENS, MEM_WIDTH)
    ym = _mem_attend(qm.reshape(b, l, MEM_WIDTH), mk3, mv3, 1, min(512, l))
    y_prompt = _channel(xp, yc.reshape(b * l, CONV_CH), yf.reshape(b * l, FOX_WIDTH), ym.reshape(b * l, MEM_WIDTH),
                        merge_w, moe_w, g2, b2).reshape(b, l, d)

    xs = x_sample.reshape(db * t, d)
    us, qs, ks, vs, ksb, vsb, logfs, cums, qms = _in_proj(xs, t, proj_w)
    us_ext = jnp.concatenate([state_conv.astype(F32), us.reshape(db, t, CONV_CH)], axis=1)
    ycs = _conv_sample(us_ext.transpose(1, 0, 2), conv_w_).transpose(1, 0, 2)
    n_phys = cache_logf.shape[0]
    suf, tot = _page_suffix(cache_logf.transpose(0, 2, 1).reshape(n_phys * FOX_HEADS, PAGE_SIZE))
    yfs = _fox_sample(qs.reshape(db, t, FOX_WIDTH), ksb.reshape(db, t, FOX_WIDTH), vsb.reshape(db, t, FOX_WIDTH),
                      cums.reshape(db, t, FOX_HEADS), cache_k, cache_v,
                      suf.reshape(n_phys, FOX_HEADS, PAGE_SIZE), tot.reshape(n_phys, FOX_HEADS, PAGE_SIZE),
                      page_table)
    t_pad = 2 * SUBLANES
    qms3 = jnp.pad(qms.reshape(db, t, MEM_WIDTH), ((0, 0), (0, t_pad - t), (0, 0)))
    yms = _mem_attend(qms3, cache_mem_k.reshape(db, MEM_TOKENS, MEM_WIDTH),
                      cache_mem_v.reshape(db, MEM_TOKENS, MEM_WIDTH), 8, t_pad)[:, :t]
    y_sample = _channel(xs, ycs.reshape(db * t, CONV_CH), yfs.reshape(db * t, FOX_WIDTH),
                        yms.reshape(db * t, MEM_WIDTH), merge_w, moe_w, g2, b2).reshape(db, t, d)

    heads = lambda a, n, s: a.reshape(n, s, FOX_HEADS, FOX_HEAD_DIM)
    return (y_prompt, y_sample,
            heads(k, b, l), heads(v, b, l), logf.reshape(b, l, FOX_HEADS),
            mk.reshape(b, MEM_TOKENS, MEM_HEADS, MEM_HEAD_DIM), mv.reshape(b, MEM_TOKENS, MEM_HEADS, MEM_HEAD_DIM),
            u3[:, l - hist_len:, :],
            heads(ks, db, t), heads(vs, db, t), logfs.reshape(db, t, FOX_HEADS),
            us_ext[:, t:, :])
```

```python
import functools

import numpy as np
import jax
import jax.numpy as jnp
from jax import lax
from jax.experimental import pallas as pl
from jax.experimental.pallas import tpu as pltpu

D_MODEL = 1024
CONV_CH = 512
CONV_WIDTH = 31
FOX_HEADS = 8
FOX_HEAD_DIM = 64
FOX_WIDTH = FOX_HEADS * FOX_HEAD_DIM
MEM_HEADS = 4
MEM_HEAD_DIM = 128
MEM_WIDTH = MEM_HEADS * MEM_HEAD_DIM
MEM_TOKENS = 256
N_EXPERTS = 32
TOP_K = 4
PAGE_SIZE = 128
SWIGLU_LIMIT = 7.0
SWIGLU_ALPHA = 1.702
LN_EPS = 1e-5
DEEPNORM_ALPHA = 2.0 ** 0.25

LANES = 128
SUBLANES = 8
HIST_ROWS = 32
MIB = 1024 * 1024

BF = jnp.bfloat16
F32 = jnp.float32
NEG_INF = float("-inf")


def _dot(a, b):
    return jnp.dot(a, b, preferred_element_type=F32)


def _dot_nt(a, b):
    return lax.dot_general(a, b, (((1,), (1,)), ((), ())), preferred_element_type=F32)


def _params(vmem_mib, n_axes, **kw):
    return pltpu.CompilerParams(dimension_semantics=("arbitrary",) * n_axes,
                                vmem_limit_bytes=vmem_mib * MIB, **kw)


def _full(shape):
    nd = len(shape)
    return pl.BlockSpec(shape, lambda *_: (0,) * nd)


def _split3(x):
    hi = x.astype(BF)
    r1 = x - hi.astype(F32)
    mid = r1.astype(BF)
    lo = (r1 - mid.astype(F32)).astype(BF)
    return hi, mid, lo


def _layernorm(x, g, b):
    mu = jnp.mean(x, axis=-1, keepdims=True)
    xc = x - mu
    var = jnp.mean(xc * xc, axis=-1, keepdims=True)
    return xc * lax.rsqrt(var + LN_EPS) * g + b


def _in_proj_kernel(x_ref, tri_ref, wglu_ref, wqkv_ref, wf_ref, wqm_ref, bglu_ref, bqkv_ref, bf_ref, bfg_ref,
                    bqm_ref, pq_ref, pk_ref, cq_ref, ck_ref, u_ref, q_ref, k_ref, v_ref, kb_ref, vb_ref, logf_ref,
                    cum_ref, qm_ref, aq_ref, ak_ref, carry_ref, *, tm, seq_len):
    i = pl.program_id(0)
    xb = x_ref[...].astype(BF)
    glu = _dot(xb, wglu_ref[...]) + bglu_ref[...]
    u_ref[...] = glu[:, :CONV_CH] * jax.nn.sigmoid(glu[:, CONV_CH:])
    qkv = _dot(xb, wqkv_ref[...]) + bqkv_ref[...]
    q_ref[...] = (qkv[:, :FOX_WIDTH] * (FOX_HEAD_DIM ** -0.5)).astype(BF)
    k = qkv[:, FOX_WIDTH:2 * FOX_WIDTH]
    v = qkv[:, 2 * FOX_WIDTH:]
    k_ref[...] = k
    v_ref[...] = v
    kb_ref[...] = k.astype(BF)
    vb_ref[...] = v.astype(BF)
    qm_ref[...] = (_dot(xb, wqm_ref[...]) + bqm_ref[...]).astype(BF)
    f = (_dot(xb, wf_ref[...]) + bf_ref[...]) + bfg_ref[...]
    lf = jnp.minimum(f, 0.0) - jnp.log1p(jnp.exp(-jnp.abs(f)))
    logf_ref[...] = lf[:, :FOX_HEADS]
    hi, mid, lo = _split3(lf)
    tri = tri_ref[...]
    cum = _dot(tri, hi) + _dot(tri, mid) + _dot(tri, lo)
    if seq_len > tm:
        @pl.when(i % (seq_len // tm) == 0)
        def _():
            carry_ref[...] = jnp.zeros_like(carry_ref)
        cum = cum + carry_ref[...]
        carry_ref[...] = cum[tm - 1:tm, :]
    cum_ref[...] = cum[:, :FOX_HEADS]
    parts = _split3(cum)
    aq = cq_ref[...] + _dot(parts[0], pq_ref[0]) + _dot(parts[1], pq_ref[1]) + _dot(parts[2], pq_ref[2])
    ak = ck_ref[...] + _dot(parts[0], pk_ref[0]) + _dot(parts[1], pk_ref[1]) + _dot(parts[2], pk_ref[2])
    aq_ref[...] = aq.astype(BF)
    ak_ref[...] = ak.astype(BF)


def _aug_lane(h):
    return h * LANES + (FOX_HEAD_DIM if h % 2 == 0 else 0)


def _aug_constants():
    pq = np.zeros((3, LANES, FOX_HEADS * LANES), np.float32)
    pk = np.zeros((3, LANES, FOX_HEADS * LANES), np.float32)
    cq = np.zeros((1, FOX_HEADS * LANES), np.float32)
    ck = np.zeros((1, FOX_HEADS * LANES), np.float32)
    for h in range(FOX_HEADS):
        base = _aug_lane(h)
        for j in range(3):
            pq[j, h, base + j] = 1.0
            pk[j, h, base + 3 + j] = -1.0
            cq[0, base + 3 + j] = 1.0
            ck[0, base + j] = 1.0
    return jnp.asarray(pq, BF), jnp.asarray(pk, BF), jnp.asarray(cq), jnp.asarray(ck)


def _in_proj(x2d, seq_len, w):
    n = x2d.shape[0]
    tm = min(512, n)
    lc = min(seq_len, tm)
    r = np.arange(tm)
    tri = jnp.asarray(((r[None, :] <= r[:, None]) & (r[None, :] // lc == r[:, None] // lc)).astype(np.float32), BF)
    tok = lambda width: pl.BlockSpec((tm, width), lambda i: (i, 0))
    out_shape = (jax.ShapeDtypeStruct((n, CONV_CH), F32), jax.ShapeDtypeStruct((n, FOX_WIDTH), BF),
                 jax.ShapeDtypeStruct((n, FOX_WIDTH), F32), jax.ShapeDtypeStruct((n, FOX_WIDTH), F32),
                 jax.ShapeDtypeStruct((n, FOX_WIDTH), BF), jax.ShapeDtypeStruct((n, FOX_WIDTH), BF),
                 jax.ShapeDtypeStruct((n, FOX_HEADS), F32), jax.ShapeDtypeStruct((n, FOX_HEADS), F32),
                 jax.ShapeDtypeStruct((n, MEM_WIDTH), BF),
                 jax.ShapeDtypeStruct((n, FOX_HEADS * LANES), BF), jax.ShapeDtypeStruct((n, FOX_HEADS * LANES), BF))
    ins = (x2d, tri, w["wglu"], w["wqkv"], w["wf"], w["wqm"], w["bglu"], w["bqkv"], w["bf"], w["bfg"], w["bqm"],
           *_aug_constants())
    return pl.pallas_call(
        functools.partial(_in_proj_kernel, tm=tm, seq_len=seq_len),
        grid=(n // tm,),
        in_specs=[tok(D_MODEL)] + [_full(a.shape) for a in ins[1:]],
        out_specs=(tok(CONV_CH), tok(FOX_WIDTH), tok(FOX_WIDTH), tok(FOX_WIDTH), tok(FOX_WIDTH), tok(FOX_WIDTH),
                   tok(FOX_HEADS), tok(FOX_HEADS), tok(MEM_WIDTH), tok(FOX_HEADS * LANES), tok(FOX_HEADS * LANES)),
        out_shape=out_shape,
        scratch_shapes=[pltpu.VMEM((1, LANES), F32)],
        compiler_params=_params(56, 1),
    )(*ins)


def _conv_post(y, cb_ref, g_ref, b_ref):
    y = _layernorm(y + cb_ref[...], g_ref[...], b_ref[...])
    return (y * jax.nn.sigmoid(y)).astype(BF)


def _conv_prompt_kernel(u_ref, prev_ref, hist_ref, w_ref, cb_ref, g_ref, b_ref, o_ref, win_ref, y_ref, *, tm):
    i = pl.program_id(1)
    win_ref[0:HIST_ROWS, :] = jnp.where(i == 0, hist_ref[0], prev_ref[0])
    win_ref[HIST_ROWS:, :] = u_ref[0]
    first = HIST_ROWS - (CONV_WIDTH - 1)
    for c in range(CONV_CH // LANES):
        cs = slice(c * LANES, (c + 1) * LANES)
        acc = jnp.zeros((tm, LANES), F32)
        for j in range(CONV_WIDTH):
            acc = acc + w_ref[j:j + 1, cs] * win_ref[first + j:first + j + tm, cs]
        y_ref[:, cs] = acc
    o_ref[0] = _conv_post(y_ref[...], cb_ref, g_ref, b_ref)


def _conv_prompt(u3, hist, cw):
    b, l, _ = u3.shape
    tm = 256
    per = tm // HIST_ROWS
    vec = _full((1, CONV_CH))
    return pl.pallas_call(
        functools.partial(_conv_prompt_kernel, tm=tm),
        grid=(b, l // tm),
        in_specs=[pl.BlockSpec((1, tm, CONV_CH), lambda bi, i: (bi, i, 0)),
                  pl.BlockSpec((1, HIST_ROWS, CONV_CH), lambda bi, i: (bi, jnp.maximum(i * per - 1, 0), 0)),
                  pl.BlockSpec((1, HIST_ROWS, CONV_CH), lambda bi, i: (bi, 0, 0)),
                  _full((HIST_ROWS, CONV_CH)), vec, vec, vec],
        out_specs=pl.BlockSpec((1, tm, CONV_CH), lambda bi, i: (bi, i, 0)),
        out_shape=jax.ShapeDtypeStruct((b, l, CONV_CH), BF),
        scratch_shapes=[pltpu.VMEM((tm + HIST_ROWS, CONV_CH), F32), pltpu.VMEM((tm, CONV_CH), F32)],
        compiler_params=_params(32, 2),
    )(u3, u3, hist, cw["w"], cw["cb"], cw["g"], cw["b"])


def _conv_sample_kernel(x_ref, w_ref, cb_ref, g_ref, b_ref, o_ref, *, steps):
    for t in range(steps):
        acc = jnp.zeros(x_ref.shape[1:], F32)
        for j in range(CONV_WIDTH):
            acc = acc + w_ref[j:j + 1, :] * x_ref[t + j]
        o_ref[t] = _conv_post(acc, cb_ref, g_ref, b_ref)


def _conv_sample(u_ext_t, cw):
    rows, b, _ = u_ext_t.shape
    steps = rows - (CONV_WIDTH - 1)
    bb = min(64, b)
    vec = _full((1, CONV_CH))
    return pl.pallas_call(
        functools.partial(_conv_sample_kernel, steps=steps),
        grid=(b // bb,),
        in_specs=[pl.BlockSpec((rows, bb, CONV_CH), lambda i: (0, i, 0)), _full((HIST_ROWS, CONV_CH)), vec, vec, vec],
        out_specs=pl.BlockSpec((steps, bb, CONV_CH), lambda i: (0, i, 0)),
        out_shape=jax.ShapeDtypeStruct((steps, b, CONV_CH), BF),
        compiler_params=_params(32, 1),
    )(u_ext_t, cw["w"], cw["cb"], cw["g"], cw["b"])


def _fox_prompt_kernel(qi_ref, kj_ref, q_ref, k_ref, v_ref, aq_ref, ak_ref, o_ref, qa_ref, ka_ref, va_ref, m_ref,
                       acc_ref, *, tile, sub):
    t = pl.program_id(2)
    qi = qi_ref[t]
    kj = kj_ref[t]
    nsub = tile // sub
    lane = lax.broadcasted_iota(jnp.int32, (1, LANES), 1)
    in_head = [(lane >= hh * FOX_HEAD_DIM) & (lane < (hh + 1) * FOX_HEAD_DIM) for hh in range(2)]
    sum_lane = [_aug_lane(hh) % LANES for hh in range(2)]

    for hh in range(2):
        ka_ref[hh] = jnp.where(in_head[hh], k_ref[0], ak_ref[0, :, hh * LANES:(hh + 1) * LANES])
        va_ref[hh] = jnp.where(in_head[hh], v_ref[0], jnp.where(lane == sum_lane[hh], 1.0, 0.0).astype(BF))

    @pl.when(kj == 0)
    def _():
        m_ref[...] = jnp.full_like(m_ref, NEG_INF)
        acc_ref[...] = jnp.zeros_like(acc_ref)
        for hh in range(2):
            qa_ref[hh] = jnp.where(in_head[hh], q_ref[0], aq_ref[0, :, hh * LANES:(hh + 1) * LANES])

    def attend(hh, i2, j2, masked):
        rows = slice(i2 * sub, (i2 + 1) * sub)
        cols = slice(j2 * sub, (j2 + 1) * sub)
        s = _dot_nt(qa_ref[hh, rows, :], ka_ref[hh, cols, :])
        if masked:
            r_id = lax.broadcasted_iota(jnp.int32, (sub, sub), 0)
            c_id = lax.broadcasted_iota(jnp.int32, (sub, sub), 1)
            s = jnp.where(c_id <= r_id, s, NEG_INF)
        m_prev = m_ref[hh, rows, :]
        m_next = jnp.maximum(m_prev, jnp.max(s, axis=1, keepdims=True))
        alpha = jnp.exp(m_prev - m_next)
        p = jnp.exp(s - jnp.concatenate([m_next] * (sub // LANES), axis=1))
        acc_ref[hh, rows, :] = alpha * acc_ref[hh, rows, :] + _dot(p.astype(BF), va_ref[hh, cols, :])
        m_ref[hh, rows, :] = m_next

    @pl.when(kj < qi)
    def _():
        for hh in range(2):
            for i2 in range(nsub):
                for j2 in range(nsub):
                    attend(hh, i2, j2, False)

    @pl.when(kj == qi)
    def _():
        for hh in range(2):
            for i2 in range(nsub):
                for j2 in range(i2 + 1):
                    attend(hh, i2, j2, j2 == i2)
        outs = []
        for hh in range(2):
            acc = acc_ref[hh]
            outs.append(acc / acc[:, sum_lane[hh]:sum_lane[hh] + 1])
        o_ref[0] = jnp.where(in_head[0], outs[0], outs[1]).astype(BF)


def _fox_prompt(q3, k3, v3, aq3, ak3):
    b, l, _ = q3.shape
    tile = min(1024, l)
    sub = min(512, tile)
    nq = l // tile
    pairs = FOX_HEADS // 2
    qi = np.concatenate([np.full(i + 1, i) for i in range(nq)]).astype(np.int32)
    kj = np.concatenate([np.arange(i + 1) for i in range(nq)]).astype(np.int32)
    grid_spec = pltpu.PrefetchScalarGridSpec(
        num_scalar_prefetch=2,
        grid=(b, pairs, len(qi)),
        in_specs=[pl.BlockSpec((1, tile, LANES), lambda bi, p, t, qi_r, kj_r: (bi, qi_r[t], p)),
                  pl.BlockSpec((1, tile, LANES), lambda bi, p, t, qi_r, kj_r: (bi, kj_r[t], p)),
                  pl.BlockSpec((1, tile, LANES), lambda bi, p, t, qi_r, kj_r: (bi, kj_r[t], p)),
                  pl.BlockSpec((1, tile, 2 * LANES), lambda bi, p, t, qi_r, kj_r: (bi, qi_r[t], p)),
                  pl.BlockSpec((1, tile, 2 * LANES), lambda bi, p, t, qi_r, kj_r: (bi, kj_r[t], p))],
        out_specs=pl.BlockSpec((1, tile, LANES), lambda bi, p, t, qi_r, kj_r: (bi, qi_r[t], p)),
        scratch_shapes=[pltpu.VMEM((2, tile, LANES), BF)] * 3 + [pltpu.VMEM((2, tile, LANES), F32)] * 2)
    return pl.pallas_call(
        functools.partial(_fox_prompt_kernel, tile=tile, sub=sub),
        grid_spec=grid_spec,
        out_shape=jax.ShapeDtypeStruct((b, l, FOX_WIDTH), BF),
        compiler_params=_params(40, 3),
    )(jnp.asarray(qi), jnp.asarray(kj), q3, k3, v3, aq3, ak3)


def _page_suffix_kernel(x_ref, upper_ref, ones_ref, suf_ref, tot_ref):
    hi, mid, lo = _split3(x_ref[...])
    up = upper_ref[...]
    on = ones_ref[...]
    suf_ref[...] = _dot(hi, up) + _dot(mid, up) + _dot(lo, up)
    tot_ref[...] = _dot(hi, on) + _dot(mid, on) + _dot(lo, on)


PAGE_COLS = PAGE_SIZE * FOX_HEADS


def _page_suffix(logf_flat):
    rows = logf_flat.shape[0]
    tr = min(512, rows)
    c = np.arange(PAGE_COLS)
    same_head = (c[:, None] % FOX_HEADS) == (c[None, :] % FOX_HEADS)
    upper = jnp.asarray((same_head & (c[:, None] // FOX_HEADS > c[None, :] // FOX_HEADS)).astype(np.float32), BF)
    ones = jnp.asarray(same_head.astype(np.float32), BF)
    spec = pl.BlockSpec((tr, PAGE_COLS), lambda i: (i, 0))
    return pl.pallas_call(
        _page_suffix_kernel,
        grid=(rows // tr,),
        in_specs=[spec, _full((PAGE_COLS, PAGE_COLS)), _full((PAGE_COLS, PAGE_COLS))],
        out_specs=(spec, spec),
        out_shape=(jax.ShapeDtypeStruct((rows, PAGE_COLS), F32),) * 2,
        compiler_params=_params(40, 1),
    )(logf_flat, upper, ones)


def _fox_sample_kernel(pt_ref, q_ref, cq_ref, crow_ref, kn_ref, vn_ref, *rest, pages):
    k_refs = rest[:pages]
    v_refs = rest[pages:2 * pages]
    suf_refs = rest[2 * pages:3 * pages]
    tot_refs = rest[3 * pages:4 * pages]
    o_ref, m_ref, l_ref, acc_ref, carry_ref = rest[4 * pages:]
    j = pl.program_id(1)
    rows = q_ref.shape[1]
    q = q_ref[0]
    cq = cq_ref[0]

    def ids(ncols):
        return (lax.broadcasted_iota(jnp.int32, (rows, ncols), 0), lax.broadcasted_iota(jnp.int32, (rows, ncols), 1))

    def update(s, vals):
        m_prev = m_ref[...]
        m_next = jnp.maximum(m_prev, jnp.max(s, axis=1, keepdims=True))
        alpha = jnp.exp(m_prev - m_next)
        p = jnp.exp(s - jnp.concatenate([m_next] * (s.shape[1] // LANES), axis=1))
        l_ref[...] = alpha * l_ref[...] + jnp.sum(p, axis=1, keepdims=True)
        pv = None
        off = 0
        for val in vals:
            term = _dot(p[:, off:off + val.shape[0]].astype(BF), val)
            pv = term if pv is None else pv + term
            off += val.shape[0]
        acc_ref[...] = alpha[:, :FOX_HEAD_DIM] * acc_ref[...] + pv
        m_ref[...] = m_next

    @pl.when(j == 0)
    def _():
        m_ref[...] = jnp.full_like(m_ref, NEG_INF)
        l_ref[...] = jnp.zeros_like(l_ref)
        acc_ref[...] = jnp.zeros_like(acc_ref)
        carry_ref[...] = jnp.zeros_like(carry_ref)
        s = _dot_nt(q, kn_ref[0]) + (cq - crow_ref[0])
        r_id, c_id = ids(LANES)
        keep = (c_id % FOX_HEADS == r_id % FOX_HEADS) & (c_id // FOX_HEADS <= r_id // FOX_HEADS)
        update(jnp.where(keep, s, NEG_INF), [vn_ref[0]])

    r_id, c_id = ids(PAGE_COLS)
    same_head = c_id % FOX_HEADS == r_id % FOX_HEADS
    cq_cols = jnp.concatenate([cq] * (PAGE_COLS // LANES), axis=1)
    carry = carry_ref[...]
    scores = [None] * pages
    for r in reversed(range(pages)):
        kk = k_refs[r][0].reshape(PAGE_COLS, FOX_HEAD_DIM).astype(BF)
        s = _dot_nt(q, kk) + cq_cols + (carry + suf_refs[r][0])
        scores[r] = jnp.where(same_head, s, NEG_INF)
        carry = carry + tot_refs[r][0]
    carry_ref[...] = carry
    update(jnp.concatenate(scores, axis=1),
           [vr[0].reshape(PAGE_COLS, FOX_HEAD_DIM).astype(BF) for vr in v_refs])

    @pl.when(j == pl.num_programs(1) - 1)
    def _():
        o_ref[0] = (acc_ref[...] / l_ref[:, :FOX_HEAD_DIM]).astype(BF)


def _fox_sample(q2, kn2, vn2, cum2, cache_k, cache_v, suf, tot, page_table):
    db, n_pages = page_table.shape
    rows = q2.shape[0] // db * FOX_HEADS
    pages = 8
    q = q2.reshape(db, rows, FOX_HEAD_DIM)
    cq = jnp.broadcast_to(cum2.reshape(db, rows, 1), (db, rows, LANES))
    crow = jnp.pad(cum2.reshape(db, 1, rows), ((0, 0), (0, 0), (0, LANES - rows)))
    kn = jnp.pad(kn2.reshape(db, rows, FOX_HEAD_DIM), ((0, 0), (0, LANES - rows), (0, 0)))
    vn = jnp.pad(vn2.reshape(db, rows, FOX_HEAD_DIM), ((0, 0), (0, LANES - rows), (0, 0)))
    n_steps = n_pages // pages

    def page_map(r, nd):
        return lambda bi, j, pt: (pt[bi, (n_steps - 1 - j) * pages + r],) + (0,) * (nd - 1)

    seq = lambda shape: pl.BlockSpec((1,) + shape, lambda bi, j, pt: (bi, 0, 0))
    in_specs = [seq((rows, FOX_HEAD_DIM)), seq((rows, LANES)), seq((1, LANES)),
                seq((LANES, FOX_HEAD_DIM)), seq((LANES, FOX_HEAD_DIM))]
    in_specs += [pl.BlockSpec((1, PAGE_SIZE, FOX_HEADS, FOX_HEAD_DIM), page_map(r, 4)) for r in range(pages)] * 2
    in_specs += [pl.BlockSpec((1, 1, PAGE_COLS), page_map(r, 3)) for r in range(pages)] * 2
    grid_spec = pltpu.PrefetchScalarGridSpec(
        num_scalar_prefetch=1,
        grid=(db, n_steps),
        in_specs=in_specs,
        out_specs=pl.BlockSpec((1, rows, FOX_HEAD_DIM), lambda bi, j, pt: (bi, 0, 0)),
        scratch_shapes=[pltpu.VMEM((rows, LANES), F32), pltpu.VMEM((rows, LANES), F32),
                        pltpu.VMEM((rows, FOX_HEAD_DIM), F32), pltpu.VMEM((1, PAGE_COLS), F32)])
    out = pl.pallas_call(
        functools.partial(_fox_sample_kernel, pages=pages),
        grid_spec=grid_spec,
        out_shape=jax.ShapeDtypeStruct((db, rows, FOX_HEAD_DIM), BF),
        compiler_params=_params(48, 2),
    )(page_table, q, cq, crow, kn, vn, *([cache_k] * pages), *([cache_v] * pages), *([suf] * pages), *([tot] * pages))
    return out.reshape(q2.shape)


def _mem_kv_kernel(m_ref, w_ref, k_ref, v_ref):
    kv = _dot(m_ref[...].astype(BF), w_ref[...])
    k_ref[...] = kv[:, :MEM_WIDTH]
    v_ref[...] = kv[:, MEM_WIDTH:]


def _mem_kv(mem2d, w_bf):
    n = mem2d.shape[0]
    tm = min(256, n)
    out = pl.BlockSpec((tm, MEM_WIDTH), lambda i: (i, 0))
    return pl.pallas_call(
        _mem_kv_kernel,
        grid=(n // tm,),
        in_specs=[pl.BlockSpec((tm, D_MODEL), lambda i: (i, 0)), _full(w_bf.shape)],
        out_specs=(out, out),
        out_shape=(jax.ShapeDtypeStruct((n, MEM_WIDTH), F32),) * 2,
        compiler_params=_params(32, 1),
    )(mem2d, w_bf)


def _mem_attend_kernel(q_ref, k_ref, v_ref, o_ref, *, bb):
    for b in range(bb):
        for h in range(MEM_HEADS):
            hs = slice(h * MEM_HEAD_DIM, (h + 1) * MEM_HEAD_DIM)
            s = _dot_nt(q_ref[b, :, hs], k_ref[b, :, hs].astype(BF)) * (MEM_HEAD_DIM ** -0.5)
            p = jnp.exp(s - jnp.max(s, axis=1, keepdims=True))
            den = jnp.sum(p, axis=1, keepdims=True)
            o_ref[b, :, hs] = (_dot(p.astype(BF), v_ref[b, :, hs].astype(BF)) / den).astype(BF)


def _mem_attend(qm3, mk3, mv3, bb, tq):
    b, l, _ = qm3.shape
    kv = pl.BlockSpec((bb, MEM_TOKENS, MEM_WIDTH), lambda bi, i: (bi, 0, 0))
    qs = pl.BlockSpec((bb, tq, MEM_WIDTH), lambda bi, i: (bi, i, 0))
    return pl.pallas_call(
        functools.partial(_mem_attend_kernel, bb=bb),
        grid=(b // bb, l // tq),
        in_specs=[qs, kv, kv],
        out_specs=qs,
        out_shape=jax.ShapeDtypeStruct((b, l, MEM_WIDTH), BF),
        compiler_params=_params(40, 2),
    )(qm3, mk3, mv3)


def _merge_kernel(x_ref, yc_ref, yf_ref, ym_ref, wg_ref, bg_ref, wc_ref, bc_ref, wfo_ref, wmo_ref, wo_ref, bo_ref,
                  g1_ref, b1_ref, wrh_ref, wrl_ref, br_ref, h_ref, e_ref, gate_ref, *, tm):
    x = x_ref[...]
    xb = x.astype(BF)

    def gate(c):
        cs = slice(c * D_MODEL, (c + 1) * D_MODEL)
        return jax.nn.sigmoid(_dot(xb, wg_ref[:, cs]) + bg_ref[:, cs])

    mix = gate(0) * (_dot(yc_ref[...], wc_ref[...]) + bc_ref[...])
    mix = mix + gate(1) * _dot(yf_ref[...], wfo_ref[...])
    mix = mix + gate(2) * _dot(ym_ref[...], wmo_ref[...])
    pre = DEEPNORM_ALPHA * x + (_dot(mix.astype(BF), wo_ref[...]) + bo_ref[...])
    h = _layernorm(pre, g1_ref[...], b1_ref[...])
    for j in range(D_MODEL // LANES):
        h_ref[pl.ds(j, tm, stride=SUBLANES), :] = h[:, j * LANES:(j + 1) * LANES]

    h_hi = h.astype(BF)
    h_lo = (h - h_hi.astype(F32)).astype(BF)
    logits = _dot(h_hi, wrh_ref[...]) + _dot(h_hi, wrl_ref[...]) + _dot(h_lo, wrh_ref[...]) + br_ref[...]
    lane = lax.broadcasted_iota(jnp.int32, (tm, LANES), 1)
    lane_f = lane.astype(F32)
    work = jnp.where(lane < N_EXPERTS, logits, NEG_INF)
    vals, idxs = [], []
    for _ in range(TOP_K):
        mx = jnp.max(work, axis=1, keepdims=True)
        idx = jnp.min(jnp.where(work == mx, lane_f, float(LANES)), axis=1, keepdims=True)
        vals.append(mx)
        idxs.append(idx)
        work = jnp.where(lane_f == idx, NEG_INF, work)
    exps = [jnp.exp(v - vals[0]) for v in vals]
    den = exps[0] + exps[1] + exps[2] + exps[3]
    e_out = jnp.zeros((tm, LANES), F32)
    g_out = jnp.zeros((tm, LANES), F32)
    for kk in range(TOP_K):
        e_out = jnp.where(lane == kk, idxs[kk], e_out)
        g_out = jnp.where(lane == kk, exps[kk] / den, g_out)
    e_ref[...] = e_out[:, :TOP_K].astype(jnp.int32)
    gate_ref[...] = g_out[:, :TOP_K]


def _merge(x2d, yc, yf, ym, w):
    n = x2d.shape[0]
    tm = min(256, n)
    tok = lambda width: pl.BlockSpec((tm, width), lambda i: (i, 0))
    ws = (w["wg"], w["bg"], w["wc"], w["bc"], w["wfo"], w["wmo"], w["wo"], w["bo"], w["g1"], w["b1"],
          w["wrh"], w["wrl"], w["br"])
    return pl.pallas_call(
        functools.partial(_merge_kernel, tm=tm),
        grid=(n // tm,),
        in_specs=[tok(D_MODEL), tok(CONV_CH), tok(FOX_WIDTH), tok(MEM_WIDTH)] + [_full(a.shape) for a in ws],
        out_specs=(pl.BlockSpec((tm * SUBLANES, LANES), lambda i: (i, 0)), tok(TOP_K), tok(TOP_K)),
        out_shape=(jax.ShapeDtypeStruct((n * SUBLANES, LANES), F32), jax.ShapeDtypeStruct((n, TOP_K), jnp.int32),
                   jax.ShapeDtypeStruct((n, TOP_K), F32)),
        compiler_params=_params(56, 1),
    )(x2d, yc, yf, ym, *ws)


MOE_ROWS = 256
IDX_SLOTS = 3
DMA_UNROLL_BITS = 3
DMA_UNROLL = 1 << DMA_UNROLL_BITS
LANE_BITS = LANES.bit_length() - 1
TOP_K_BITS = TOP_K.bit_length() - 1


def _moe_kernel(blk_e_ref, nvalid_ref, nreal_ref, rows_hbm, h_hbm, wg_ref, bg_ref, wu_ref, bu_ref, wd_ref, bd_ref,
                y_hbm, idx_ref, xbuf, ybuf, wgb, wub, wdb, idx_sem, in_sem, out_sem):
    i = pl.program_id(0)
    nreal = nreal_ref[0]
    tile = SUBLANES

    def idx_slot(blk):
        return lax.rem(blk, IDX_SLOTS)

    def idx_copy(blk):
        slot = idx_slot(blk)
        return pltpu.make_async_copy(rows_hbm.at[blk], idx_ref.at[slot], idx_sem.at[slot])

    def row_index(islot, r):
        return idx_ref[islot, r >> LANE_BITS, r & (LANES - 1)]

    def gather_row(islot, slot, r):
        tok = row_index(islot, r) >> TOP_K_BITS
        return pltpu.make_async_copy(h_hbm.at[pl.ds(pl.multiple_of(tok * tile, tile), tile)],
                                     xbuf.at[slot, pl.ds(pl.multiple_of(r * tile, tile), tile)],
                                     in_sem.at[slot])

    def scatter_row(islot, slot, r):
        dst = row_index(islot, r)
        return pltpu.make_async_copy(ybuf.at[slot, pl.ds(pl.multiple_of(r * tile, tile), tile)],
                                     y_hbm.at[pl.ds(pl.multiple_of(dst * tile, tile), tile)],
                                     out_sem.at[slot])

    def start_gather(blk):
        islot = idx_slot(blk)
        slot = blk & 1

        def body(c, carry):
            for u in range(DMA_UNROLL):
                gather_row(islot, slot, c * DMA_UNROLL + u).start()
            return carry
        lax.fori_loop(0, MOE_ROWS // DMA_UNROLL, body, 0)

    def wait_rows(sem, slot):
        pltpu.make_async_copy(xbuf.at[slot], ybuf.at[slot], sem.at[slot]).wait()

    def wait_scatter(blk):
        slot = blk & 1
        nv = nvalid_ref[blk]

        @pl.when(nv == MOE_ROWS)
        def _():
            wait_rows(out_sem, slot)

        @pl.when(nv < MOE_ROWS)
        def _():
            def body(r, c):
                pltpu.make_async_copy(ybuf.at[slot, pl.ds(0, tile)], y_hbm.at[pl.ds(0, tile)],
                                      out_sem.at[slot]).wait()
                return c
            lax.fori_loop(0, nv, body, 0)

    @pl.when((i == 0) & (nreal > 0))
    def _():
        idx_copy(0).start()
        idx_copy(0).wait()
        start_gather(0)

        @pl.when(nreal > 1)
        def _():
            idx_copy(1).start()

    @pl.when(i + 1 < nreal)
    def _():
        idx_copy(i + 1).wait()
        start_gather(i + 1)

        @pl.when(i + 2 < nreal)
        def _():
            idx_copy(i + 2).start()

    changed = (i == 0) | (blk_e_ref[i] != blk_e_ref[jnp.maximum(i - 1, 0)])

    @pl.when(changed & (i < nreal))
    def _():
        wgb[...] = wg_ref[0].astype(BF)
        wub[...] = wu_ref[0].astype(BF)
        wdb[...] = wd_ref[0].astype(BF)

    @pl.when(i < nreal)
    def _():
        slot = i & 1
        islot = idx_slot(i)
        wait_rows(in_sem, slot)

        @pl.when(i >= 2)
        def _():
            wait_scatter(i - 2)

        x = jnp.concatenate([xbuf[slot, pl.ds(j, MOE_ROWS, stride=SUBLANES), :] for j in range(D_MODEL // LANES)],
                            axis=1).astype(BF)
        a = jnp.minimum(_dot(x, wgb[...]) + bg_ref[0], SWIGLU_LIMIT)
        u = jnp.clip(_dot(x, wub[...]) + bu_ref[0], -SWIGLU_LIMIT, SWIGLU_LIMIT)
        hid = (u + 1.0) * a * jax.nn.sigmoid(SWIGLU_ALPHA * a)
        y = _dot(hid.astype(BF), wdb[...]) + bd_ref[0]
        for j in range(D_MODEL // LANES):
            ybuf[slot, pl.ds(j, MOE_ROWS, stride=SUBLANES), :] = y[:, j * LANES:(j + 1) * LANES]

        nv = nvalid_ref[i]

        def body(c, carry):
            for u in range(DMA_UNROLL):
                r = c * DMA_UNROLL + u

                @pl.when(r < nv)
                def _():
                    scatter_row(islot, slot, r).start()
            return carry
        lax.fori_loop(0, (nv + DMA_UNROLL - 1) >> DMA_UNROLL_BITS, body, 0)

        @pl.when(i == nreal - 1)
        def _():
            @pl.when(i >= 1)
            def _():
                wait_scatter(i - 1)
            wait_scatter(i)


def _moe(h_rows, top_e, w):
    n = top_e.shape[0]
    flat_e = top_e.reshape(-1)
    n_flat = n * TOP_K
    n_blocks = -(-n_flat // MOE_ROWS) + N_EXPERTS
    order = jnp.argsort(flat_e).astype(jnp.int32)
    counts = jnp.sum((flat_e[:, None] == jnp.arange(N_EXPERTS, dtype=jnp.int32)[None, :]).astype(jnp.int32), axis=0)
    starts = jnp.cumsum(counts) - counts
    padded = (counts + MOE_ROWS - 1) // MOE_ROWS * MOE_ROWS
    pad_ends = jnp.cumsum(padded)
    pad_starts = pad_ends - padded
    blk_start = jnp.arange(n_blocks, dtype=jnp.int32) * MOE_ROWS
    blk_e = jnp.minimum(jnp.sum((pad_ends[None, :] <= blk_start[:, None]).astype(jnp.int32), axis=1),
                        N_EXPERTS - 1).astype(jnp.int32)
    nreal = (pad_ends[-1] // MOE_ROWS).astype(jnp.int32).reshape(1)
    r = jnp.arange(n_blocks * MOE_ROWS, dtype=jnp.int32)
    e_r = jnp.repeat(blk_e, MOE_ROWS)
    rank = r - pad_starts[e_r]
    valid = (rank < counts[e_r]) & (r < pad_ends[-1])
    src = jnp.clip(starts[e_r] + rank, 0, n_flat - 1)
    rows = jnp.where(valid, order[src], 0).astype(jnp.int32)
    rows = jnp.pad(rows.reshape(n_blocks, MOE_ROWS // LANES, LANES),
                   ((0, 0), (0, SUBLANES - MOE_ROWS // LANES), (0, 0)))
    nvalid = jnp.sum(valid.reshape(n_blocks, MOE_ROWS).astype(jnp.int32), axis=1)
    out_rows = n_flat * SUBLANES
    wspec = pl.BlockSpec((1, D_MODEL, D_MODEL), lambda i, be, nv, nr: (be[i], 0, 0))
    bspec = pl.BlockSpec((1, 1, D_MODEL), lambda i, be, nv, nr: (be[i], 0, 0))
    any_spec = pl.BlockSpec(memory_space=pl.ANY)
    grid_spec = pltpu.PrefetchScalarGridSpec(
        num_scalar_prefetch=3,
        grid=(n_blocks,),
        in_specs=[any_spec, any_spec, wspec, bspec, wspec, bspec, wspec, bspec],
        out_specs=any_spec,
        scratch_shapes=[pltpu.SMEM((IDX_SLOTS, SUBLANES, LANES), jnp.int32),
                        pltpu.VMEM((2, MOE_ROWS * SUBLANES, LANES), F32),
                        pltpu.VMEM((2, MOE_ROWS * SUBLANES, LANES), F32),
                        pltpu.VMEM((D_MODEL, D_MODEL), BF), pltpu.VMEM((D_MODEL, D_MODEL), BF),
                        pltpu.VMEM((D_MODEL, D_MODEL), BF),
                        pltpu.SemaphoreType.DMA((IDX_SLOTS,)), pltpu.SemaphoreType.DMA((2,)),
                        pltpu.SemaphoreType.DMA((2,))])
    return pl.pallas_call(
        _moe_kernel,
        grid_spec=grid_spec,
        out_shape=jax.ShapeDtypeStruct((out_rows, LANES), F32),
        compiler_params=_params(56, 1, disable_bounds_checks=True),
    )(blk_e, nvalid, nreal, rows, h_rows, w["wgate"], w["bgate"], w["wup"], w["bup"], w["wdown"], w["bdown"])


def _combine_kernel(h_ref, y_ref, gate_ref, g2_ref, b2_ref, o_ref, *, tm):
    g = gate_ref[...]
    cols = []
    for j in range(D_MODEL // LANES):
        f = None
        for kk in range(TOP_K):
            term = g[:, kk:kk + 1] * y_ref[pl.ds(kk * SUBLANES + j, tm, stride=TOP_K * SUBLANES), :]
            f = term if f is None else f + term
        cols.append(DEEPNORM_ALPHA * h_ref[pl.ds(j, tm, stride=SUBLANES), :] + f)
    o_ref[...] = _layernorm(jnp.concatenate(cols, axis=1), g2_ref[...], b2_ref[...])


def _combine(h_rows, y_rows, gate, g2, b2):
    n = gate.shape[0]
    tm = min(256, n)
    return pl.pallas_call(
        functools.partial(_combine_kernel, tm=tm),
        grid=(n // tm,),
        in_specs=[pl.BlockSpec((tm * SUBLANES, LANES), lambda i: (i, 0)),
                  pl.BlockSpec((tm * TOP_K * SUBLANES, LANES), lambda i: (i, 0)),
                  pl.BlockSpec((tm, TOP_K), lambda i: (i, 0)), _full((1, D_MODEL)), _full((1, D_MODEL))],
        out_specs=pl.BlockSpec((tm, D_MODEL), lambda i: (i, 0)),
        out_shape=jax.ShapeDtypeStruct((n, D_MODEL), F32),
        compiler_params=_params(40, 1),
    )(h_rows, y_rows, gate, g2, b2)


def _row(v):
    return v.reshape(1, -1).astype(F32)


def _prep_weights(w_in, b_in, b_forget, conv_w, conv_b, conv_ln_g, conv_ln_b, w_conv_out, b_conv_out, w_fox_out,
                  w_mem_kv, w_mem_out, w_out, b_out, ln1_g, ln1_b, w_router, b_router, w_gate, b_gate, w_up, b_up,
                  w_down, b_down, ln2_g, ln2_b):
    o_q = 2 * CONV_CH
    o_f = o_q + 3 * FOX_WIDTH
    o_qm = o_f + FOX_HEADS
    o_g = o_qm + MEM_WIDTH
    pad_f = LANES - FOX_HEADS
    proj = dict(
        wglu=w_in[:, :o_q].astype(BF), bglu=_row(b_in[:o_q]),
        wqkv=w_in[:, o_q:o_f].astype(BF), bqkv=_row(b_in[o_q:o_f]),
        wf=jnp.pad(w_in[:, o_f:o_qm], ((0, 0), (0, pad_f))).astype(BF),
        bf=_row(jnp.pad(b_in[o_f:o_qm], (0, pad_f))), bfg=_row(jnp.pad(b_forget, (0, pad_f))),
        wqm=w_in[:, o_qm:o_g].astype(BF), bqm=_row(b_in[o_qm:o_g]))
    conv = dict(w=jnp.pad(conv_w, ((0, HIST_ROWS - CONV_WIDTH), (0, 0))).astype(F32), cb=_row(conv_b),
                g=_row(conv_ln_g), b=_row(conv_ln_b))
    pad_r = LANES - N_EXPERTS
    wr = jnp.pad(w_router, ((0, 0), (0, pad_r)))
    wr_hi = wr.astype(BF)
    merge = dict(
        wg=w_in[:, o_g:].astype(BF), bg=_row(b_in[o_g:]), wc=w_conv_out.astype(BF), bc=_row(b_conv_out),
        wfo=w_fox_out.astype(BF), wmo=w_mem_out.astype(BF), wo=w_out.astype(BF), bo=_row(b_out),
        g1=_row(ln1_g), b1=_row(ln1_b), wrh=wr_hi, wrl=(wr - wr_hi.astype(F32)).astype(BF),
        br=_row(jnp.pad(b_router, (0, pad_r))))
    moe = dict(wgate=w_gate, bgate=b_gate.reshape(N_EXPERTS, 1, D_MODEL), wup=w_up,
               bup=b_up.reshape(N_EXPERTS, 1, D_MODEL), wdown=w_down, bdown=b_down.reshape(N_EXPERTS, 1, D_MODEL))
    return proj, conv, merge, moe, w_mem_kv.astype(BF), _row(ln2_g), _row(ln2_b)


def _channel(x2d, yc, yf, ym, merge_w, moe_w, g2, b2):
    h_rows, top_e, gate = _merge(x2d, yc, yf, ym, merge_w)
    y_rows = _moe(h_rows, top_e, moe_w)
    return _combine(h_rows, y_rows, gate, g2, b2)


def kernel(x_prompt, x_sample, mem_prompt, cache_k, cache_v, cache_logf, page_table, cache_mem_k, cache_mem_v, state_conv, w_in, b_in, b_forget, conv_w, conv_b, conv_ln_g, conv_ln_b, w_conv_out, b_conv_out, w_fox_out, w_mem_kv, w_mem_out, w_out, b_out, ln1_g, ln1_b, w_router, b_router, w_gate, b_gate, w_up, b_up, w_down, b_down, ln2_g, ln2_b):
    proj_w, conv_w_, merge_w, moe_w, wkv, g2, b2 = _prep_weights(
        w_in, b_in, b_forget, conv_w, conv_b, conv_ln_g, conv_ln_b, w_conv_out, b_conv_out, w_fox_out, w_mem_kv,
        w_mem_out, w_out, b_out, ln1_g, ln1_b, w_router, b_router, w_gate, b_gate, w_up, b_up, w_down, b_down,
        ln2_g, ln2_b)
    b, l, d = x_prompt.shape
    db, t, _ = x_sample.shape
    hist_len = CONV_WIDTH - 1

    xp = x_prompt.reshape(b * l, d)
    u, q, k, v, kb, vb, logf, _, qm, aq, ak = _in_proj(xp, l, proj_w)
    u3 = u.reshape(b, l, CONV_CH)
    yc = _conv_prompt(u3, jnp.zeros((b, HIST_ROWS, CONV_CH), F32), conv_w_)
    yf = _fox_prompt(q.reshape(b, l, FOX_WIDTH), kb.reshape(b, l, FOX_WIDTH), vb.reshape(b, l, FOX_WIDTH),
                     aq.reshape(b, l, FOX_HEADS * LANES), ak.reshape(b, l, FOX_HEADS * LANES))
    mk, mv = _mem_kv(mem_prompt.reshape(b * MEM_TOKENS, d), wkv)
    mk3 = mk.reshape(b, MEM_TOKENS, MEM_WIDTH)
    mv3 = mv.reshape(b, MEM_TOKENS, MEM_WIDTH)
    ym = _mem_attend(qm.reshape(b, l, MEM_WIDTH), mk3, mv3, 1, min(512, l))
    y_prompt = _channel(xp, yc.reshape(b * l, CONV_CH), yf.reshape(b * l, FOX_WIDTH), ym.reshape(b * l, MEM_WIDTH),
                        merge_w, moe_w, g2, b2).reshape(b, l, d)

    xs = x_sample.reshape(db * t, d)
    us, qs, ks, vs, ksb, vsb, logfs, cums, qms, _, _ = _in_proj(xs, t, proj_w)
    us_ext = jnp.concatenate([state_conv.astype(F32), us.reshape(db, t, CONV_CH)], axis=1)
    ycs = _conv_sample(us_ext.transpose(1, 0, 2), conv_w_).transpose(1, 0, 2)
    n_phys = cache_logf.shape[0]
    suf, tot = _page_suffix(cache_logf.reshape(n_phys, PAGE_COLS))
    yfs = _fox_sample(qs, ksb, vsb, cums, cache_k, cache_v, suf.reshape(n_phys, 1, PAGE_COLS),
                      tot.reshape(n_phys, 1, PAGE_COLS), page_table)
    t_pad = 2 * SUBLANES
    qms3 = jnp.pad(qms.reshape(db, t, MEM_WIDTH), ((0, 0), (0, t_pad - t), (0, 0)))
    yms = _mem_attend(qms3, cache_mem_k.reshape(db, MEM_TOKENS, MEM_WIDTH),
                      cache_mem_v.reshape(db, MEM_TOKENS, MEM_WIDTH), 8, t_pad)[:, :t]
    y_sample = _channel(xs, ycs.reshape(db * t, CONV_CH), yfs.reshape(db * t, FOX_WIDTH),
                        yms.reshape(db * t, MEM_WIDTH), merge_w, moe_w, g2, b2).reshape(db, t, d)

    heads = lambda a, n, s: a.reshape(n, s, FOX_HEADS, FOX_HEAD_DIM)
    return (y_prompt, y_sample,
            heads(k, b, l), heads(v, b, l), logf.reshape(b, l, FOX_HEADS),
            mk.reshape(b, MEM_TOKENS, MEM_HEADS, MEM_HEAD_DIM), mv.reshape(b, MEM_TOKENS, MEM_HEADS, MEM_HEAD_DIM),
            u3[:, l - hist_len:, :],
            heads(ks, db, t), heads(vs, db, t), logfs.reshape(db, t, FOX_HEADS),
            us_ext[:, t:, :])
```

```python
import functools

import numpy as np
import jax
import jax.numpy as jnp
from jax import lax
from jax.experimental import pallas as pl
from jax.experimental.pallas import tpu as pltpu

D_MODEL = 1024
CONV_CH = 512
CONV_WIDTH = 31
FOX_HEADS = 8
FOX_HEAD_DIM = 64
FOX_WIDTH = FOX_HEADS * FOX_HEAD_DIM
MEM_HEADS = 4
MEM_HEAD_DIM = 128
MEM_WIDTH = MEM_HEADS * MEM_HEAD_DIM
MEM_TOKENS = 256
N_EXPERTS = 32
TOP_K = 4
PAGE_SIZE = 128
SWIGLU_LIMIT = 7.0
SWIGLU_ALPHA = 1.702
LN_EPS = 1e-5
DEEPNORM_ALPHA = 2.0 ** 0.25

LANES = 128
SUBLANES = 8
HIST_ROWS = 32
MIB = 1024 * 1024

BF = jnp.bfloat16
F32 = jnp.float32
NEG_INF = float("-inf")


def _dot(a, b):
    return jnp.dot(a, b, preferred_element_type=F32)


def _dot_nt(a, b):
    return lax.dot_general(a, b, (((1,), (1,)), ((), ())), preferred_element_type=F32)


def _params(vmem_mib, n_axes, **kw):
    return pltpu.CompilerParams(dimension_semantics=("arbitrary",) * n_axes,
                                vmem_limit_bytes=vmem_mib * MIB, **kw)


def _full(shape):
    nd = len(shape)
    return pl.BlockSpec(shape, lambda *_: (0,) * nd)


def _split3(x):
    hi = x.astype(BF)
    r1 = x - hi.astype(F32)
    mid = r1.astype(BF)
    lo = (r1 - mid.astype(F32)).astype(BF)
    return hi, mid, lo


def _layernorm(x, g, b):
    mu = jnp.mean(x, axis=-1, keepdims=True)
    xc = x - mu
    var = jnp.mean(xc * xc, axis=-1, keepdims=True)
    return xc * lax.rsqrt(var + LN_EPS) * g + b


def _in_proj_kernel(x_ref, tri_ref, wglu_ref, wqkv_ref, wf_ref, wqm_ref, bglu_ref, bqkv_ref, bf_ref, bfg_ref,
                    bqm_ref, pq_ref, pk_ref, cq_ref, ck_ref, u_ref, q_ref, k_ref, v_ref, kb_ref, vb_ref, logf_ref,
                    cum_ref, qm_ref, aq_ref, ak_ref, carry_ref, *, tm, seq_len):
    i = pl.program_id(0)
    xb = x_ref[...].astype(BF)
    glu = _dot(xb, wglu_ref[...]) + bglu_ref[...]
    u_ref[...] = glu[:, :CONV_CH] * jax.nn.sigmoid(glu[:, CONV_CH:])
    qkv = _dot(xb, wqkv_ref[...]) + bqkv_ref[...]
    q_ref[...] = (qkv[:, :FOX_WIDTH] * (FOX_HEAD_DIM ** -0.5)).astype(BF)
    k = qkv[:, FOX_WIDTH:2 * FOX_WIDTH]
    v = qkv[:, 2 * FOX_WIDTH:]
    k_ref[...] = k
    v_ref[...] = v
    kb_ref[...] = k.astype(BF)
    vb_ref[...] = v.astype(BF)
    qm_ref[...] = (_dot(xb, wqm_ref[...]) + bqm_ref[...]).astype(BF)
    f = (_dot(xb, wf_ref[...]) + bf_ref[...]) + bfg_ref[...]
    lf = jnp.minimum(f, 0.0) - jnp.log1p(jnp.exp(-jnp.abs(f)))
    logf_ref[...] = lf[:, :FOX_HEADS]
    hi, mid, lo = _split3(lf)
    tri = tri_ref[...]
    cum = _dot(tri, hi) + _dot(tri, mid) + _dot(tri, lo)
    if seq_len > tm:
        @pl.when(i % (seq_len // tm) == 0)
        def _():
            carry_ref[...] = jnp.zeros_like(carry_ref)
        cum = cum + carry_ref[...]
        carry_ref[...] = cum[tm - 1:tm, :]
    cum_ref[...] = cum[:, :FOX_HEADS]
    parts = _split3(cum)
    aq = cq_ref[...] + _dot(parts[0], pq_ref[0]) + _dot(parts[1], pq_ref[1]) + _dot(parts[2], pq_ref[2])
    ak = ck_ref[...] + _dot(parts[0], pk_ref[0]) + _dot(parts[1], pk_ref[1]) + _dot(parts[2], pk_ref[2])
    aq_ref[...] = aq.astype(BF)
    ak_ref[...] = ak.astype(BF)


def _aug_lane(h):
    return h * LANES + (FOX_HEAD_DIM if h % 2 == 0 else 0)


def _aug_constants():
    pq = np.zeros((3, LANES, FOX_HEADS * LANES), np.float32)
    pk = np.zeros((3, LANES, FOX_HEADS * LANES), np.float32)
    cq = np.zeros((1, FOX_HEADS * LANES), np.float32)
    ck = np.zeros((1, FOX_HEADS * LANES), np.float32)
    for h in range(FOX_HEADS):
        base = _aug_lane(h)
        for j in range(3):
            pq[j, h, base + j] = 1.0
            pk[j, h, base + 3 + j] = -1.0
            cq[0, base + 3 + j] = 1.0
            ck[0, base + j] = 1.0
    return jnp.asarray(pq, BF), jnp.asarray(pk, BF), jnp.asarray(cq), jnp.asarray(ck)


def _in_proj(x2d, seq_len, w):
    n = x2d.shape[0]
    tm = min(512, n)
    lc = min(seq_len, tm)
    r = np.arange(tm)
    tri = jnp.asarray(((r[None, :] <= r[:, None]) & (r[None, :] // lc == r[:, None] // lc)).astype(np.float32), BF)
    tok = lambda width: pl.BlockSpec((tm, width), lambda i: (i, 0))
    out_shape = (jax.ShapeDtypeStruct((n, CONV_CH), F32), jax.ShapeDtypeStruct((n, FOX_WIDTH), BF),
                 jax.ShapeDtypeStruct((n, FOX_WIDTH), F32), jax.ShapeDtypeStruct((n, FOX_WIDTH), F32),
                 jax.ShapeDtypeStruct((n, FOX_WIDTH), BF), jax.ShapeDtypeStruct((n, FOX_WIDTH), BF),
                 jax.ShapeDtypeStruct((n, FOX_HEADS), F32), jax.ShapeDtypeStruct((n, FOX_HEADS), F32),
                 jax.ShapeDtypeStruct((n, MEM_WIDTH), BF),
                 jax.ShapeDtypeStruct((n, FOX_HEADS * LANES), BF), jax.ShapeDtypeStruct((n, FOX_HEADS * LANES), BF))
    ins = (x2d, tri, w["wglu"], w["wqkv"], w["wf"], w["wqm"], w["bglu"], w["bqkv"], w["bf"], w["bfg"], w["bqm"],
           *_aug_constants())
    return pl.pallas_call(
        functools.partial(_in_proj_kernel, tm=tm, seq_len=seq_len),
        grid=(n // tm,),
        in_specs=[tok(D_MODEL)] + [_full(a.shape) for a in ins[1:]],
        out_specs=(tok(CONV_CH), tok(FOX_WIDTH), tok(FOX_WIDTH), tok(FOX_WIDTH), tok(FOX_WIDTH), tok(FOX_WIDTH),
                   tok(FOX_HEADS), tok(FOX_HEADS), tok(MEM_WIDTH), tok(FOX_HEADS * LANES), tok(FOX_HEADS * LANES)),
        out_shape=out_shape,
        scratch_shapes=[pltpu.VMEM((1, LANES), F32)],
        compiler_params=_params(56, 1),
    )(*ins)


def _conv_post(y, cb_ref, g_ref, b_ref):
    y = _layernorm(y + cb_ref[...], g_ref[...], b_ref[...])
    return (y * jax.nn.sigmoid(y)).astype(BF)


def _conv_prompt_kernel(u_ref, prev_ref, hist_ref, w_ref, cb_ref, g_ref, b_ref, o_ref, win_ref, y_ref, *, tm):
    i = pl.program_id(1)
    win_ref[0:HIST_ROWS, :] = jnp.where(i == 0, hist_ref[0], prev_ref[0])
    win_ref[HIST_ROWS:, :] = u_ref[0]
    first = HIST_ROWS - (CONV_WIDTH - 1)
    for c in range(CONV_CH // LANES):
        cs = slice(c * LANES, (c + 1) * LANES)
        acc = jnp.zeros((tm, LANES), F32)
        for j in range(CONV_WIDTH):
            acc = acc + w_ref[j:j + 1, cs] * win_ref[first + j:first + j + tm, cs]
        y_ref[:, cs] = acc
    o_ref[0] = _conv_post(y_ref[...], cb_ref, g_ref, b_ref)


def _conv_prompt(u3, hist, cw):
    b, l, _ = u3.shape
    tm = 256
    per = tm // HIST_ROWS
    vec = _full((1, CONV_CH))
    return pl.pallas_call(
        functools.partial(_conv_prompt_kernel, tm=tm),
        grid=(b, l // tm),
        in_specs=[pl.BlockSpec((1, tm, CONV_CH), lambda bi, i: (bi, i, 0)),
                  pl.BlockSpec((1, HIST_ROWS, CONV_CH), lambda bi, i: (bi, jnp.maximum(i * per - 1, 0), 0)),
                  pl.BlockSpec((1, HIST_ROWS, CONV_CH), lambda bi, i: (bi, 0, 0)),
                  _full((HIST_ROWS, CONV_CH)), vec, vec, vec],
        out_specs=pl.BlockSpec((1, tm, CONV_CH), lambda bi, i: (bi, i, 0)),
        out_shape=jax.ShapeDtypeStruct((b, l, CONV_CH), BF),
        scratch_shapes=[pltpu.VMEM((tm + HIST_ROWS, CONV_CH), F32), pltpu.VMEM((tm, CONV_CH), F32)],
        compiler_params=_params(32, 2),
    )(u3, u3, hist, cw["w"], cw["cb"], cw["g"], cw["b"])


def _conv_sample_kernel(x_ref, w_ref, cb_ref, g_ref, b_ref, o_ref, *, steps):
    for t in range(steps):
        acc = jnp.zeros(x_ref.shape[1:], F32)
        for j in range(CONV_WIDTH):
            acc = acc + w_ref[j:j + 1, :] * x_ref[t + j]
        o_ref[t] = _conv_post(acc, cb_ref, g_ref, b_ref)


def _conv_sample(u_ext_t, cw):
    rows, b, _ = u_ext_t.shape
    steps = rows - (CONV_WIDTH - 1)
    bb = min(64, b)
    vec = _full((1, CONV_CH))
    return pl.pallas_call(
        functools.partial(_conv_sample_kernel, steps=steps),
        grid=(b // bb,),
        in_specs=[pl.BlockSpec((rows, bb, CONV_CH), lambda i: (0, i, 0)), _full((HIST_ROWS, CONV_CH)), vec, vec, vec],
        out_specs=pl.BlockSpec((steps, bb, CONV_CH), lambda i: (0, i, 0)),
        out_shape=jax.ShapeDtypeStruct((steps, b, CONV_CH), BF),
        compiler_params=_params(32, 1),
    )(u_ext_t, cw["w"], cw["cb"], cw["g"], cw["b"])


def _fox_prompt_kernel(qi_ref, kj_ref, q_ref, k_ref, v_ref, aq_ref, ak_ref, o_ref, qa_ref, ka_ref, va_ref, m_ref,
                       acc_ref, *, tile, sub):
    t = pl.program_id(2)
    qi = qi_ref[t]
    kj = kj_ref[t]
    nsub = tile // sub
    lane = lax.broadcasted_iota(jnp.int32, (1, LANES), 1)
    in_head = [(lane >= hh * FOX_HEAD_DIM) & (lane < (hh + 1) * FOX_HEAD_DIM) for hh in range(2)]
    sum_lane = [_aug_lane(hh) % LANES for hh in range(2)]

    for hh in range(2):
        ka_ref[hh] = jnp.where(in_head[hh], k_ref[0], ak_ref[0, :, hh * LANES:(hh + 1) * LANES])
        va_ref[hh] = jnp.where(in_head[hh], v_ref[0], jnp.where(lane == sum_lane[hh], 1.0, 0.0).astype(BF))

    @pl.when(kj == 0)
    def _():
        m_ref[...] = jnp.full_like(m_ref, NEG_INF)
        acc_ref[...] = jnp.zeros_like(acc_ref)
        for hh in range(2):
            qa_ref[hh] = jnp.where(in_head[hh], q_ref[0], aq_ref[0, :, hh * LANES:(hh + 1) * LANES])

    def attend(hh, i2, j2, masked):
        rows = slice(i2 * sub, (i2 + 1) * sub)
        cols = slice(j2 * sub, (j2 + 1) * sub)
        s = _dot_nt(qa_ref[hh, rows, :], ka_ref[hh, cols, :])
        if masked:
            r_id = lax.broadcasted_iota(jnp.int32, (sub, sub), 0)
            c_id = lax.broadcasted_iota(jnp.int32, (sub, sub), 1)
            s = jnp.where(c_id <= r_id, s, NEG_INF)
        m_prev = m_ref[hh, rows, :]
        m_next = jnp.maximum(m_prev, jnp.max(s, axis=1, keepdims=True))
        alpha = jnp.exp(m_prev - m_next)
        p = jnp.exp(s - jnp.concatenate([m_next] * (sub // LANES), axis=1))
        acc_ref[hh, rows, :] = alpha * acc_ref[hh, rows, :] + _dot(p.astype(BF), va_ref[hh, cols, :])
        m_ref[hh, rows, :] = m_next

    @pl.when(kj < qi)
    def _():
        for hh in range(2):
            for i2 in range(nsub):
                for j2 in range(nsub):
                    attend(hh, i2, j2, False)

    @pl.when(kj == qi)
    def _():
        for hh in range(2):
            for i2 in range(nsub):
                for j2 in range(i2 + 1):
                    attend(hh, i2, j2, j2 == i2)
        outs = []
        for hh in range(2):
            acc = acc_ref[hh]
            outs.append(acc / acc[:, sum_lane[hh]:sum_lane[hh] + 1])
        o_ref[0] = jnp.where(in_head[0], outs[0], outs[1]).astype(BF)


def _fox_prompt(q3, k3, v3, aq3, ak3):
    b, l, _ = q3.shape
    tile = min(1024, l)
    sub = min(512, tile)
    nq = l // tile
    pairs = FOX_HEADS // 2
    qi = np.concatenate([np.full(i + 1, i) for i in range(nq)]).astype(np.int32)
    kj = np.concatenate([np.arange(i + 1) for i in range(nq)]).astype(np.int32)
    grid_spec = pltpu.PrefetchScalarGridSpec(
        num_scalar_prefetch=2,
        grid=(b, pairs, len(qi)),
        in_specs=[pl.BlockSpec((1, tile, LANES), lambda bi, p, t, qi_r, kj_r: (bi, qi_r[t], p)),
                  pl.BlockSpec((1, tile, LANES), lambda bi, p, t, qi_r, kj_r: (bi, kj_r[t], p)),
                  pl.BlockSpec((1, tile, LANES), lambda bi, p, t, qi_r, kj_r: (bi, kj_r[t], p)),
                  pl.BlockSpec((1, tile, 2 * LANES), lambda bi, p, t, qi_r, kj_r: (bi, qi_r[t], p)),
                  pl.BlockSpec((1, tile, 2 * LANES), lambda bi, p, t, qi_r, kj_r: (bi, kj_r[t], p))],
        out_specs=pl.BlockSpec((1, tile, LANES), lambda bi, p, t, qi_r, kj_r: (bi, qi_r[t], p)),
        scratch_shapes=[pltpu.VMEM((2, tile, LANES), BF)] * 3 + [pltpu.VMEM((2, tile, LANES), F32)] * 2)
    return pl.pallas_call(
        functools.partial(_fox_prompt_kernel, tile=tile, sub=sub),
        grid_spec=grid_spec,
        out_shape=jax.ShapeDtypeStruct((b, l, FOX_WIDTH), BF),
        compiler_params=_params(40, 3),
    )(jnp.asarray(qi), jnp.asarray(kj), q3, k3, v3, aq3, ak3)


def _page_suffix_kernel(x_ref, upper_ref, ones_ref, suf_ref, tot_ref):
    hi, mid, lo = _split3(x_ref[...])
    up = upper_ref[...]
    on = ones_ref[...]
    suf_ref[...] = _dot(hi, up) + _dot(mid, up) + _dot(lo, up)
    tot_ref[...] = _dot(hi, on) + _dot(mid, on) + _dot(lo, on)


def _page_suffix(logf_t):
    rows = logf_t.shape[0]
    tr = min(2048, rows)
    kk = np.arange(PAGE_SIZE)
    upper = jnp.asarray((kk[:, None] > kk[None, :]).astype(np.float32), BF)
    ones = jnp.ones((PAGE_SIZE, PAGE_SIZE), BF)
    spec = pl.BlockSpec((tr, PAGE_SIZE), lambda i: (i, 0))
    return pl.pallas_call(
        _page_suffix_kernel,
        grid=(rows // tr,),
        in_specs=[spec, _full((PAGE_SIZE, PAGE_SIZE)), _full((PAGE_SIZE, PAGE_SIZE))],
        out_specs=(spec, spec),
        out_shape=(jax.ShapeDtypeStruct((rows, PAGE_SIZE), F32),) * 2,
        compiler_params=_params(32, 1),
    )(logf_t, upper, ones)


def _fox_sample_kernel(pt_ref, qrep_ref, cq_ref, ct_ref, kn_ref, vn_ref, *rest, pages, steps):
    k_refs = rest[:pages]
    v_refs = rest[pages:2 * pages]
    suf_refs = rest[2 * pages:3 * pages]
    tot_refs = rest[3 * pages:4 * pages]
    o_ref, m_ref, l_ref, acc_ref, carry_ref = rest[4 * pages:]
    j = pl.program_id(1)
    rows = steps * FOX_HEADS
    row_id = lax.broadcasted_iota(jnp.int32, (rows, FOX_WIDTH), 0)
    col_id = lax.broadcasted_iota(jnp.int32, (rows, FOX_WIDTH), 1)
    head_mask = (col_id // FOX_HEAD_DIM) == (row_id % FOX_HEADS)
    qrep = qrep_ref[0]
    qe = jnp.where(head_mask, qrep, jnp.zeros_like(qrep))
    cq = cq_ref[0]

    def update(s, pv_fn):
        m_prev = m_ref[...]
        m_next = jnp.maximum(m_prev, jnp.max(s, axis=1, keepdims=True))
        alpha = jnp.exp(m_prev - m_next)
        p = jnp.exp(s - jnp.concatenate([m_next] * (s.shape[1] // LANES), axis=1))
        l_ref[...] = alpha * l_ref[...] + jnp.sum(p, axis=1, keepdims=True)
        acc_ref[...] = jnp.concatenate([alpha] * (FOX_WIDTH // LANES), axis=1) * acc_ref[...] + pv_fn(p.astype(BF))
        m_ref[...] = m_next

    @pl.when(j == 0)
    def _():
        m_ref[...] = jnp.full_like(m_ref, NEG_INF)
        l_ref[...] = jnp.zeros_like(l_ref)
        acc_ref[...] = jnp.zeros_like(acc_ref)
        carry_ref[...] = jnp.zeros_like(carry_ref)
        s = _dot_nt(qe, kn_ref[0]) + (cq - jnp.concatenate([ct_ref[0]] * steps, axis=0))
        r_id = lax.broadcasted_iota(jnp.int32, (rows, LANES), 0)
        c_id = lax.broadcasted_iota(jnp.int32, (rows, LANES), 1)
        s = jnp.where(c_id <= r_id // FOX_HEADS, s, NEG_INF)
        update(s, lambda p: _dot(p, vn_ref[0]))

    carry = carry_ref[...]
    scores = [None] * pages
    for r in reversed(range(pages)):
        bias = cq + carry + jnp.concatenate([suf_refs[r][0]] * steps, axis=0)
        scores[r] = _dot(qe, k_refs[r][0].reshape(FOX_WIDTH, PAGE_SIZE).astype(BF)) + bias
        carry = carry + jnp.concatenate([tot_refs[r][0]] * steps, axis=0)
    carry_ref[...] = carry

    def pv_pages(p):
        out = None
        for r in range(pages):
            term = _dot_nt(p[:, r * PAGE_SIZE:(r + 1) * PAGE_SIZE],
                           v_refs[r][0].reshape(FOX_WIDTH, PAGE_SIZE).astype(BF))
            out = term if out is None else out + term
        return out

    update(jnp.concatenate(scores, axis=1), pv_pages)

    @pl.when(j == pl.num_programs(1) - 1)
    def _():
        o = jnp.where(head_mask, acc_ref[...] / jnp.concatenate([l_ref[...]] * (FOX_WIDTH // LANES), axis=1), 0.0)
        o_ref[0] = jnp.sum(o.reshape(steps, FOX_HEADS, FOX_WIDTH), axis=1).astype(BF)


def _fox_sample(q3, kn3, vn3, cum3, cache_kt, cache_vt, suf, tot, page_table):
    db, steps, _ = q3.shape
    n_pages = page_table.shape[1]
    pages = min(16, n_pages)
    rows = steps * FOX_HEADS
    qrep = jnp.repeat(q3, FOX_HEADS, axis=1)
    cq = jnp.broadcast_to(cum3.reshape(db, rows, 1), (db, rows, LANES))
    ct = jnp.pad(cum3.transpose(0, 2, 1), ((0, 0), (0, 0), (0, LANES - steps)))
    kn = jnp.pad(kn3, ((0, 0), (0, LANES - steps), (0, 0)))
    vn = jnp.pad(vn3, ((0, 0), (0, LANES - steps), (0, 0)))
    n_steps = n_pages // pages

    def page_map(r, nd):
        return lambda bi, j, pt: (pt[bi, (n_steps - 1 - j) * pages + r],) + (0,) * (nd - 1)

    seq = lambda shape: pl.BlockSpec((1,) + shape, lambda bi, j, pt: (bi, 0, 0))
    in_specs = [seq((rows, FOX_WIDTH)), seq((rows, LANES)), seq((FOX_HEADS, LANES)),
                seq((LANES, FOX_WIDTH)), seq((LANES, FOX_WIDTH))]
    in_specs += [pl.BlockSpec((1, FOX_HEADS, FOX_HEAD_DIM, PAGE_SIZE), page_map(r, 4)) for r in range(pages)] * 2
    in_specs += [pl.BlockSpec((1, FOX_HEADS, PAGE_SIZE), page_map(r, 3)) for r in range(pages)] * 2
    grid_spec = pltpu.PrefetchScalarGridSpec(
        num_scalar_prefetch=1,
        grid=(db, n_steps),
        in_specs=in_specs,
        out_specs=pl.BlockSpec((1, steps, FOX_WIDTH), lambda bi, j, pt: (bi, 0, 0)),
        scratch_shapes=[pltpu.VMEM((rows, LANES), F32), pltpu.VMEM((rows, LANES), F32),
                        pltpu.VMEM((rows, FOX_WIDTH), F32), pltpu.VMEM((rows, LANES), F32)])
    return pl.pallas_call(
        functools.partial(_fox_sample_kernel, pages=pages, steps=steps),
        grid_spec=grid_spec,
        out_shape=jax.ShapeDtypeStruct((db, steps, FOX_WIDTH), BF),
        compiler_params=_params(40, 2),
    )(page_table, qrep, cq, ct, kn, vn, *([cache_kt] * pages), *([cache_vt] * pages), *([suf] * pages),
      *([tot] * pages))


def _mem_kv_kernel(m_ref, w_ref, k_ref, v_ref):
    kv = _dot(m_ref[...].astype(BF), w_ref[...])
    k_ref[...] = kv[:, :MEM_WIDTH]
    v_ref[...] = kv[:, MEM_WIDTH:]


def _mem_kv(mem2d, w_bf):
    n = mem2d.shape[0]
    tm = min(256, n)
    out = pl.BlockSpec((tm, MEM_WIDTH), lambda i: (i, 0))
    return pl.pallas_call(
        _mem_kv_kernel,
        grid=(n // tm,),
        in_specs=[pl.BlockSpec((tm, D_MODEL), lambda i: (i, 0)), _full(w_bf.shape)],
        out_specs=(out, out),
        out_shape=(jax.ShapeDtypeStruct((n, MEM_WIDTH), F32),) * 2,
        compiler_params=_params(32, 1),
    )(mem2d, w_bf)


def _mem_attend_kernel(q_ref, k_ref, v_ref, o_ref, *, bb):
    for b in range(bb):
        for h in range(MEM_HEADS):
            hs = slice(h * MEM_HEAD_DIM, (h + 1) * MEM_HEAD_DIM)
            s = _dot_nt(q_ref[b, :, hs], k_ref[b, :, hs].astype(BF)) * (MEM_HEAD_DIM ** -0.5)
            p = jnp.exp(s - jnp.max(s, axis=1, keepdims=True))
            den = jnp.sum(p, axis=1, keepdims=True)
            o_ref[b, :, hs] = (_dot(p.astype(BF), v_ref[b, :, hs].astype(BF)) / den).astype(BF)


def _mem_attend(qm3, mk3, mv3, bb, tq):
    b, l, _ = qm3.shape
    kv = pl.BlockSpec((bb, MEM_TOKENS, MEM_WIDTH), lambda bi, i: (bi, 0, 0))
    qs = pl.BlockSpec((bb, tq, MEM_WIDTH), lambda bi, i: (bi, i, 0))
    return pl.pallas_call(
        functools.partial(_mem_attend_kernel, bb=bb),
        grid=(b // bb, l // tq),
        in_specs=[qs, kv, kv],
        out_specs=qs,
        out_shape=jax.ShapeDtypeStruct((b, l, MEM_WIDTH), BF),
        compiler_params=_params(40, 2),
    )(qm3, mk3, mv3)


def _merge_kernel(x_ref, yc_ref, yf_ref, ym_ref, wg_ref, bg_ref, wc_ref, bc_ref, wfo_ref, wmo_ref, wo_ref, bo_ref,
                  g1_ref, b1_ref, wrh_ref, wrl_ref, br_ref, h_ref, e_ref, gate_ref, *, tm):
    x = x_ref[...]
    xb = x.astype(BF)

    def gate(c):
        cs = slice(c * D_MODEL, (c + 1) * D_MODEL)
        return jax.nn.sigmoid(_dot(xb, wg_ref[:, cs]) + bg_ref[:, cs])

    mix = gate(0) * (_dot(yc_ref[...], wc_ref[...]) + bc_ref[...])
    mix = mix + gate(1) * _dot(yf_ref[...], wfo_ref[...])
    mix = mix + gate(2) * _dot(ym_ref[...], wmo_ref[...])
    pre = DEEPNORM_ALPHA * x + (_dot(mix.astype(BF), wo_ref[...]) + bo_ref[...])
    h = _layernorm(pre, g1_ref[...], b1_ref[...])
    for j in range(D_MODEL // LANES):
        h_ref[pl.ds(j, tm, stride=SUBLANES), :] = h[:, j * LANES:(j + 1) * LANES]

    h_hi = h.astype(BF)
    h_lo = (h - h_hi.astype(F32)).astype(BF)
    logits = _dot(h_hi, wrh_ref[...]) + _dot(h_hi, wrl_ref[...]) + _dot(h_lo, wrh_ref[...]) + br_ref[...]
    lane = lax.broadcasted_iota(jnp.int32, (tm, LANES), 1)
    lane_f = lane.astype(F32)
    work = jnp.where(lane < N_EXPERTS, logits, NEG_INF)
    vals, idxs = [], []
    for _ in range(TOP_K):
        mx = jnp.max(work, axis=1, keepdims=True)
        idx = jnp.min(jnp.where(work == mx, lane_f, float(LANES)), axis=1, keepdims=True)
        vals.append(mx)
        idxs.append(idx)
        work = jnp.where(lane_f == idx, NEG_INF, work)
    exps = [jnp.exp(v - vals[0]) for v in vals]
    den = exps[0] + exps[1] + exps[2] + exps[3]
    e_out = jnp.zeros((tm, LANES), F32)
    g_out = jnp.zeros((tm, LANES), F32)
    for kk in range(TOP_K):
        e_out = jnp.where(lane == kk, idxs[kk], e_out)
        g_out = jnp.where(lane == kk, exps[kk] / den, g_out)
    e_ref[...] = e_out[:, :TOP_K].astype(jnp.int32)
    gate_ref[...] = g_out[:, :TOP_K]


def _merge(x2d, yc, yf, ym, w):
    n = x2d.shape[0]
    tm = min(256, n)
    tok = lambda width: pl.BlockSpec((tm, width), lambda i: (i, 0))
    ws = (w["wg"], w["bg"], w["wc"], w["bc"], w["wfo"], w["wmo"], w["wo"], w["bo"], w["g1"], w["b1"],
          w["wrh"], w["wrl"], w["br"])
    return pl.pallas_call(
        functools.partial(_merge_kernel, tm=tm),
        grid=(n // tm,),
        in_specs=[tok(D_MODEL), tok(CONV_CH), tok(FOX_WIDTH), tok(MEM_WIDTH)] + [_full(a.shape) for a in ws],
        out_specs=(pl.BlockSpec((tm * SUBLANES, LANES), lambda i: (i, 0)), tok(TOP_K), tok(TOP_K)),
        out_shape=(jax.ShapeDtypeStruct((n * SUBLANES, LANES), F32), jax.ShapeDtypeStruct((n, TOP_K), jnp.int32),
                   jax.ShapeDtypeStruct((n, TOP_K), F32)),
        compiler_params=_params(56, 1),
    )(x2d, yc, yf, ym, *ws)


MOE_ROWS = 256
IDX_SLOTS = 3
DMA_UNROLL_BITS = 3
DMA_UNROLL = 1 << DMA_UNROLL_BITS
LANE_BITS = LANES.bit_length() - 1
TOP_K_BITS = TOP_K.bit_length() - 1


def _moe_kernel(blk_e_ref, nvalid_ref, nreal_ref, rows_hbm, h_hbm, wg_ref, bg_ref, wu_ref, bu_ref, wd_ref, bd_ref,
                y_hbm, idx_ref, xbuf, ybuf, wgb, wub, wdb, idx_sem, in_sem, out_sem):
    i = pl.program_id(0)
    nreal = nreal_ref[0]
    tile = SUBLANES

    def idx_slot(blk):
        return lax.rem(blk, IDX_SLOTS)

    def idx_copy(blk):
        slot = idx_slot(blk)
        return pltpu.make_async_copy(rows_hbm.at[blk], idx_ref.at[slot], idx_sem.at[slot])

    def row_index(islot, r):
        return idx_ref[islot, r >> LANE_BITS, r & (LANES - 1)]

    def gather_row(islot, slot, r):
        tok = row_index(islot, r) >> TOP_K_BITS
        return pltpu.make_async_copy(h_hbm.at[pl.ds(pl.multiple_of(tok * tile, tile), tile)],
                                     xbuf.at[slot, pl.ds(pl.multiple_of(r * tile, tile), tile)],
                                     in_sem.at[slot])

    def scatter_row(islot, slot, r):
        dst = row_index(islot, r)
        return pltpu.make_async_copy(ybuf.at[slot, pl.ds(pl.multiple_of(r * tile, tile), tile)],
                                     y_hbm.at[pl.ds(pl.multiple_of(dst * tile, tile), tile)],
                                     out_sem.at[slot])

    def start_gather(blk):
        islot = idx_slot(blk)
        slot = blk & 1

        def body(c, carry):
            for u in range(DMA_UNROLL):
                gather_row(islot, slot, c * DMA_UNROLL + u).start()
            return carry
        lax.fori_loop(0, MOE_ROWS // DMA_UNROLL, body, 0)

    def wait_rows(sem, slot):
        pltpu.make_async_copy(xbuf.at[slot], ybuf.at[slot], sem.at[slot]).wait()

    def wait_scatter(blk):
        slot = blk & 1
        nv = nvalid_ref[blk]

        @pl.when(nv == MOE_ROWS)
        def _():
            wait_rows(out_sem, slot)

        @pl.when(nv < MOE_ROWS)
        def _():
            def body(r, c):
                pltpu.make_async_copy(ybuf.at[slot, pl.ds(0, tile)], y_hbm.at[pl.ds(0, tile)],
                                      out_sem.at[slot]).wait()
                return c
            lax.fori_loop(0, nv, body, 0)

    @pl.when((i == 0) & (nreal > 0))
    def _():
        idx_copy(0).start()
        idx_copy(0).wait()
        start_gather(0)

        @pl.when(nreal > 1)
        def _():
            idx_copy(1).start()

    @pl.when(i + 1 < nreal)
    def _():
        idx_copy(i + 1).wait()
        start_gather(i + 1)

        @pl.when(i + 2 < nreal)
        def _():
            idx_copy(i + 2).start()

    changed = (i == 0) | (blk_e_ref[i] != blk_e_ref[jnp.maximum(i - 1, 0)])

    @pl.when(changed & (i < nreal))
    def _():
        wgb[...] = wg_ref[0].astype(BF)
        wub[...] = wu_ref[0].astype(BF)
        wdb[...] = wd_ref[0].astype(BF)

    @pl.when(i < nreal)
    def _():
        slot = i & 1
        islot = idx_slot(i)
        wait_rows(in_sem, slot)

        @pl.when(i >= 2)
        def _():
            wait_scatter(i - 2)

        x = jnp.concatenate([xbuf[slot, pl.ds(j, MOE_ROWS, stride=SUBLANES), :] for j in range(D_MODEL // LANES)],
                            axis=1).astype(BF)
        a = jnp.minimum(_dot(x, wgb[...]) + bg_ref[0], SWIGLU_LIMIT)
        u = jnp.clip(_dot(x, wub[...]) + bu_ref[0], -SWIGLU_LIMIT, SWIGLU_LIMIT)
        hid = (u + 1.0) * a * jax.nn.sigmoid(SWIGLU_ALPHA * a)
        y = _dot(hid.astype(BF), wdb[...]) + bd_ref[0]
        for j in range(D_MODEL // LANES):
            ybuf[slot, pl.ds(j, MOE_ROWS, stride=SUBLANES), :] = y[:, j * LANES:(j + 1) * LANES]

        nv = nvalid_ref[i]

        def body(c, carry):
            for u in range(DMA_UNROLL):
                r = c * DMA_UNROLL + u

                @pl.when(r < nv)
                def _():
                    scatter_row(islot, slot, r).start()
            return carry
        lax.fori_loop(0, (nv + DMA_UNROLL - 1) >> DMA_UNROLL_BITS, body, 0)

        @pl.when(i == nreal - 1)
        def _():
            @pl.when(i >= 1)
            def _():
                wait_scatter(i - 1)
            wait_scatter(i)


def _moe(h_rows, top_e, w):
    n = top_e.shape[0]
    flat_e = top_e.reshape(-1)
    n_flat = n * TOP_K
    n_blocks = -(-n_flat // MOE_ROWS) + N_EXPERTS
    order = jnp.argsort(flat_e).astype(jnp.int32)
    counts = jnp.sum((flat_e[:, None] == jnp.arange(N_EXPERTS, dtype=jnp.int32)[None, :]).astype(jnp.int32), axis=0)
    starts = jnp.cumsum(counts) - counts
    padded = (counts + MOE_ROWS - 1) // MOE_ROWS * MOE_ROWS
    pad_ends = jnp.cumsum(padded)
    pad_starts = pad_ends - padded
    blk_start = jnp.arange(n_blocks, dtype=jnp.int32) * MOE_ROWS
    blk_e = jnp.minimum(jnp.sum((pad_ends[None, :] <= blk_start[:, None]).astype(jnp.int32), axis=1),
                        N_EXPERTS - 1).astype(jnp.int32)
    nreal = (pad_ends[-1] // MOE_ROWS).astype(jnp.int32).reshape(1)
    r = jnp.arange(n_blocks * MOE_ROWS, dtype=jnp.int32)
    e_r = jnp.repeat(blk_e, MOE_ROWS)
    rank = r - pad_starts[e_r]
    valid = (rank < counts[e_r]) & (r < pad_ends[-1])
    src = jnp.clip(starts[e_r] + rank, 0, n_flat - 1)
    rows = jnp.where(valid, order[src], 0).astype(jnp.int32)
    rows = jnp.pad(rows.reshape(n_blocks, MOE_ROWS // LANES, LANES),
                   ((0, 0), (0, SUBLANES - MOE_ROWS // LANES), (0, 0)))
    nvalid = jnp.sum(valid.reshape(n_blocks, MOE_ROWS).astype(jnp.int32), axis=1)
    out_rows = n_flat * SUBLANES
    wspec = pl.BlockSpec((1, D_MODEL, D_MODEL), lambda i, be, nv, nr: (be[i], 0, 0))
    bspec = pl.BlockSpec((1, 1, D_MODEL), lambda i, be, nv, nr: (be[i], 0, 0))
    any_spec = pl.BlockSpec(memory_space=pl.ANY)
    grid_spec = pltpu.PrefetchScalarGridSpec(
        num_scalar_prefetch=3,
        grid=(n_blocks,),
        in_specs=[any_spec, any_spec, wspec, bspec, wspec, bspec, wspec, bspec],
        out_specs=any_spec,
        scratch_shapes=[pltpu.SMEM((IDX_SLOTS, SUBLANES, LANES), jnp.int32),
                        pltpu.VMEM((2, MOE_ROWS * SUBLANES, LANES), F32),
                        pltpu.VMEM((2, MOE_ROWS * SUBLANES, LANES), F32),
                        pltpu.VMEM((D_MODEL, D_MODEL), BF), pltpu.VMEM((D_MODEL, D_MODEL), BF),
                        pltpu.VMEM((D_MODEL, D_MODEL), BF),
                        pltpu.SemaphoreType.DMA((IDX_SLOTS,)), pltpu.SemaphoreType.DMA((2,)),
                        pltpu.SemaphoreType.DMA((2,))])
    return pl.pallas_call(
        _moe_kernel,
        grid_spec=grid_spec,
        out_shape=jax.ShapeDtypeStruct((out_rows, LANES), F32),
        compiler_params=_params(56, 1, disable_bounds_checks=True),
    )(blk_e, nvalid, nreal, rows, h_rows, w["wgate"], w["bgate"], w["wup"], w["bup"], w["wdown"], w["bdown"])


def _combine_kernel(h_ref, y_ref, gate_ref, g2_ref, b2_ref, o_ref, *, tm):
    g = gate_ref[...]
    cols = []
    for j in range(D_MODEL // LANES):
        f = None
        for kk in range(TOP_K):
            term = g[:, kk:kk + 1] * y_ref[pl.ds(kk * SUBLANES + j, tm, stride=TOP_K * SUBLANES), :]
            f = term if f is None else f + term
        cols.append(DEEPNORM_ALPHA * h_ref[pl.ds(j, tm, stride=SUBLANES), :] + f)
    o_ref[...] = _layernorm(jnp.concatenate(cols, axis=1), g2_ref[...], b2_ref[...])


def _combine(h_rows, y_rows, gate, g2, b2):
    n = gate.shape[0]
    tm = min(256, n)
    return pl.pallas_call(
        functools.partial(_combine_kernel, tm=tm),
        grid=(n // tm,),
        in_specs=[pl.BlockSpec((tm * SUBLANES, LANES), lambda i: (i, 0)),
                  pl.BlockSpec((tm * TOP_K * SUBLANES, LANES), lambda i: (i, 0)),
                  pl.BlockSpec((tm, TOP_K), lambda i: (i, 0)), _full((1, D_MODEL)), _full((1, D_MODEL))],
        out_specs=pl.BlockSpec((tm, D_MODEL), lambda i: (i, 0)),
        out_shape=jax.ShapeDtypeStruct((n, D_MODEL), F32),
        compiler_params=_params(40, 1),
    )(h_rows, y_rows, gate, g2, b2)


def _row(v):
    return v.reshape(1, -1).astype(F32)


def _prep_weights(w_in, b_in, b_forget, conv_w, conv_b, conv_ln_g, conv_ln_b, w_conv_out, b_conv_out, w_fox_out,
                  w_mem_kv, w_mem_out, w_out, b_out, ln1_g, ln1_b, w_router, b_router, w_gate, b_gate, w_up, b_up,
                  w_down, b_down, ln2_g, ln2_b):
    o_q = 2 * CONV_CH
    o_f = o_q + 3 * FOX_WIDTH
    o_qm = o_f + FOX_HEADS
    o_g = o_qm + MEM_WIDTH
    pad_f = LANES - FOX_HEADS
    proj = dict(
        wglu=w_in[:, :o_q].astype(BF), bglu=_row(b_in[:o_q]),
        wqkv=w_in[:, o_q:o_f].astype(BF), bqkv=_row(b_in[o_q:o_f]),
        wf=jnp.pad(w_in[:, o_f:o_qm], ((0, 0), (0, pad_f))).astype(BF),
        bf=_row(jnp.pad(b_in[o_f:o_qm], (0, pad_f))), bfg=_row(jnp.pad(b_forget, (0, pad_f))),
        wqm=w_in[:, o_qm:o_g].astype(BF), bqm=_row(b_in[o_qm:o_g]))
    conv = dict(w=jnp.pad(conv_w, ((0, HIST_ROWS - CONV_WIDTH), (0, 0))).astype(F32), cb=_row(conv_b),
                g=_row(conv_ln_g), b=_row(conv_ln_b))
    pad_r = LANES - N_EXPERTS
    wr = jnp.pad(w_router, ((0, 0), (0, pad_r)))
    wr_hi = wr.astype(BF)
    merge = dict(
        wg=w_in[:, o_g:].astype(BF), bg=_row(b_in[o_g:]), wc=w_conv_out.astype(BF), bc=_row(b_conv_out),
        wfo=w_fox_out.astype(BF), wmo=w_mem_out.astype(BF), wo=w_out.astype(BF), bo=_row(b_out),
        g1=_row(ln1_g), b1=_row(ln1_b), wrh=wr_hi, wrl=(wr - wr_hi.astype(F32)).astype(BF),
        br=_row(jnp.pad(b_router, (0, pad_r))))
    moe = dict(wgate=w_gate, bgate=b_gate.reshape(N_EXPERTS, 1, D_MODEL), wup=w_up,
               bup=b_up.reshape(N_EXPERTS, 1, D_MODEL), wdown=w_down, bdown=b_down.reshape(N_EXPERTS, 1, D_MODEL))
    return proj, conv, merge, moe, w_mem_kv.astype(BF), _row(ln2_g), _row(ln2_b)


def _channel(x2d, yc, yf, ym, merge_w, moe_w, g2, b2):
    h_rows, top_e, gate = _merge(x2d, yc, yf, ym, merge_w)
    y_rows = _moe(h_rows, top_e, moe_w)
    return _combine(h_rows, y_rows, gate, g2, b2)


def kernel(x_prompt, x_sample, mem_prompt, cache_k, cache_v, cache_logf, page_table, cache_mem_k, cache_mem_v, state_conv, w_in, b_in, b_forget, conv_w, conv_b, conv_ln_g, conv_ln_b, w_conv_out, b_conv_out, w_fox_out, w_mem_kv, w_mem_out, w_out, b_out, ln1_g, ln1_b, w_router, b_router, w_gate, b_gate, w_up, b_up, w_down, b_down, ln2_g, ln2_b):
    proj_w, conv_w_, merge_w, moe_w, wkv, g2, b2 = _prep_weights(
        w_in, b_in, b_forget, conv_w, conv_b, conv_ln_g, conv_ln_b, w_conv_out, b_conv_out, w_fox_out, w_mem_kv,
        w_mem_out, w_out, b_out, ln1_g, ln1_b, w_router, b_router, w_gate, b_gate, w_up, b_up, w_down, b_down,
        ln2_g, ln2_b)
    b, l, d = x_prompt.shape
    db, t, _ = x_sample.shape
    hist_len = CONV_WIDTH - 1

    xp = x_prompt.reshape(b * l, d)
    u, q, k, v, kb, vb, logf, _, qm, aq, ak = _in_proj(xp, l, proj_w)
    u3 = u.reshape(b, l, CONV_CH)
    yc = _conv_prompt(u3, jnp.zeros((b, HIST_ROWS, CONV_CH), F32), conv_w_)
    yf = _fox_prompt(q.reshape(b, l, FOX_WIDTH), kb.reshape(b, l, FOX_WIDTH), vb.reshape(b, l, FOX_WIDTH),
                     aq.reshape(b, l, FOX_HEADS * LANES), ak.reshape(b, l, FOX_HEADS * LANES))
    mk, mv = _mem_kv(mem_prompt.reshape(b * MEM_TOKENS, d), wkv)
    mk3 = mk.reshape(b, MEM_TOKENS, MEM_WIDTH)
    mv3 = mv.reshape(b, MEM_TOKENS, MEM_WIDTH)
    ym = _mem_attend(qm.reshape(b, l, MEM_WIDTH), mk3, mv3, 1, min(512, l))
    y_prompt = _channel(xp, yc.reshape(b * l, CONV_CH), yf.reshape(b * l, FOX_WIDTH), ym.reshape(b * l, MEM_WIDTH),
                        merge_w, moe_w, g2, b2).reshape(b, l, d)

    xs = x_sample.reshape(db * t, d)
    us, qs, ks, vs, ksb, vsb, logfs, cums, qms, _, _ = _in_proj(xs, t, proj_w)
    us_ext = jnp.concatenate([state_conv.astype(F32), us.reshape(db, t, CONV_CH)], axis=1)
    ycs = _conv_sample(us_ext.transpose(1, 0, 2), conv_w_).transpose(1, 0, 2)
    n_phys = cache_logf.shape[0]
    suf, tot = _page_suffix(cache_logf.transpose(0, 2, 1).reshape(n_phys * FOX_HEADS, PAGE_SIZE))
    yfs = _fox_sample(qs.reshape(db, t, FOX_WIDTH), ksb.reshape(db, t, FOX_WIDTH), vsb.reshape(db, t, FOX_WIDTH),
                      cums.reshape(db, t, FOX_HEADS), cache_k.transpose(0, 2, 3, 1), cache_v.transpose(0, 2, 3, 1),
                      suf.reshape(n_phys, FOX_HEADS, PAGE_SIZE), tot.reshape(n_phys, FOX_HEADS, PAGE_SIZE),
                      page_table)
    t_pad = 2 * SUBLANES
    qms3 = jnp.pad(qms.reshape(db, t, MEM_WIDTH), ((0, 0), (0, t_pad - t), (0, 0)))
    yms = _mem_attend(qms3, cache_mem_k.reshape(db, MEM_TOKENS, MEM_WIDTH),
                      cache_mem_v.reshape(db, MEM_TOKENS, MEM_WIDTH), 8, t_pad)[:, :t]
    y_sample = _channel(xs, ycs.reshape(db * t, CONV_CH), yfs.reshape(db * t, FOX_WIDTH),
                        yms.reshape(db * t, MEM_WIDTH), merge_w, moe_w, g2, b2).reshape(db, t, d)

    heads = lambda a, n, s: a.reshape(n, s, FOX_HEADS, FOX_HEAD_DIM)
    return (y_prompt, y_sample,
            heads(k, b, l), heads(v, b, l), logf.reshape(b, l, FOX_HEADS),
            mk.reshape(b, MEM_TOKENS, MEM_HEADS, MEM_HEAD_DIM), mv.reshape(b, MEM_TOKENS, MEM_HEADS, MEM_HEAD_DIM),
            u3[:, l - hist_len:, :],
            heads(ks, db, t), heads(vs, db, t), logfs.reshape(db, t, FOX_HEADS),
            us_ext[:, t:, :])
```

```python
import functools

import numpy as np
import jax
import jax.numpy as jnp
from jax import lax
from jax.experimental import pallas as pl
from jax.experimental.pallas import tpu as pltpu

D_MODEL = 1024
CONV_CH = 512
CONV_WIDTH = 31
FOX_HEADS = 8
FOX_HEAD_DIM = 64
FOX_WIDTH = FOX_HEADS * FOX_HEAD_DIM
MEM_HEADS = 4
MEM_HEAD_DIM = 128
MEM_WIDTH = MEM_HEADS * MEM_HEAD_DIM
MEM_TOKENS = 256
N_EXPERTS = 32
TOP_K = 4
PAGE_SIZE = 128
SWIGLU_LIMIT = 7.0
SWIGLU_ALPHA = 1.702
LN_EPS = 1e-5
DEEPNORM_ALPHA = 2.0 ** 0.25

LANES = 128
SUBLANES = 8
HIST_ROWS = 32
MIB = 1024 * 1024

BF = jnp.bfloat16
F32 = jnp.float32
NEG_INF = float("-inf")


def _dot(a, b):
    return jnp.dot(a, b, preferred_element_type=F32)


def _dot_nt(a, b):
    return lax.dot_general(a, b, (((1,), (1,)), ((), ())), preferred_element_type=F32)


def _params(vmem_mib, n_axes, **kw):
    return pltpu.CompilerParams(dimension_semantics=("arbitrary",) * n_axes,
                                vmem_limit_bytes=vmem_mib * MIB, **kw)


def _full(shape):
    nd = len(shape)
    return pl.BlockSpec(shape, lambda *_: (0,) * nd)


def _split3(x):
    hi = x.astype(BF)
    r1 = x - hi.astype(F32)
    mid = r1.astype(BF)
    lo = (r1 - mid.astype(F32)).astype(BF)
    return hi, mid, lo


def _layernorm(x, g, b):
    mu = jnp.mean(x, axis=-1, keepdims=True)
    xc = x - mu
    var = jnp.mean(xc * xc, axis=-1, keepdims=True)
    return xc * lax.rsqrt(var + LN_EPS) * g + b


def _in_proj_kernel(x_ref, tri_ref, wglu_ref, wqkv_ref, wf_ref, wqm_ref, bglu_ref, bqkv_ref, bf_ref, bfg_ref,
                    bqm_ref, pq_ref, pk_ref, cq_ref, ck_ref, u_ref, q_ref, k_ref, v_ref, kb_ref, vb_ref, logf_ref,
                    cum_ref, qm_ref, aq_ref, ak_ref, carry_ref, *, tm, seq_len):
    i = pl.program_id(0)
    xb = x_ref[...].astype(BF)
    glu = _dot(xb, wglu_ref[...]) + bglu_ref[...]
    u_ref[...] = glu[:, :CONV_CH] * jax.nn.sigmoid(glu[:, CONV_CH:])
    qkv = _dot(xb, wqkv_ref[...]) + bqkv_ref[...]
    q_ref[...] = (qkv[:, :FOX_WIDTH] * (FOX_HEAD_DIM ** -0.5)).astype(BF)
    k = qkv[:, FOX_WIDTH:2 * FOX_WIDTH]
    v = qkv[:, 2 * FOX_WIDTH:]
    k_ref[...] = k
    v_ref[...] = v
    kb_ref[...] = k.astype(BF)
    vb_ref[...] = v.astype(BF)
    qm_ref[...] = (_dot(xb, wqm_ref[...]) + bqm_ref[...]).astype(BF)
    f = (_dot(xb, wf_ref[...]) + bf_ref[...]) + bfg_ref[...]
    lf = jnp.minimum(f, 0.0) - jnp.log1p(jnp.exp(-jnp.abs(f)))
    logf_ref[...] = lf[:, :FOX_HEADS]
    hi, mid, lo = _split3(lf)
    tri = tri_ref[...]
    cum = _dot(tri, hi) + _dot(tri, mid) + _dot(tri, lo)
    if seq_len > tm:
        @pl.when(i % (seq_len // tm) == 0)
        def _():
            carry_ref[...] = jnp.zeros_like(carry_ref)
        cum = cum + carry_ref[...]
        carry_ref[...] = cum[tm - 1:tm, :]
    cum_ref[...] = cum[:, :FOX_HEADS]
    parts = _split3(cum)
    aq = cq_ref[...] + _dot(parts[0], pq_ref[0]) + _dot(parts[1], pq_ref[1]) + _dot(parts[2], pq_ref[2])
    ak = ck_ref[...] + _dot(parts[0], pk_ref[0]) + _dot(parts[1], pk_ref[1]) + _dot(parts[2], pk_ref[2])
    aq_ref[...] = aq.astype(BF)
    ak_ref[...] = ak.astype(BF)


def _aug_lane(h):
    return h * LANES + (FOX_HEAD_DIM if h % 2 == 0 else 0)


def _aug_constants():
    pq = np.zeros((3, LANES, FOX_HEADS * LANES), np.float32)
    pk = np.zeros((3, LANES, FOX_HEADS * LANES), np.float32)
    cq = np.zeros((1, FOX_HEADS * LANES), np.float32)
    ck = np.zeros((1, FOX_HEADS * LANES), np.float32)
    for h in range(FOX_HEADS):
        base = _aug_lane(h)
        for j in range(3):
            pq[j, h, base + j] = 1.0
            pk[j, h, base + 3 + j] = -1.0
            cq[0, base + 3 + j] = 1.0
            ck[0, base + j] = 1.0
    return jnp.asarray(pq, BF), jnp.asarray(pk, BF), jnp.asarray(cq), jnp.asarray(ck)


def _in_proj(x2d, seq_len, w):
    n = x2d.shape[0]
    tm = min(512, n)
    lc = min(seq_len, tm)
    r = np.arange(tm)
    tri = jnp.asarray(((r[None, :] <= r[:, None]) & (r[None, :] // lc == r[:, None] // lc)).astype(np.float32), BF)
    tok = lambda width: pl.BlockSpec((tm, width), lambda i: (i, 0))
    out_shape = (jax.ShapeDtypeStruct((n, CONV_CH), F32), jax.ShapeDtypeStruct((n, FOX_WIDTH), BF),
                 jax.ShapeDtypeStruct((n, FOX_WIDTH), F32), jax.ShapeDtypeStruct((n, FOX_WIDTH), F32),
                 jax.ShapeDtypeStruct((n, FOX_WIDTH), BF), jax.ShapeDtypeStruct((n, FOX_WIDTH), BF),
                 jax.ShapeDtypeStruct((n, FOX_HEADS), F32), jax.ShapeDtypeStruct((n, FOX_HEADS), F32),
                 jax.ShapeDtypeStruct((n, MEM_WIDTH), BF),
                 jax.ShapeDtypeStruct((n, FOX_HEADS * LANES), BF), jax.ShapeDtypeStruct((n, FOX_HEADS * LANES), BF))
    ins = (x2d, tri, w["wglu"], w["wqkv"], w["wf"], w["wqm"], w["bglu"], w["bqkv"], w["bf"], w["bfg"], w["bqm"],
           *_aug_constants())
    return pl.pallas_call(
        functools.partial(_in_proj_kernel, tm=tm, seq_len=seq_len),
        grid=(n // tm,),
        in_specs=[tok(D_MODEL)] + [_full(a.shape) for a in ins[1:]],
        out_specs=(tok(CONV_CH), tok(FOX_WIDTH), tok(FOX_WIDTH), tok(FOX_WIDTH), tok(FOX_WIDTH), tok(FOX_WIDTH),
                   tok(FOX_HEADS), tok(FOX_HEADS), tok(MEM_WIDTH), tok(FOX_HEADS * LANES), tok(FOX_HEADS * LANES)),
        out_shape=out_shape,
        scratch_shapes=[pltpu.VMEM((1, LANES), F32)],
        compiler_params=_params(56, 1),
    )(*ins)


def _conv_post(y, cb_ref, g_ref, b_ref):
    y = _layernorm(y + cb_ref[...], g_ref[...], b_ref[...])
    return (y * jax.nn.sigmoid(y)).astype(BF)


def _conv_prompt_kernel(u_ref, prev_ref, hist_ref, w_ref, cb_ref, g_ref, b_ref, o_ref, win_ref, y_ref, *, tm):
    i = pl.program_id(1)
    win_ref[0:HIST_ROWS, :] = jnp.where(i == 0, hist_ref[0], prev_ref[0])
    win_ref[HIST_ROWS:, :] = u_ref[0]
    first = HIST_ROWS - (CONV_WIDTH - 1)
    for c in range(CONV_CH // LANES):
        cs = slice(c * LANES, (c + 1) * LANES)
        acc = jnp.zeros((tm, LANES), F32)
        for j in range(CONV_WIDTH):
            acc = acc + w_ref[j:j + 1, cs] * win_ref[first + j:first + j + tm, cs]
        y_ref[:, cs] = acc
    o_ref[0] = _conv_post(y_ref[...], cb_ref, g_ref, b_ref)


def _conv_prompt(u3, hist, cw):
    b, l, _ = u3.shape
    tm = 256
    per = tm // HIST_ROWS
    vec = _full((1, CONV_CH))
    return pl.pallas_call(
        functools.partial(_conv_prompt_kernel, tm=tm),
        grid=(b, l // tm),
        in_specs=[pl.BlockSpec((1, tm, CONV_CH), lambda bi, i: (bi, i, 0)),
                  pl.BlockSpec((1, HIST_ROWS, CONV_CH), lambda bi, i: (bi, jnp.maximum(i * per - 1, 0), 0)),
                  pl.BlockSpec((1, HIST_ROWS, CONV_CH), lambda bi, i: (bi, 0, 0)),
                  _full((HIST_ROWS, CONV_CH)), vec, vec, vec],
        out_specs=pl.BlockSpec((1, tm, CONV_CH), lambda bi, i: (bi, i, 0)),
        out_shape=jax.ShapeDtypeStruct((b, l, CONV_CH), BF),
        scratch_shapes=[pltpu.VMEM((tm + HIST_ROWS, CONV_CH), F32), pltpu.VMEM((tm, CONV_CH), F32)],
        compiler_params=_params(32, 2),
    )(u3, u3, hist, cw["w"], cw["cb"], cw["g"], cw["b"])


def _conv_sample_kernel(x_ref, w_ref, cb_ref, g_ref, b_ref, o_ref, *, steps):
    for t in range(steps):
        acc = jnp.zeros(x_ref.shape[1:], F32)
        for j in range(CONV_WIDTH):
            acc = acc + w_ref[j:j + 1, :] * x_ref[t + j]
        o_ref[t] = _conv_post(acc, cb_ref, g_ref, b_ref)


def _conv_sample(u_ext_t, cw):
    rows, b, _ = u_ext_t.shape
    steps = rows - (CONV_WIDTH - 1)
    bb = min(64, b)
    vec = _full((1, CONV_CH))
    return pl.pallas_call(
        functools.partial(_conv_sample_kernel, steps=steps),
        grid=(b // bb,),
        in_specs=[pl.BlockSpec((rows, bb, CONV_CH), lambda i: (0, i, 0)), _full((HIST_ROWS, CONV_CH)), vec, vec, vec],
        out_specs=pl.BlockSpec((steps, bb, CONV_CH), lambda i: (0, i, 0)),
        out_shape=jax.ShapeDtypeStruct((steps, b, CONV_CH), BF),
        compiler_params=_params(32, 1),
    )(u_ext_t, cw["w"], cw["cb"], cw["g"], cw["b"])


def _fox_prompt_kernel(qi_ref, kj_ref, q_ref, k_ref, v_ref, aq_ref, ak_ref, o_ref, qa_ref, ka_ref, va_ref, m_ref,
                       acc_ref, *, tile, sub):
    t = pl.program_id(2)
    qi = qi_ref[t]
    kj = kj_ref[t]
    nsub = tile // sub
    lane = lax.broadcasted_iota(jnp.int32, (1, LANES), 1)
    in_head = [(lane >= hh * FOX_HEAD_DIM) & (lane < (hh + 1) * FOX_HEAD_DIM) for hh in range(2)]
    sum_lane = [_aug_lane(hh) % LANES for hh in range(2)]

    for hh in range(2):
        ka_ref[hh] = jnp.where(in_head[hh], k_ref[0], ak_ref[0, :, hh * LANES:(hh + 1) * LANES])
        va_ref[hh] = jnp.where(in_head[hh], v_ref[0], jnp.where(lane == sum_lane[hh], 1.0, 0.0).astype(BF))

    @pl.when(kj == 0)
    def _():
        m_ref[...] = jnp.full_like(m_ref, NEG_INF)
        acc_ref[...] = jnp.zeros_like(acc_ref)
        for hh in range(2):
            qa_ref[hh] = jnp.where(in_head[hh], q_ref[0], aq_ref[0, :, hh * LANES:(hh + 1) * LANES])

    def attend(hh, i2, j2, masked):
        rows = slice(i2 * sub, (i2 + 1) * sub)
        cols = slice(j2 * sub, (j2 + 1) * sub)
        s = _dot_nt(qa_ref[hh, rows, :], ka_ref[hh, cols, :])
        if masked:
            r_id = lax.broadcasted_iota(jnp.int32, (sub, sub), 0)
            c_id = lax.broadcasted_iota(jnp.int32, (sub, sub), 1)
            s = jnp.where(c_id <= r_id, s, NEG_INF)
        m_prev = m_ref[hh, rows, :]
        m_next = jnp.maximum(m_prev, jnp.max(s, axis=1, keepdims=True))
        alpha = jnp.exp(m_prev - m_next)
        p = jnp.exp(s - jnp.concatenate([m_next] * (sub // LANES), axis=1))
        acc_ref[hh, rows, :] = alpha * acc_ref[hh, rows, :] + _dot(p.astype(BF), va_ref[hh, cols, :])
        m_ref[hh, rows, :] = m_next

    @pl.when(kj < qi)
    def _():
        for hh in range(2):
            for i2 in range(nsub):
                for j2 in range(nsub):
                    attend(hh, i2, j2, False)

    @pl.when(kj == qi)
    def _():
        for hh in range(2):
            for i2 in range(nsub):
                for j2 in range(i2 + 1):
                    attend(hh, i2, j2, j2 == i2)
        outs = []
        for hh in range(2):
            acc = acc_ref[hh]
            outs.append(acc / acc[:, sum_lane[hh]:sum_lane[hh] + 1])
        o_ref[0] = jnp.where(in_head[0], outs[0], outs[1]).astype(BF)


def _fox_prompt(q3, k3, v3, aq3, ak3):
    b, l, _ = q3.shape
    tile = min(1024, l)
    sub = min(512, tile)
    nq = l // tile
    pairs = FOX_HEADS // 2
    qi = np.concatenate([np.full(i + 1, i) for i in range(nq)]).astype(np.int32)
    kj = np.concatenate([np.arange(i + 1) for i in range(nq)]).astype(np.int32)
    grid_spec = pltpu.PrefetchScalarGridSpec(
        num_scalar_prefetch=2,
        grid=(b, pairs, len(qi)),
        in_specs=[pl.BlockSpec((1, tile, LANES), lambda bi, p, t, qi_r, kj_r: (bi, qi_r[t], p)),
                  pl.BlockSpec((1, tile, LANES), lambda bi, p, t, qi_r, kj_r: (bi, kj_r[t], p)),
                  pl.BlockSpec((1, tile, LANES), lambda bi, p, t, qi_r, kj_r: (bi, kj_r[t], p)),
                  pl.BlockSpec((1, tile, 2 * LANES), lambda bi, p, t, qi_r, kj_r: (bi, qi_r[t], p)),
                  pl.BlockSpec((1, tile, 2 * LANES), lambda bi, p, t, qi_r, kj_r: (bi, kj_r[t], p))],
        out_specs=pl.BlockSpec((1, tile, LANES), lambda bi, p, t, qi_r, kj_r: (bi, qi_r[t], p)),
        scratch_shapes=[pltpu.VMEM((2, tile, LANES), BF)] * 3 + [pltpu.VMEM((2, tile, LANES), F32)] * 2)
    return pl.pallas_call(
        functools.partial(_fox_prompt_kernel, tile=tile, sub=sub),
        grid_spec=grid_spec,
        out_shape=jax.ShapeDtypeStruct((b, l, FOX_WIDTH), BF),
        compiler_params=_params(40, 3),
    )(jnp.asarray(qi), jnp.asarray(kj), q3, k3, v3, aq3, ak3)


def _page_suffix_kernel(x_ref, upper_ref, ones_ref, suf_ref, tot_ref):
    hi, mid, lo = _split3(x_ref[...])
    up = upper_ref[...]
    on = ones_ref[...]
    suf_ref[...] = _dot(hi, up) + _dot(mid, up) + _dot(lo, up)
    tot_ref[...] = _dot(hi, on) + _dot(mid, on) + _dot(lo, on)


def _page_suffix(logf_t):
    rows = logf_t.shape[0]
    tr = min(2048, rows)
    kk = np.arange(PAGE_SIZE)
    upper = jnp.asarray((kk[:, None] > kk[None, :]).astype(np.float32), BF)
    ones = jnp.ones((PAGE_SIZE, PAGE_SIZE), BF)
    spec = pl.BlockSpec((tr, PAGE_SIZE), lambda i: (i, 0))
    return pl.pallas_call(
        _page_suffix_kernel,
        grid=(rows // tr,),
        in_specs=[spec, _full((PAGE_SIZE, PAGE_SIZE)), _full((PAGE_SIZE, PAGE_SIZE))],
        out_specs=(spec, spec),
        out_shape=(jax.ShapeDtypeStruct((rows, PAGE_SIZE), F32),) * 2,
        compiler_params=_params(32, 1),
    )(logf_t, upper, ones)


def _fox_sample_kernel(pt_ref, qrep_ref, cq_ref, ct_ref, kn_ref, vn_ref, *rest, pages, steps):
    k_refs = rest[:pages]
    v_refs = rest[pages:2 * pages]
    suf_refs = rest[2 * pages:3 * pages]
    tot_refs = rest[3 * pages:4 * pages]
    o_ref, m_ref, l_ref, acc_ref, carry_ref = rest[4 * pages:]
    j = pl.program_id(1)
    rows = steps * FOX_HEADS
    row_id = lax.broadcasted_iota(jnp.int32, (rows, FOX_WIDTH), 0)
    col_id = lax.broadcasted_iota(jnp.int32, (rows, FOX_WIDTH), 1)
    head_mask = (col_id // FOX_HEAD_DIM) == (row_id % FOX_HEADS)
    qrep = qrep_ref[0]
    qe = jnp.where(head_mask, qrep, jnp.zeros_like(qrep))
    cq = cq_ref[0]

    def update(s, pv_fn):
        m_prev = m_ref[...]
        m_next = jnp.maximum(m_prev, jnp.max(s, axis=1, keepdims=True))
        alpha = jnp.exp(m_prev - m_next)
        p = jnp.exp(s - jnp.concatenate([m_next] * (s.shape[1] // LANES), axis=1))
        l_ref[...] = alpha * l_ref[...] + jnp.sum(p, axis=1, keepdims=True)
        acc_ref[...] = jnp.concatenate([alpha] * (FOX_WIDTH // LANES), axis=1) * acc_ref[...] + pv_fn(p.astype(BF))
        m_ref[...] = m_next

    @pl.when(j == 0)
    def _():
        m_ref[...] = jnp.full_like(m_ref, NEG_INF)
        l_ref[...] = jnp.zeros_like(l_ref)
        acc_ref[...] = jnp.zeros_like(acc_ref)
        carry_ref[...] = jnp.zeros_like(carry_ref)
        s = _dot_nt(qe, kn_ref[0]) + (cq - jnp.concatenate([ct_ref[0]] * steps, axis=0))
        r_id = lax.broadcasted_iota(jnp.int32, (rows, LANES), 0)
        c_id = lax.broadcasted_iota(jnp.int32, (rows, LANES), 1)
        s = jnp.where(c_id <= r_id // FOX_HEADS, s, NEG_INF)
        update(s, lambda p: _dot(p, vn_ref[0]))

    carry = carry_ref[...]
    scores = [None] * pages
    for r in reversed(range(pages)):
        bias = cq + carry + jnp.concatenate([suf_refs[r][0]] * steps, axis=0)
        scores[r] = _dot(qe, k_refs[r][0].reshape(FOX_WIDTH, PAGE_SIZE).astype(BF)) + bias
        carry = carry + jnp.concatenate([tot_refs[r][0]] * steps, axis=0)
    carry_ref[...] = carry

    def pv_pages(p):
        out = None
        for r in range(pages):
            term = _dot_nt(p[:, r * PAGE_SIZE:(r + 1) * PAGE_SIZE],
                           v_refs[r][0].reshape(FOX_WIDTH, PAGE_SIZE).astype(BF))
            out = term if out is None else out + term
        return out

    update(jnp.concatenate(scores, axis=1), pv_pages)

    @pl.when(j == pl.num_programs(1) - 1)
    def _():
        o = jnp.where(head_mask, acc_ref[...] / jnp.concatenate([l_ref[...]] * (FOX_WIDTH // LANES), axis=1), 0.0)
        o_ref[0] = jnp.sum(o.reshape(steps, FOX_HEADS, FOX_WIDTH), axis=1).astype(BF)


def _fox_sample(q3, kn3, vn3, cum3, cache_kt, cache_vt, suf, tot, page_table):
    db, steps, _ = q3.shape
    n_pages = page_table.shape[1]
    pages = min(16, n_pages)
    rows = steps * FOX_HEADS
    qrep = jnp.repeat(q3, FOX_HEADS, axis=1)
    cq = jnp.broadcast_to(cum3.reshape(db, rows, 1), (db, rows, LANES))
    ct = jnp.pad(cum3.transpose(0, 2, 1), ((0, 0), (0, 0), (0, LANES - steps)))
    kn = jnp.pad(kn3, ((0, 0), (0, LANES - steps), (0, 0)))
    vn = jnp.pad(vn3, ((0, 0), (0, LANES - steps), (0, 0)))
    n_steps = n_pages // pages

    def page_map(r, nd):
        return lambda bi, j, pt: (pt[bi, (n_steps - 1 - j) * pages + r],) + (0,) * (nd - 1)

    seq = lambda shape: pl.BlockSpec((1,) + shape, lambda bi, j, pt: (bi, 0, 0))
    in_specs = [seq((rows, FOX_WIDTH)), seq((rows, LANES)), seq((FOX_HEADS, LANES)),
                seq((LANES, FOX_WIDTH)), seq((LANES, FOX_WIDTH))]
    in_specs += [pl.BlockSpec((1, FOX_HEADS, FOX_HEAD_DIM, PAGE_SIZE), page_map(r, 4)) for r in range(pages)] * 2
    in_specs += [pl.BlockSpec((1, FOX_HEADS, PAGE_SIZE), page_map(r, 3)) for r in range(pages)] * 2
    grid_spec = pltpu.PrefetchScalarGridSpec(
        num_scalar_prefetch=1,
        grid=(db, n_steps),
        in_specs=in_specs,
        out_specs=pl.BlockSpec((1, steps, FOX_WIDTH), lambda bi, j, pt: (bi, 0, 0)),
        scratch_shapes=[pltpu.VMEM((rows, LANES), F32), pltpu.VMEM((rows, LANES), F32),
                        pltpu.VMEM((rows, FOX_WIDTH), F32), pltpu.VMEM((rows, LANES), F32)])
    return pl.pallas_call(
        functools.partial(_fox_sample_kernel, pages=pages, steps=steps),
        grid_spec=grid_spec,
        out_shape=jax.ShapeDtypeStruct((db, steps, FOX_WIDTH), BF),
        compiler_params=_params(40, 2),
    )(page_table, qrep, cq, ct, kn, vn, *([cache_kt] * pages), *([cache_vt] * pages), *([suf] * pages),
      *([tot] * pages))


def _mem_kv_kernel(m_ref, w_ref, k_ref, v_ref):
    kv = _dot(m_ref[...].astype(BF), w_ref[...])
    k_ref[...] = kv[:, :MEM_WIDTH]
    v_ref[...] = kv[:, MEM_WIDTH:]


def _mem_kv(mem2d, w_bf):
    n = mem2d.shape[0]
    tm = min(256, n)
    out = pl.BlockSpec((tm, MEM_WIDTH), lambda i: (i, 0))
    return pl.pallas_call(
        _mem_kv_kernel,
        grid=(n // tm,),
        in_specs=[pl.BlockSpec((tm, D_MODEL), lambda i: (i, 0)), _full(w_bf.shape)],
        out_specs=(out, out),
        out_shape=(jax.ShapeDtypeStruct((n, MEM_WIDTH), F32),) * 2,
        compiler_params=_params(32, 1),
    )(mem2d, w_bf)


def _mem_attend_kernel(q_ref, k_ref, v_ref, o_ref, *, bb, rows_by_head):
    for b in range(bb):
        for h in range(MEM_HEADS):
            hs = slice(h * MEM_HEAD_DIM, (h + 1) * MEM_HEAD_DIM)
            if rows_by_head:
                k = k_ref[b, pl.ds(h, MEM_TOKENS, stride=MEM_HEADS), :]
                v = v_ref[b, pl.ds(h, MEM_TOKENS, stride=MEM_HEADS), :]
            else:
                k = k_ref[b, :, hs]
                v = v_ref[b, :, hs]
            s = _dot_nt(q_ref[b, :, hs], k.astype(BF)) * (MEM_HEAD_DIM ** -0.5)
            p = jnp.exp(s - jnp.max(s, axis=1, keepdims=True))
            den = jnp.sum(p, axis=1, keepdims=True)
            o_ref[b, :, hs] = (_dot(p.astype(BF), v.astype(BF)) / den).astype(BF)


def _mem_attend(qm3, mk3, mv3, bb, tq):
    b, l, _ = qm3.shape
    rows_by_head = mk3.shape[2] == MEM_HEAD_DIM
    kv = pl.BlockSpec((bb,) + mk3.shape[1:], lambda bi, i: (bi, 0, 0))
    qs = pl.BlockSpec((bb, tq, MEM_WIDTH), lambda bi, i: (bi, i, 0))
    return pl.pallas_call(
        functools.partial(_mem_attend_kernel, bb=bb, rows_by_head=rows_by_head),
        grid=(b // bb, l // tq),
        in_specs=[qs, kv, kv],
        out_specs=qs,
        out_shape=jax.ShapeDtypeStruct((b, l, MEM_WIDTH), BF),
        compiler_params=_params(40, 2),
    )(qm3, mk3, mv3)


def _merge_kernel(x_ref, yc_ref, yf_ref, ym_ref, wg_ref, bg_ref, wc_ref, bc_ref, wfo_ref, wmo_ref, wo_ref, bo_ref,
                  g1_ref, b1_ref, wrh_ref, wrl_ref, br_ref, h_ref, e_ref, gate_ref, *, tm):
    x = x_ref[...]
    xb = x.astype(BF)

    def gate(c):
        cs = slice(c * D_MODEL, (c + 1) * D_MODEL)
        return jax.nn.sigmoid(_dot(xb, wg_ref[:, cs]) + bg_ref[:, cs])

    mix = gate(0) * (_dot(yc_ref[...], wc_ref[...]) + bc_ref[...])
    mix = mix + gate(1) * _dot(yf_ref[...], wfo_ref[...])
    mix = mix + gate(2) * _dot(ym_ref[...], wmo_ref[...])
    pre = DEEPNORM_ALPHA * x + (_dot(mix.astype(BF), wo_ref[...]) + bo_ref[...])
    h = _layernorm(pre, g1_ref[...], b1_ref[...])
    for j in range(D_MODEL // LANES):
        h_ref[pl.ds(j, tm, stride=SUBLANES), :] = h[:, j * LANES:(j + 1) * LANES]

    h_hi = h.astype(BF)
    h_lo = (h - h_hi.astype(F32)).astype(BF)
    logits = _dot(h_hi, wrh_ref[...]) + _dot(h_hi, wrl_ref[...]) + _dot(h_lo, wrh_ref[...]) + br_ref[...]
    lane = lax.broadcasted_iota(jnp.int32, (tm, LANES), 1)
    lane_f = lane.astype(F32)
    work = jnp.where(lane < N_EXPERTS, logits, NEG_INF)
    vals, idxs = [], []
    for _ in range(TOP_K):
        mx = jnp.max(work, axis=1, keepdims=True)
        idx = jnp.min(jnp.where(work == mx, lane_f, float(LANES)), axis=1, keepdims=True)
        vals.append(mx)
        idxs.append(idx)
        work = jnp.where(lane_f == idx, NEG_INF, work)
    exps = [jnp.exp(v - vals[0]) for v in vals]
    den = exps[0] + exps[1] + exps[2] + exps[3]
    e_out = jnp.zeros((tm, LANES), F32)
    g_out = jnp.zeros((tm, LANES), F32)
    for kk in range(TOP_K):
        e_out = jnp.where(lane == kk, idxs[kk], e_out)
        g_out = jnp.where(lane == kk, exps[kk] / den, g_out)
    e_ref[...] = e_out[:, :TOP_K].astype(jnp.int32)
    gate_ref[...] = g_out[:, :TOP_K]


def _merge(x2d, yc, yf, ym, w):
    n = x2d.shape[0]
    tm = min(256, n)
    tok = lambda width: pl.BlockSpec((tm, width), lambda i: (i, 0))
    ws = (w["wg"], w["bg"], w["wc"], w["bc"], w["wfo"], w["wmo"], w["wo"], w["bo"], w["g1"], w["b1"],
          w["wrh"], w["wrl"], w["br"])
    return pl.pallas_call(
        functools.partial(_merge_kernel, tm=tm),
        grid=(n // tm,),
        in_specs=[tok(D_MODEL), tok(CONV_CH), tok(FOX_WIDTH), tok(MEM_WIDTH)] + [_full(a.shape) for a in ws],
        out_specs=(pl.BlockSpec((tm * SUBLANES, LANES), lambda i: (i, 0)), tok(TOP_K), tok(TOP_K)),
        out_shape=(jax.ShapeDtypeStruct((n * SUBLANES, LANES), F32), jax.ShapeDtypeStruct((n, TOP_K), jnp.int32),
                   jax.ShapeDtypeStruct((n, TOP_K), F32)),
        compiler_params=_params(56, 1),
    )(x2d, yc, yf, ym, *ws)


MOE_ROWS = 256
IDX_SLOTS = 4
IDX_STRIDE = 1024
DMA_UNROLL_BITS = 3
DMA_UNROLL = 1 << DMA_UNROLL_BITS
TOP_K_BITS = TOP_K.bit_length() - 1


def _moe_kernel(blk_e_ref, nvalid_ref, nreal_ref, rows_hbm, h_hbm, wg_ref, bg_ref, wu_ref, bu_ref, wd_ref, bd_ref,
                y_hbm, idx_ref, xbuf, ybuf, wgb, wub, wdb, idx_sem, in_sem, out_sem):
    i = pl.program_id(0)
    nreal = nreal_ref[0]
    tile = SUBLANES

    def idx_slot(blk):
        return blk & (IDX_SLOTS - 1)

    def idx_copy(blk):
        slot = idx_slot(blk)
        return pltpu.make_async_copy(rows_hbm.at[pl.ds(pl.multiple_of(blk * IDX_STRIDE, IDX_STRIDE), IDX_STRIDE)],
                                     idx_ref.at[pl.ds(pl.multiple_of(slot * IDX_STRIDE, IDX_STRIDE), IDX_STRIDE)],
                                     idx_sem.at[slot])

    def row_index(islot, r):
        return idx_ref[islot * IDX_STRIDE + r]

    def gather_row(islot, slot, r):
        tok = row_index(islot, r) >> TOP_K_BITS
        return pltpu.make_async_copy(h_hbm.at[pl.ds(pl.multiple_of(tok * tile, tile), tile)],
                                     xbuf.at[slot, pl.ds(pl.multiple_of(r * tile, tile), tile)],
                                     in_sem.at[slot])

    def scatter_row(islot, slot, r):
        dst = row_index(islot, r)
        return pltpu.make_async_copy(ybuf.at[slot, pl.ds(pl.multiple_of(r * tile, tile), tile)],
                                     y_hbm.at[pl.ds(pl.multiple_of(dst * tile, tile), tile)],
                                     out_sem.at[slot])

    def start_gather(blk):
        islot = idx_slot(blk)
        slot = blk & 1

        def body(c, carry):
            for u in range(DMA_UNROLL):
                gather_row(islot, slot, c * DMA_UNROLL + u).start()
            return carry
        lax.fori_loop(0, MOE_ROWS // DMA_UNROLL, body, 0)

    def wait_rows(sem, slot):
        pltpu.make_async_copy(xbuf.at[slot], ybuf.at[slot], sem.at[slot]).wait()

    def wait_scatter(blk):
        slot = blk & 1
        nv = nvalid_ref[blk]

        @pl.when(nv == MOE_ROWS)
        def _():
            wait_rows(out_sem, slot)

        @pl.when(nv < MOE_ROWS)
        def _():
            def body(r, c):
                pltpu.make_async_copy(ybuf.at[slot, pl.ds(0, tile)], y_hbm.at[pl.ds(0, tile)],
                                      out_sem.at[slot]).wait()
                return c
            lax.fori_loop(0, nv, body, 0)

    @pl.when((i == 0) & (nreal > 0))
    def _():
        idx_copy(0).start()
        idx_copy(0).wait()
        start_gather(0)

        @pl.when(nreal > 1)
        def _():
            idx_copy(1).start()

    @pl.when(i + 1 < nreal)
    def _():
        idx_copy(i + 1).wait()
        start_gather(i + 1)

        @pl.when(i + 2 < nreal)
        def _():
            idx_copy(i + 2).start()

    changed = (i == 0) | (blk_e_ref[i] != blk_e_ref[jnp.maximum(i - 1, 0)])

    @pl.when(changed & (i < nreal))
    def _():
        wgb[...] = wg_ref[0].astype(BF)
        wub[...] = wu_ref[0].astype(BF)
        wdb[...] = wd_ref[0].astype(BF)

    @pl.when(i < nreal)
    def _():
        slot = i & 1
        islot = idx_slot(i)
        wait_rows(in_sem, slot)

        @pl.when(i >= 2)
        def _():
            wait_scatter(i - 2)

        x = jnp.concatenate([xbuf[slot, pl.ds(j, MOE_ROWS, stride=SUBLANES), :] for j in range(D_MODEL // LANES)],
                            axis=1).astype(BF)

        a = jnp.minimum(_dot(x, wgb[...]) + bg_ref[0], SWIGLU_LIMIT)
        u = jnp.clip(_dot(x, wub[...]) + bu_ref[0], -SWIGLU_LIMIT, SWIGLU_LIMIT)
        hid = (u + 1.0) * a * jax.nn.sigmoid(SWIGLU_ALPHA * a)
        y = _dot(hid.astype(BF), wdb[...]) + bd_ref[0]
        for j in range(D_MODEL // LANES):
            ybuf[slot, pl.ds(j, MOE_ROWS, stride=SUBLANES), :] = y[:, j * LANES:(j + 1) * LANES]

        nv = nvalid_ref[i]

        def body(c, carry):
            for u in range(DMA_UNROLL):
                r = c * DMA_UNROLL + u

                @pl.when(r < nv)
                def _():
                    scatter_row(islot, slot, r).start()
            return carry
        lax.fori_loop(0, (nv + DMA_UNROLL - 1) >> DMA_UNROLL_BITS, body, 0)

        @pl.when(i == nreal - 1)
        def _():
            @pl.when(i >= 1)
            def _():
                wait_scatter(i - 1)
            wait_scatter(i)


def _moe(h_rows, top_e, w):
    n = top_e.shape[0]
    flat_e = top_e.reshape(-1)
    n_flat = n * TOP_K
    n_blocks = -(-n_flat // MOE_ROWS) + N_EXPERTS
    order = jnp.argsort(flat_e).astype(jnp.int32)
    counts = jnp.sum((flat_e[:, None] == jnp.arange(N_EXPERTS, dtype=jnp.int32)[None, :]).astype(jnp.int32), axis=0)
    starts = jnp.cumsum(counts) - counts
    padded = (counts + MOE_ROWS - 1) // MOE_ROWS * MOE_ROWS
    pad_ends = jnp.cumsum(padded)
    pad_starts = pad_ends - padded
    blk_start = jnp.arange(n_blocks, dtype=jnp.int32) * MOE_ROWS
    blk_e = jnp.minimum(jnp.sum((pad_ends[None, :] <= blk_start[:, None]).astype(jnp.int32), axis=1),
                        N_EXPERTS - 1).astype(jnp.int32)
    nreal = (pad_ends[-1] // MOE_ROWS).astype(jnp.int32).reshape(1)
    blk_rank = blk_start - pad_starts[blk_e]
    nvalid = jnp.where(blk_start < pad_ends[-1], jnp.clip(counts[blk_e] - blk_rank, 0, MOE_ROWS), 0).astype(jnp.int32)
    r = jnp.arange(MOE_ROWS, dtype=jnp.int32)[None, :]
    src = jnp.clip((starts[blk_e] + blk_rank)[:, None] + r, 0, n_flat - 1)
    rows = jnp.where(r < nvalid[:, None], order[src], 0).astype(jnp.int32)
    rows = jnp.pad(rows, ((0, 0), (0, IDX_STRIDE - MOE_ROWS))).reshape(-1)
    out_rows = n_flat * SUBLANES
    wspec = pl.BlockSpec((1, D_MODEL, D_MODEL), lambda i, be, nv, nr: (be[i], 0, 0))
    bspec = pl.BlockSpec((1, 1, D_MODEL), lambda i, be, nv, nr: (be[i], 0, 0))
    any_spec = pl.BlockSpec(memory_space=pl.ANY)
    grid_spec = pltpu.PrefetchScalarGridSpec(
        num_scalar_prefetch=3,
        grid=(n_blocks,),
        in_specs=[any_spec, any_spec, wspec, bspec, wspec, bspec, wspec, bspec],
        out_specs=any_spec,
        scratch_shapes=[pltpu.SMEM((IDX_SLOTS * IDX_STRIDE,), jnp.int32),
                        pltpu.VMEM((2, MOE_ROWS * SUBLANES, LANES), F32),
                        pltpu.VMEM((2, MOE_ROWS * SUBLANES, LANES), F32),
                        pltpu.VMEM((D_MODEL, D_MODEL), BF), pltpu.VMEM((D_MODEL, D_MODEL), BF),
                        pltpu.VMEM((D_MODEL, D_MODEL), BF),
                        pltpu.SemaphoreType.DMA((IDX_SLOTS,)), pltpu.SemaphoreType.DMA((2,)),
                        pltpu.SemaphoreType.DMA((2,))])
    return pl.pallas_call(
        _moe_kernel,
        grid_spec=grid_spec,
        out_shape=jax.ShapeDtypeStruct((out_rows, LANES), F32),
        compiler_params=_params(56, 1, disable_bounds_checks=True),
    )(blk_e, nvalid, nreal, rows, h_rows, w["wgate"], w["bgate"], w["wup"], w["bup"], w["wdown"], w["bdown"])


def _combine_kernel(h_ref, y_ref, gate_ref, g2_ref, b2_ref, o_ref, *, tm):
    g = gate_ref[...]
    cols = []
    for j in range(D_MODEL // LANES):
        f = None
        for kk in range(TOP_K):
            term = g[:, kk:kk + 1] * y_ref[pl.ds(kk * SUBLANES + j, tm, stride=TOP_K * SUBLANES), :]
            f = term if f is None else f + term
        cols.append(DEEPNORM_ALPHA * h_ref[pl.ds(j, tm, stride=SUBLANES), :] + f)
    o_ref[...] = _layernorm(jnp.concatenate(cols, axis=1), g2_ref[...], b2_ref[...])


def _combine(h_rows, y_rows, gate, g2, b2, first_token):
    n = gate.shape[0]
    tm = min(256, n)
    first = first_token // tm
    return pl.pallas_call(
        functools.partial(_combine_kernel, tm=tm),
        grid=(n // tm,),
        in_specs=[pl.BlockSpec((tm * SUBLANES, LANES), lambda i: (i, 0)),
                  pl.BlockSpec((tm * TOP_K * SUBLANES, LANES), lambda i: (i + first, 0)),
                  pl.BlockSpec((tm, TOP_K), lambda i: (i, 0)), _full((1, D_MODEL)), _full((1, D_MODEL))],
        out_specs=pl.BlockSpec((tm, D_MODEL), lambda i: (i, 0)),
        out_shape=jax.ShapeDtypeStruct((n, D_MODEL), F32),
        compiler_params=_params(40, 1),
    )(h_rows, y_rows, gate, g2, b2)


def _row(v):
    return v.reshape(1, -1).astype(F32)


def _prep_weights(w_in, b_in, b_forget, conv_w, conv_b, conv_ln_g, conv_ln_b, w_conv_out, b_conv_out, w_fox_out,
                  w_mem_kv, w_mem_out, w_out, b_out, ln1_g, ln1_b, w_router, b_router, w_gate, b_gate, w_up, b_up,
                  w_down, b_down, ln2_g, ln2_b):
    o_q = 2 * CONV_CH
    o_f = o_q + 3 * FOX_WIDTH
    o_qm = o_f + FOX_HEADS
    o_g = o_qm + MEM_WIDTH
    pad_f = LANES - FOX_HEADS
    proj = dict(
        wglu=w_in[:, :o_q].astype(BF), bglu=_row(b_in[:o_q]),
        wqkv=w_in[:, o_q:o_f].astype(BF), bqkv=_row(b_in[o_q:o_f]),
        wf=jnp.pad(w_in[:, o_f:o_qm], ((0, 0), (0, pad_f))).astype(BF),
        bf=_row(jnp.pad(b_in[o_f:o_qm], (0, pad_f))), bfg=_row(jnp.pad(b_forget, (0, pad_f))),
        wqm=w_in[:, o_qm:o_g].astype(BF), bqm=_row(b_in[o_qm:o_g]))
    conv = dict(w=jnp.pad(conv_w, ((0, HIST_ROWS - CONV_WIDTH), (0, 0))).astype(F32), cb=_row(conv_b),
                g=_row(conv_ln_g), b=_row(conv_ln_b))
    pad_r = LANES - N_EXPERTS
    wr = jnp.pad(w_router, ((0, 0), (0, pad_r)))
    wr_hi = wr.astype(BF)
    merge = dict(
        wg=w_in[:, o_g:].astype(BF), bg=_row(b_in[o_g:]), wc=w_conv_out.astype(BF), bc=_row(b_conv_out),
        wfo=w_fox_out.astype(BF), wmo=w_mem_out.astype(BF), wo=w_out.astype(BF), bo=_row(b_out),
        g1=_row(ln1_g), b1=_row(ln1_b), wrh=wr_hi, wrl=(wr - wr_hi.astype(F32)).astype(BF),
        br=_row(jnp.pad(b_router, (0, pad_r))))
    moe = dict(wgate=w_gate, bgate=b_gate.reshape(N_EXPERTS, 1, D_MODEL), wup=w_up,
               bup=b_up.reshape(N_EXPERTS, 1, D_MODEL), wdown=w_down, bdown=b_down.reshape(N_EXPERTS, 1, D_MODEL))
    return proj, conv, merge, moe, w_mem_kv.astype(BF), _row(ln2_g), _row(ln2_b)


def _channel(groups, merge_w, moe_w, g2, b2):
    merged = [_merge(x2d, yc, yf, ym, merge_w) for x2d, yc, yf, ym in groups]
    y_rows = _moe(jnp.concatenate([m[0] for m in merged], axis=0), jnp.concatenate([m[1] for m in merged], axis=0),
                  moe_w)
    outs = []
    first = 0
    for h_rows, _, gate in merged:
        outs.append(_combine(h_rows, y_rows, gate, g2, b2, first))
        first += gate.shape[0]
    return outs


def kernel(x_prompt, x_sample, mem_prompt, cache_k, cache_v, cache_logf, page_table, cache_mem_k, cache_mem_v, state_conv, w_in, b_in, b_forget, conv_w, conv_b, conv_ln_g, conv_ln_b, w_conv_out, b_conv_out, w_fox_out, w_mem_kv, w_mem_out, w_out, b_out, ln1_g, ln1_b, w_router, b_router, w_gate, b_gate, w_up, b_up, w_down, b_down, ln2_g, ln2_b):
    proj_w, conv_w_, merge_w, moe_w, wkv, g2, b2 = _prep_weights(
        w_in, b_in, b_forget, conv_w, conv_b, conv_ln_g, conv_ln_b, w_conv_out, b_conv_out, w_fox_out, w_mem_kv,
        w_mem_out, w_out, b_out, ln1_g, ln1_b, w_router, b_router, w_gate, b_gate, w_up, b_up, w_down, b_down,
        ln2_g, ln2_b)
    b, l, d = x_prompt.shape
    db, t, _ = x_sample.shape
    hist_len = CONV_WIDTH - 1

    xp = x_prompt.reshape(b * l, d)
    u, q, k, v, kb, vb, logf, _, qm, aq, ak = _in_proj(xp, l, proj_w)
    u3 = u.reshape(b, l, CONV_CH)
    yc = _conv_prompt(u3, jnp.zeros((b, HIST_ROWS, CONV_CH), F32), conv_w_)
    yf = _fox_prompt(q.reshape(b, l, FOX_WIDTH), kb.reshape(b, l, FOX_WIDTH), vb.reshape(b, l, FOX_WIDTH),
                     aq.reshape(b, l, FOX_HEADS * LANES), ak.reshape(b, l, FOX_HEADS * LANES))
    mk, mv = _mem_kv(mem_prompt.reshape(b * MEM_TOKENS, d), wkv)
    mk3 = mk.reshape(b, MEM_TOKENS, MEM_WIDTH)
    mv3 = mv.reshape(b, MEM_TOKENS, MEM_WIDTH)
    ym = _mem_attend(qm.reshape(b, l, MEM_WIDTH), mk3, mv3, 1, min(512, l))

    xs = x_sample.reshape(db * t, d)
    us, qs, ks, vs, ksb, vsb, logfs, cums, qms, _, _ = _in_proj(xs, t, proj_w)
    us_ext = jnp.concatenate([state_conv.astype(F32), us.reshape(db, t, CONV_CH)], axis=1)
    ycs = _conv_sample(us_ext.transpose(1, 0, 2), conv_w_).transpose(1, 0, 2)
    n_phys = cache_logf.shape[0]
    suf, tot = _page_suffix(cache_logf.transpose(0, 2, 1).reshape(n_phys * FOX_HEADS, PAGE_SIZE))
    yfs = _fox_sample(qs.reshape(db, t, FOX_WIDTH), ksb.reshape(db, t, FOX_WIDTH), vsb.reshape(db, t, FOX_WIDTH),
                      cums.reshape(db, t, FOX_HEADS), cache_k.transpose(0, 2, 3, 1), cache_v.transpose(0, 2, 3, 1),
                      suf.reshape(n_phys, FOX_HEADS, PAGE_SIZE), tot.reshape(n_phys, FOX_HEADS, PAGE_SIZE),
                      page_table)
    t_pad = 2 * SUBLANES
    qms3 = jnp.pad(qms.reshape(db, t, MEM_WIDTH), ((0, 0), (0, t_pad - t), (0, 0)))
    yms = _mem_attend(qms3, cache_mem_k.reshape(db, MEM_TOKENS * MEM_HEADS, MEM_HEAD_DIM),
                      cache_mem_v.reshape(db, MEM_TOKENS * MEM_HEADS, MEM_HEAD_DIM), 8, t_pad)[:, :t]

    y_prompt, y_sample = _channel(
        [(xp, yc.reshape(b * l, CONV_CH), yf.reshape(b * l, FOX_WIDTH), ym.reshape(b * l, MEM_WIDTH)),
         (xs, ycs.reshape(db * t, CONV_CH), yfs.reshape(db * t, FOX_WIDTH), yms.reshape(db * t, MEM_WIDTH))],
        merge_w, moe_w, g2, b2)
    y_prompt = y_prompt.reshape(b, l, d)
    y_sample = y_sample.reshape(db, t, d)

    heads = lambda a, n, s: a.reshape(n, s, FOX_HEADS, FOX_HEAD_DIM)
    return (y_prompt, y_sample,
            heads(k, b, l), heads(v, b, l), logf.reshape(b, l, FOX_HEADS),
            mk.reshape(b, MEM_TOKENS, MEM_HEADS, MEM_HEAD_DIM), mv.reshape(b, MEM_TOKENS, MEM_HEADS, MEM_HEAD_DIM),
            u3[:, l - hist_len:, :],
            heads(ks, db, t), heads(vs, db, t), logfs.reshape(db, t, FOX_HEADS),
            us_ext[:, t:, :])
```

```python
import functools

import numpy as np
import jax
import jax.numpy as jnp
from jax import lax
from jax.experimental import pallas as pl
from jax.experimental.pallas import tpu as pltpu

D_MODEL = 1024
CONV_CH = 512
CONV_WIDTH = 31
FOX_HEADS = 8
FOX_HEAD_DIM = 64
FOX_WIDTH = FOX_HEADS * FOX_HEAD_DIM
MEM_HEADS = 4
MEM_HEAD_DIM = 128
MEM_WIDTH = MEM_HEADS * MEM_HEAD_DIM
MEM_TOKENS = 256
N_EXPERTS = 32
TOP_K = 4
PAGE_SIZE = 128
SWIGLU_LIMIT = 7.0
SWIGLU_ALPHA = 1.702
LN_EPS = 1e-5
DEEPNORM_ALPHA = 2.0 ** 0.25

LANES = 128
SUBLANES = 8
HIST_ROWS = 32
MIB = 1024 * 1024

BF = jnp.bfloat16
F32 = jnp.float32
NEG_INF = float("-inf")


def _dot(a, b):
    return jnp.dot(a, b, preferred_element_type=F32)


def _dot_nt(a, b):
    return lax.dot_general(a, b, (((1,), (1,)), ((), ())), preferred_element_type=F32)


def _params(vmem_mib, n_axes, **kw):
    return pltpu.CompilerParams(dimension_semantics=("arbitrary",) * n_axes,
                                vmem_limit_bytes=vmem_mib * MIB, **kw)


def _full(shape):
    nd = len(shape)
    return pl.BlockSpec(shape, lambda *_: (0,) * nd)


def _split3(x):
    hi = x.astype(BF)
    r1 = x - hi.astype(F32)
    mid = r1.astype(BF)
    lo = (r1 - mid.astype(F32)).astype(BF)
    return hi, mid, lo


def _layernorm(x, g, b):
    mu = jnp.mean(x, axis=-1, keepdims=True)
    xc = x - mu
    var = jnp.mean(xc * xc, axis=-1, keepdims=True)
    return xc * lax.rsqrt(var + LN_EPS) * g + b


def _in_proj_kernel(x_ref, tri_ref, wglu_ref, wqkv_ref, wf_ref, wqm_ref, bglu_ref, bqkv_ref, bf_ref, bfg_ref,
                    bqm_ref, pq_ref, pk_ref, cq_ref, ck_ref, u_ref, q_ref, k_ref, v_ref, kb_ref, vb_ref, logf_ref,
                    cum_ref, qm_ref, aq_ref, ak_ref, carry_ref, *, tm, seq_len):
    i = pl.program_id(0)
    xb = x_ref[...].astype(BF)
    glu = _dot(xb, wglu_ref[...]) + bglu_ref[...]
    u_ref[...] = glu[:, :CONV_CH] * jax.nn.sigmoid(glu[:, CONV_CH:])
    qkv = _dot(xb, wqkv_ref[...]) + bqkv_ref[...]
    q_ref[...] = (qkv[:, :FOX_WIDTH] * (FOX_HEAD_DIM ** -0.5)).astype(BF)
    k = qkv[:, FOX_WIDTH:2 * FOX_WIDTH]
    v = qkv[:, 2 * FOX_WIDTH:]
    k_ref[...] = k
    v_ref[...] = v
    kb_ref[...] = k.astype(BF)
    vb_ref[...] = v.astype(BF)
    qm_ref[...] = (_dot(xb, wqm_ref[...]) + bqm_ref[...]).astype(BF)
    f = (_dot(xb, wf_ref[...]) + bf_ref[...]) + bfg_ref[...]
    lf = jnp.minimum(f, 0.0) - jnp.log1p(jnp.exp(-jnp.abs(f)))
    logf_ref[...] = lf[:, :FOX_HEADS]
    hi, mid, lo = _split3(lf)
    tri = tri_ref[...]
    cum = _dot(tri, hi) + _dot(tri, mid) + _dot(tri, lo)
    if seq_len > tm:
        @pl.when(i % (seq_len // tm) == 0)
        def _():
            carry_ref[...] = jnp.zeros_like(carry_ref)
        cum = cum + carry_ref[...]
        carry_ref[...] = cum[tm - 1:tm, :]
    cum_ref[...] = cum[:, :FOX_HEADS]
    parts = _split3(cum)
    aq = cq_ref[...] + _dot(parts[0], pq_ref[0]) + _dot(parts[1], pq_ref[1]) + _dot(parts[2], pq_ref[2])
    ak = ck_ref[...] + _dot(parts[0], pk_ref[0]) + _dot(parts[1], pk_ref[1]) + _dot(parts[2], pk_ref[2])
    aq_ref[...] = aq.astype(BF)
    ak_ref[...] = ak.astype(BF)


def _aug_lane(h):
    return h * LANES + (FOX_HEAD_DIM if h % 2 == 0 else 0)


def _aug_constants():
    pq = np.zeros((3, LANES, FOX_HEADS * LANES), np.float32)
    pk = np.zeros((3, LANES, FOX_HEADS * LANES), np.float32)
    cq = np.zeros((1, FOX_HEADS * LANES), np.float32)
    ck = np.zeros((1, FOX_HEADS * LANES), np.float32)
    for h in range(FOX_HEADS):
        base = _aug_lane(h)
        for j in range(3):
            pq[j, h, base + j] = 1.0
            pk[j, h, base + 3 + j] = -1.0
            cq[0, base + 3 + j] = 1.0
            ck[0, base + j] = 1.0
    return jnp.asarray(pq, BF), jnp.asarray(pk, BF), jnp.asarray(cq), jnp.asarray(ck)


def _in_proj(x2d, seq_len, w):
    n = x2d.shape[0]
    tm = min(512, n)
    lc = min(seq_len, tm)
    r = np.arange(tm)
    tri = jnp.asarray(((r[None, :] <= r[:, None]) & (r[None, :] // lc == r[:, None] // lc)).astype(np.float32), BF)
    tok = lambda width: pl.BlockSpec((tm, width), lambda i: (i, 0))
    out_shape = (jax.ShapeDtypeStruct((n, CONV_CH), F32), jax.ShapeDtypeStruct((n, FOX_WIDTH), BF),
                 jax.ShapeDtypeStruct((n, FOX_WIDTH), F32), jax.ShapeDtypeStruct((n, FOX_WIDTH), F32),
                 jax.ShapeDtypeStruct((n, FOX_WIDTH), BF), jax.ShapeDtypeStruct((n, FOX_WIDTH), BF),
                 jax.ShapeDtypeStruct((n, FOX_HEADS), F32), jax.ShapeDtypeStruct((n, FOX_HEADS), F32),
                 jax.ShapeDtypeStruct((n, MEM_WIDTH), BF),
                 jax.ShapeDtypeStruct((n, FOX_HEADS * LANES), BF), jax.ShapeDtypeStruct((n, FOX_HEADS * LANES), BF))
    ins = (x2d, tri, w["wglu"], w["wqkv"], w["wf"], w["wqm"], w["bglu"], w["bqkv"], w["bf"], w["bfg"], w["bqm"],
           *_aug_constants())
    return pl.pallas_call(
        functools.partial(_in_proj_kernel, tm=tm, seq_len=seq_len),
        grid=(n // tm,),
        in_specs=[tok(D_MODEL)] + [_full(a.shape) for a in ins[1:]],
        out_specs=(tok(CONV_CH), tok(FOX_WIDTH), tok(FOX_WIDTH), tok(FOX_WIDTH), tok(FOX_WIDTH), tok(FOX_WIDTH),
                   tok(FOX_HEADS), tok(FOX_HEADS), tok(MEM_WIDTH), tok(FOX_HEADS * LANES), tok(FOX_HEADS * LANES)),
        out_shape=out_shape,
        scratch_shapes=[pltpu.VMEM((1, LANES), F32)],
        compiler_params=_params(56, 1),
    )(*ins)


def _conv_post(y, cb_ref, g_ref, b_ref):
    y = _layernorm(y + cb_ref[...], g_ref[...], b_ref[...])
    return (y * jax.nn.sigmoid(y)).astype(BF)


def _conv_prompt_kernel(u_ref, prev_ref, hist_ref, w_ref, cb_ref, g_ref, b_ref, o_ref, win_ref, y_ref, *, tm):
    i = pl.program_id(1)
    win_ref[0:HIST_ROWS, :] = jnp.where(i == 0, hist_ref[0], prev_ref[0])
    win_ref[HIST_ROWS:, :] = u_ref[0]
    first = HIST_ROWS - (CONV_WIDTH - 1)
    for c in range(CONV_CH // LANES):
        cs = slice(c * LANES, (c + 1) * LANES)
        acc = jnp.zeros((tm, LANES), F32)
        for j in range(CONV_WIDTH):
            acc = acc + w_ref[j:j + 1, cs] * win_ref[first + j:first + j + tm, cs]
        y_ref[:, cs] = acc
    o_ref[0] = _conv_post(y_ref[...], cb_ref, g_ref, b_ref)


def _conv_prompt(u3, hist, cw):
    b, l, _ = u3.shape
    tm = 256
    per = tm // HIST_ROWS
    vec = _full((1, CONV_CH))
    return pl.pallas_call(
        functools.partial(_conv_prompt_kernel, tm=tm),
        grid=(b, l // tm),
        in_specs=[pl.BlockSpec((1, tm, CONV_CH), lambda bi, i: (bi, i, 0)),
                  pl.BlockSpec((1, HIST_ROWS, CONV_CH), lambda bi, i: (bi, jnp.maximum(i * per - 1, 0), 0)),
                  pl.BlockSpec((1, HIST_ROWS, CONV_CH), lambda bi, i: (bi, 0, 0)),
                  _full((HIST_ROWS, CONV_CH)), vec, vec, vec],
        out_specs=pl.BlockSpec((1, tm, CONV_CH), lambda bi, i: (bi, i, 0)),
        out_shape=jax.ShapeDtypeStruct((b, l, CONV_CH), BF),
        scratch_shapes=[pltpu.VMEM((tm + HIST_ROWS, CONV_CH), F32), pltpu.VMEM((tm, CONV_CH), F32)],
        compiler_params=_params(32, 2),
    )(u3, u3, hist, cw["w"], cw["cb"], cw["g"], cw["b"])


def _conv_sample_kernel(x_ref, w_ref, cb_ref, g_ref, b_ref, o_ref, *, steps):
    for t in range(steps):
        acc = jnp.zeros(x_ref.shape[1:], F32)
        for j in range(CONV_WIDTH):
            acc = acc + w_ref[j:j + 1, :] * x_ref[t + j]
        o_ref[t] = _conv_post(acc, cb_ref, g_ref, b_ref)


def _conv_sample(u_ext_t, cw):
    rows, b, _ = u_ext_t.shape
    steps = rows - (CONV_WIDTH - 1)
    bb = min(64, b)
    vec = _full((1, CONV_CH))
    return pl.pallas_call(
        functools.partial(_conv_sample_kernel, steps=steps),
        grid=(b // bb,),
        in_specs=[pl.BlockSpec((rows, bb, CONV_CH), lambda i: (0, i, 0)), _full((HIST_ROWS, CONV_CH)), vec, vec, vec],
        out_specs=pl.BlockSpec((steps, bb, CONV_CH), lambda i: (0, i, 0)),
        out_shape=jax.ShapeDtypeStruct((steps, b, CONV_CH), BF),
        compiler_params=_params(32, 1),
    )(u_ext_t, cw["w"], cw["cb"], cw["g"], cw["b"])


def _prompt_attention(qi_ref, kj_ref, q_ref, k_ref, v_ref, aq_ref, ak_ref, o_ref, qa_ref, ka_ref, va_ref, m_ref,
                      acc_ref, *, tile, sub, also):
    t = pl.program_id(2)
    qi = qi_ref[t]
    kj = kj_ref[t]
    nsub = tile // sub
    lane = lax.broadcasted_iota(jnp.int32, (1, LANES), 1)
    in_head = [(lane >= hh * FOX_HEAD_DIM) & (lane < (hh + 1) * FOX_HEAD_DIM) for hh in range(2)]
    sum_lane = [_aug_lane(hh) % LANES for hh in range(2)]

    for hh in range(2):
        ka_ref[hh] = jnp.where(in_head[hh], k_ref[0], ak_ref[0, :, hh * LANES:(hh + 1) * LANES])
        va_ref[hh] = jnp.where(in_head[hh], v_ref[0], jnp.where(lane == sum_lane[hh], 1.0, 0.0).astype(BF))

    @pl.when(kj == 0)
    def _():
        m_ref[...] = jnp.full_like(m_ref, NEG_INF)
        acc_ref[...] = jnp.zeros_like(acc_ref)
        for hh in range(2):
            qa_ref[hh] = jnp.where(in_head[hh], q_ref[0], aq_ref[0, :, hh * LANES:(hh + 1) * LANES])

    def attend(hh, i2, j2, masked):
        rows = slice(i2 * sub, (i2 + 1) * sub)
        cols = slice(j2 * sub, (j2 + 1) * sub)
        s = _dot_nt(qa_ref[hh, rows, :], ka_ref[hh, cols, :])
        if masked:
            r_id = lax.broadcasted_iota(jnp.int32, (sub, sub), 0)
            c_id = lax.broadcasted_iota(jnp.int32, (sub, sub), 1)
            s = jnp.where(c_id <= r_id, s, NEG_INF)
        m_prev = m_ref[hh, rows, :]
        m_next = jnp.maximum(m_prev, jnp.max(s, axis=1, keepdims=True))
        alpha = jnp.exp(m_prev - m_next)
        p = jnp.exp(s - jnp.concatenate([m_next] * (sub // LANES), axis=1))
        acc_ref[hh, rows, :] = alpha * acc_ref[hh, rows, :] + _dot(p.astype(BF), va_ref[hh, cols, :])
        m_ref[hh, rows, :] = m_next

    @pl.when(kj < qi)
    def _():
        for j2 in range(nsub):
            for i2 in range(nsub):
                for hh in range(2):
                    attend(hh, i2, j2, False)
        also()

    @pl.when(kj == qi)
    def _():
        for j2 in range(nsub):
            for i2 in range(j2, nsub):
                for hh in range(2):
                    attend(hh, i2, j2, j2 == i2)
        also()
        outs = []
        for hh in range(2):
            acc = acc_ref[hh]
            outs.append(acc / acc[:, sum_lane[hh]:sum_lane[hh] + 1])
        o_ref[0] = jnp.where(in_head[0], outs[0], outs[1]).astype(BF)


def _page_suffix_kernel(x_ref, upper_ref, ones_ref, o_ref):
    hi, mid, lo = _split3(x_ref[...])
    up = upper_ref[...]
    on = ones_ref[...]
    n = o_ref.shape[0]
    o_ref[:, :FOX_HEADS, :] = (_dot(hi, up) + _dot(mid, up) + _dot(lo, up)).reshape(n, FOX_HEADS, PAGE_SIZE)
    o_ref[:, FOX_HEADS:, :] = (_dot(hi, on) + _dot(mid, on) + _dot(lo, on)).reshape(n, FOX_HEADS, PAGE_SIZE)


def _page_suffix(logf_t):
    rows = logf_t.shape[0]
    tr = min(2048, rows)
    kk = np.arange(PAGE_SIZE)
    upper = jnp.asarray((kk[:, None] > kk[None, :]).astype(np.float32), BF)
    ones = jnp.ones((PAGE_SIZE, PAGE_SIZE), BF)
    return pl.pallas_call(
        _page_suffix_kernel,
        grid=(rows // tr,),
        in_specs=[pl.BlockSpec((tr, PAGE_SIZE), lambda i: (i, 0)), _full((PAGE_SIZE, PAGE_SIZE)),
                  _full((PAGE_SIZE, PAGE_SIZE))],
        out_specs=pl.BlockSpec((tr // FOX_HEADS, 2 * FOX_HEADS, PAGE_SIZE), lambda i: (i, 0, 0)),
        out_shape=jax.ShapeDtypeStruct((rows // FOX_HEADS, 2 * FOX_HEADS, PAGE_SIZE), F32),
        compiler_params=_params(32, 1),
    )(logf_t, upper, ones)


def _sample_attention(j, is_last, qrep_ref, cq_ref, ct_ref, kn_ref, vn_ref, k_refs, v_refs, gate_refs,
                      o_ref, m_ref, l_ref, acc_ref, carry_ref, *, pages, steps):
    rows = steps * FOX_HEADS
    row_id = lax.broadcasted_iota(jnp.int32, (rows, FOX_WIDTH), 0)
    col_id = lax.broadcasted_iota(jnp.int32, (rows, FOX_WIDTH), 1)
    head_mask = (col_id // FOX_HEAD_DIM) == (row_id % FOX_HEADS)
    qrep = qrep_ref[0]
    qe = jnp.where(head_mask, qrep, jnp.zeros_like(qrep))
    cq = cq_ref[0]

    def update(s, pv_fn):
        m_prev = m_ref[...]
        m_next = jnp.maximum(m_prev, jnp.max(s, axis=1, keepdims=True))
        alpha = jnp.exp(m_prev - m_next)
        p = jnp.exp(s - jnp.concatenate([m_next] * (s.shape[1] // LANES), axis=1))
        l_ref[...] = alpha * l_ref[...] + jnp.sum(p, axis=1, keepdims=True)
        acc_ref[...] = jnp.concatenate([alpha] * (FOX_WIDTH // LANES), axis=1) * acc_ref[...] + pv_fn(p.astype(BF))
        m_ref[...] = m_next

    @pl.when(j == 0)
    def _():
        m_ref[...] = jnp.full_like(m_ref, NEG_INF)
        l_ref[...] = jnp.zeros_like(l_ref)
        acc_ref[...] = jnp.zeros_like(acc_ref)
        carry_ref[...] = jnp.zeros_like(carry_ref)
        s = _dot_nt(qe, kn_ref[0]) + (cq - jnp.concatenate([ct_ref[0]] * steps, axis=0))
        r_id = lax.broadcasted_iota(jnp.int32, (rows, LANES), 0)
        c_id = lax.broadcasted_iota(jnp.int32, (rows, LANES), 1)
        s = jnp.where(c_id <= r_id // FOX_HEADS, s, NEG_INF)
        update(s, lambda p: _dot(p, vn_ref[0]))

    def main():
        carry = carry_ref[...]
        scores = [None] * pages
        for r in reversed(range(pages)):
            bias = cq + carry + jnp.concatenate([gate_refs[r][0, :FOX_HEADS, :]] * steps, axis=0)
            scores[r] = _dot(qe, k_refs[r][0].reshape(FOX_WIDTH, PAGE_SIZE).astype(BF)) + bias
            carry = carry + jnp.concatenate([gate_refs[r][0, FOX_HEADS:, :]] * steps, axis=0)
        carry_ref[...] = carry

        def pv_pages(p):
            out = None
            for r in range(pages):
                term = _dot_nt(p[:, r * PAGE_SIZE:(r + 1) * PAGE_SIZE],
                               v_refs[r][0].reshape(FOX_WIDTH, PAGE_SIZE).astype(BF))
                out = term if out is None else out + term
            return out

        update(jnp.concatenate(scores, axis=1), pv_pages)

    def finish():
        @pl.when(is_last)
        def _():
            o = jnp.where(head_mask, acc_ref[...] / jnp.concatenate([l_ref[...]] * (FOX_WIDTH // LANES), axis=1),
                          0.0)
            o_ref[0] = jnp.sum(o.reshape(steps, FOX_HEADS, FOX_WIDTH), axis=1).astype(BF)

    return main, finish


N_PROMPT_IN = 5
N_SAMPLE_IN = 5


def _fox_kernel(qi_ref, kj_ref, pt_ref, *refs, tile, sub, pages, steps, n_steps):
    del pt_ref
    prompt_in = refs[:N_PROMPT_IN]
    sample_in = refs[N_PROMPT_IN:N_PROMPT_IN + N_SAMPLE_IN]
    page_refs = refs[N_PROMPT_IN + N_SAMPLE_IN:N_PROMPT_IN + N_SAMPLE_IN + 3 * pages]
    op_ref, os_ref = refs[N_PROMPT_IN + N_SAMPLE_IN + 3 * pages:][:2]
    scratch = refs[N_PROMPT_IN + N_SAMPLE_IN + 3 * pages + 2:]
    step = (pl.program_id(0) * pl.num_programs(1) + pl.program_id(1)) * pl.num_programs(2) + pl.program_id(2)
    j = lax.rem(step, n_steps)
    main, finish = _sample_attention(
        j, j == n_steps - 1, *sample_in, page_refs[:pages], page_refs[pages:2 * pages], page_refs[2 * pages:],
        os_ref, *scratch[5:], pages=pages, steps=steps)
    _prompt_attention(qi_ref, kj_ref, *prompt_in, op_ref, *scratch[:5], tile=tile, sub=sub, also=main)
    finish()


def _fox_attention(q3, k3, v3, aq3, ak3, qs3, kn3, vn3, cum3, cache_kt, cache_vt, page_gates, page_table):
    b, l, _ = q3.shape
    tile = min(1024, l)
    sub = min(512, tile)
    nq = l // tile
    pairs = FOX_HEADS // 2
    qi = np.concatenate([np.full(i + 1, i) for i in range(nq)]).astype(np.int32)
    kj = np.concatenate([np.arange(i + 1) for i in range(nq)]).astype(np.int32)
    n_tiles = len(qi)

    db, steps, _ = qs3.shape
    n_pages = page_table.shape[1]
    pages = min(16, n_pages)
    n_steps = n_pages // pages
    rows = steps * FOX_HEADS
    assert b * pairs * n_tiles >= db * n_steps
    qrep = jnp.repeat(qs3, FOX_HEADS, axis=1)
    cq = jnp.broadcast_to(cum3.reshape(db, rows, 1), (db, rows, LANES))
    ct = jnp.pad(cum3.transpose(0, 2, 1), ((0, 0), (0, 0), (0, LANES - steps)))
    kn = jnp.pad(kn3, ((0, 0), (0, LANES - steps), (0, 0)))
    vn = jnp.pad(vn3, ((0, 0), (0, LANES - steps), (0, 0)))

    def step_of(bi, p, t):
        return (bi * pairs + p) * n_tiles + t

    def seq_of(bi, p, t):
        return jnp.minimum(step_of(bi, p, t) // n_steps, db - 1)

    def page_map(r, nd):
        def index(bi, p, t, qi_r, kj_r, pt):
            j = lax.rem(step_of(bi, p, t), n_steps)
            return (pt[seq_of(bi, p, t), (n_steps - 1 - j) * pages + r],) + (0,) * (nd - 1)
        return index

    seq = lambda shape: pl.BlockSpec((1,) + shape, lambda bi, p, t, qi_r, kj_r, pt: (seq_of(bi, p, t), 0, 0))
    in_specs = [pl.BlockSpec((1, tile, LANES), lambda bi, p, t, qi_r, kj_r, pt: (bi, qi_r[t], p)),
                pl.BlockSpec((1, tile, LANES), lambda bi, p, t, qi_r, kj_r, pt: (bi, kj_r[t], p)),
                pl.BlockSpec((1, tile, LANES), lambda bi, p, t, qi_r, kj_r, pt: (bi, kj_r[t], p)),
                pl.BlockSpec((1, tile, 2 * LANES), lambda bi, p, t, qi_r, kj_r, pt: (bi, qi_r[t], p)),
                pl.BlockSpec((1, tile, 2 * LANES), lambda bi, p, t, qi_r, kj_r, pt: (bi, kj_r[t], p))]
    in_specs += [seq((rows, FOX_WIDTH)), seq((rows, LANES)), seq((FOX_HEADS, LANES)),
                 seq((LANES, FOX_WIDTH)), seq((LANES, FOX_WIDTH))]
    in_specs += [pl.BlockSpec((1, FOX_HEADS, FOX_HEAD_DIM, PAGE_SIZE), page_map(r, 4)) for r in range(pages)] * 2
    in_specs += [pl.BlockSpec((1, 2 * FOX_HEADS, PAGE_SIZE), page_map(r, 3)) for r in range(pages)]
    grid_spec = pltpu.PrefetchScalarGridSpec(
        num_scalar_prefetch=3,
        grid=(b, pairs, n_tiles),
        in_specs=in_specs,
        out_specs=(pl.BlockSpec((1, tile, LANES), lambda bi, p, t, qi_r, kj_r, pt: (bi, qi_r[t], p)),
                   seq((steps, FOX_WIDTH))),
        scratch_shapes=[pltpu.VMEM((2, tile, LANES), BF)] * 3 + [pltpu.VMEM((2, tile, LANES), F32)] * 2 +
                       [pltpu.VMEM((rows, LANES), F32), pltpu.VMEM((rows, LANES), F32),
                        pltpu.VMEM((rows, FOX_WIDTH), F32), pltpu.VMEM((rows, LANES), F32)])
    return pl.pallas_call(
        functools.partial(_fox_kernel, tile=tile, sub=sub, pages=pages, steps=steps, n_steps=n_steps),
        grid_spec=grid_spec,
        out_shape=(jax.ShapeDtypeStruct((b, l, FOX_WIDTH), BF), jax.ShapeDtypeStruct((db, steps, FOX_WIDTH), BF)),
        compiler_params=_params(48, 3),
    )(jnp.asarray(qi), jnp.asarray(kj), page_table, q3, k3, v3, aq3, ak3, qrep, cq, ct, kn, vn,
      *([cache_kt] * pages), *([cache_vt] * pages), *([page_gates] * pages))


def _mem_kv_kernel(m_ref, w_ref, k_ref, v_ref):
    kv = _dot(m_ref[...].astype(BF), w_ref[...])
    k_ref[...] = kv[:, :MEM_WIDTH]
    v_ref[...] = kv[:, MEM_WIDTH:]


def _mem_kv(mem2d, w_bf):
    n = mem2d.shape[0]
    tm = min(256, n)
    out = pl.BlockSpec((tm, MEM_WIDTH), lambda i: (i, 0))
    return pl.pallas_call(
        _mem_kv_kernel,
        grid=(n // tm,),
        in_specs=[pl.BlockSpec((tm, D_MODEL), lambda i: (i, 0)), _full(w_bf.shape)],
        out_specs=(out, out),
        out_shape=(jax.ShapeDtypeStruct((n, MEM_WIDTH), F32),) * 2,
        compiler_params=_params(32, 1),
    )(mem2d, w_bf)


def _mem_attend_kernel(q_ref, k_ref, v_ref, o_ref, *, bb, rows_by_head):
    for b in range(bb):
        for h in range(MEM_HEADS):
            hs = slice(h * MEM_HEAD_DIM, (h + 1) * MEM_HEAD_DIM)
            if rows_by_head:
                k = k_ref[b, pl.ds(h, MEM_TOKENS, stride=MEM_HEADS), :]
                v = v_ref[b, pl.ds(h, MEM_TOKENS, stride=MEM_HEADS), :]
            else:
                k = k_ref[b, :, hs]
                v = v_ref[b, :, hs]
            s = _dot_nt(q_ref[b, :, hs], k.astype(BF)) * (MEM_HEAD_DIM ** -0.5)
            p = jnp.exp(s - jnp.max(s, axis=1, keepdims=True))
            den = jnp.sum(p, axis=1, keepdims=True)
            o_ref[b, :, hs] = (_dot(p.astype(BF), v.astype(BF)) / den).astype(BF)


def _mem_attend(qm3, mk3, mv3, bb, tq):
    b, l, _ = qm3.shape
    rows_by_head = mk3.shape[2] == MEM_HEAD_DIM
    kv = pl.BlockSpec((bb,) + mk3.shape[1:], lambda bi, i: (bi, 0, 0))
    qs = pl.BlockSpec((bb, tq, MEM_WIDTH), lambda bi, i: (bi, i, 0))
    return pl.pallas_call(
        functools.partial(_mem_attend_kernel, bb=bb, rows_by_head=rows_by_head),
        grid=(b // bb, l // tq),
        in_specs=[qs, kv, kv],
        out_specs=qs,
        out_shape=jax.ShapeDtypeStruct((b, l, MEM_WIDTH), BF),
        compiler_params=_params(40, 2),
    )(qm3, mk3, mv3)


def _merge_kernel(x_ref, yc_ref, yf_ref, ym_ref, wg_ref, bg_ref, wc_ref, bc_ref, wfo_ref, wmo_ref, wo_ref, bo_ref,
                  g1_ref, b1_ref, wrh_ref, wrl_ref, br_ref, h_ref, e_ref, gate_ref, *, tm):
    x = x_ref[...]
    xb = x.astype(BF)

    def gate(c):
        cs = slice(c * D_MODEL, (c + 1) * D_MODEL)
        return jax.nn.sigmoid(_dot(xb, wg_ref[:, cs]) + bg_ref[:, cs])

    mix = gate(0) * (_dot(yc_ref[...], wc_ref[...]) + bc_ref[...])
    mix = mix + gate(1) * _dot(yf_ref[...], wfo_ref[...])
    mix = mix + gate(2) * _dot(ym_ref[...], wmo_ref[...])
    pre = DEEPNORM_ALPHA * x + (_dot(mix.astype(BF), wo_ref[...]) + bo_ref[...])
    h = _layernorm(pre, g1_ref[...], b1_ref[...])
    for j in range(D_MODEL // LANES):
        h_ref[pl.ds(j, tm, stride=SUBLANES), :] = h[:, j * LANES:(j + 1) * LANES]

    h_hi = h.astype(BF)
    h_lo = (h - h_hi.astype(F32)).astype(BF)
    logits = _dot(h_hi, wrh_ref[...]) + _dot(h_hi, wrl_ref[...]) + _dot(h_lo, wrh_ref[...]) + br_ref[...]
    lane = lax.broadcasted_iota(jnp.int32, (tm, LANES), 1)
    lane_f = lane.astype(F32)
    work = jnp.where(lane < N_EXPERTS, logits, NEG_INF)
    vals, idxs = [], []
    for _ in range(TOP_K):
        mx = jnp.max(work, axis=1, keepdims=True)
        idx = jnp.min(jnp.where(work == mx, lane_f, float(LANES)), axis=1, keepdims=True)
        vals.append(mx)
        idxs.append(idx)
        work = jnp.where(lane_f == idx, NEG_INF, work)
    exps = [jnp.exp(v - vals[0]) for v in vals]
    den = exps[0] + exps[1] + exps[2] + exps[3]
    e_out = jnp.zeros((tm, LANES), F32)
    g_out = jnp.zeros((tm, LANES), F32)
    for kk in range(TOP_K):
        e_out = jnp.where(lane == kk, idxs[kk], e_out)
        g_out = jnp.where(lane == kk, exps[kk] / den, g_out)
    e_ref[...] = e_out[:, :TOP_K].astype(jnp.int32)
    gate_ref[...] = g_out[:, :TOP_K]


def _merge(x2d, yc, yf, ym, w):
    n = x2d.shape[0]
    tm = min(256, n)
    tok = lambda width: pl.BlockSpec((tm, width), lambda i: (i, 0))
    ws = (w["wg"], w["bg"], w["wc"], w["bc"], w["wfo"], w["wmo"], w["wo"], w["bo"], w["g1"], w["b1"],
          w["wrh"], w["wrl"], w["br"])
    return pl.pallas_call(
        functools.partial(_merge_kernel, tm=tm),
        grid=(n // tm,),
        in_specs=[tok(D_MODEL), tok(CONV_CH), tok(FOX_WIDTH), tok(MEM_WIDTH)] + [_full(a.shape) for a in ws],
        out_specs=(pl.BlockSpec((tm * SUBLANES, LANES), lambda i: (i, 0)), tok(TOP_K), tok(TOP_K)),
        out_shape=(jax.ShapeDtypeStruct((n * SUBLANES, LANES), F32), jax.ShapeDtypeStruct((n, TOP_K), jnp.int32),
                   jax.ShapeDtypeStruct((n, TOP_K), F32)),
        compiler_params=_params(56, 1),
    )(x2d, yc, yf, ym, *ws)


MOE_ROWS = 256
IDX_SLOTS = 4
IDX_STRIDE = 1024
DMA_UNROLL_BITS = 3
DMA_UNROLL = 1 << DMA_UNROLL_BITS
TOP_K_BITS = TOP_K.bit_length() - 1


def _moe_kernel(blk_e_ref, nvalid_ref, nreal_ref, rows_hbm, h_hbm, wg_ref, bg_ref, wu_ref, bu_ref, wd_ref, bd_ref,
                y_hbm, idx_ref, xbuf, ybuf, wgb, wub, wdb, idx_sem, in_sem, out_sem):
    i = pl.program_id(0)
    nreal = nreal_ref[0]
    tile = SUBLANES

    def idx_slot(blk):
        return blk & (IDX_SLOTS - 1)

    def idx_copy(blk):
        slot = idx_slot(blk)
        return pltpu.make_async_copy(rows_hbm.at[pl.ds(pl.multiple_of(blk * IDX_STRIDE, IDX_STRIDE), IDX_STRIDE)],
                                     idx_ref.at[pl.ds(pl.multiple_of(slot * IDX_STRIDE, IDX_STRIDE), IDX_STRIDE)],
                                     idx_sem.at[slot])

    def row_index(islot, r):
        return idx_ref[islot * IDX_STRIDE + r]

    def gather_row(islot, slot, r):
        tok = row_index(islot, r) >> TOP_K_BITS
        return pltpu.make_async_copy(h_hbm.at[pl.ds(pl.multiple_of(tok * tile, tile), tile)],
                                     xbuf.at[slot, pl.ds(pl.multiple_of(r * tile, tile), tile)],
                                     in_sem.at[slot])

    def scatter_row(islot, slot, r):
        dst = row_index(islot, r)
        return pltpu.make_async_copy(ybuf.at[slot, pl.ds(pl.multiple_of(r * tile, tile), tile)],
                                     y_hbm.at[pl.ds(pl.multiple_of(dst * tile, tile), tile)],
                                     out_sem.at[slot])

    def start_gather(blk):
        islot = idx_slot(blk)
        slot = blk & 1

        def body(c, carry):
            for u in range(DMA_UNROLL):
                gather_row(islot, slot, c * DMA_UNROLL + u).start()
            return carry
        lax.fori_loop(0, MOE_ROWS // DMA_UNROLL, body, 0)

    def wait_rows(sem, slot):
        pltpu.make_async_copy(xbuf.at[slot], ybuf.at[slot], sem.at[slot]).wait()

    def wait_scatter(blk):
        slot = blk & 1
        nv = nvalid_ref[blk]

        @pl.when(nv == MOE_ROWS)
        def _():
            wait_rows(out_sem, slot)

        @pl.when(nv < MOE_ROWS)
        def _():
            def body(r, c):
                pltpu.make_async_copy(ybuf.at[slot, pl.ds(0, tile)], y_hbm.at[pl.ds(0, tile)],
                                      out_sem.at[slot]).wait()
                return c
            lax.fori_loop(0, nv, body, 0)

    @pl.when((i == 0) & (nreal > 0))
    def _():
        idx_copy(0).start()
        idx_copy(0).wait()
        start_gather(0)

        @pl.when(nreal > 1)
        def _():
            idx_copy(1).start()

    @pl.when(i + 1 < nreal)
    def _():
        idx_copy(i + 1).wait()
        start_gather(i + 1)

        @pl.when(i + 2 < nreal)
        def _():
            idx_copy(i + 2).start()

    changed = (i == 0) | (blk_e_ref[i] != blk_e_ref[jnp.maximum(i - 1, 0)])

    @pl.when(changed & (i < nreal))
    def _():
        wgb[...] = wg_ref[0].astype(BF)
        wub[...] = wu_ref[0].astype(BF)
        wdb[...] = wd_ref[0].astype(BF)

    @pl.when(i < nreal)
    def _():
        slot = i & 1
        islot = idx_slot(i)
        wait_rows(in_sem, slot)

        @pl.when(i >= 2)
        def _():
            wait_scatter(i - 2)

        x = jnp.concatenate([xbuf[slot, pl.ds(j, MOE_ROWS, stride=SUBLANES), :] for j in range(D_MODEL // LANES)],
                            axis=1).astype(BF)

        a = jnp.minimum(_dot(x, wgb[...]) + bg_ref[0], SWIGLU_LIMIT)
        u = jnp.clip(_dot(x, wub[...]) + bu_ref[0], -SWIGLU_LIMIT, SWIGLU_LIMIT)
        hid = (u + 1.0) * a * jax.nn.sigmoid(SWIGLU_ALPHA * a)
        y = _dot(hid.astype(BF), wdb[...]) + bd_ref[0]
        for j in range(D_MODEL // LANES):
            ybuf[slot, pl.ds(j, MOE_ROWS, stride=SUBLANES), :] = y[:, j * LANES:(j + 1) * LANES]

        nv = nvalid_ref[i]

        def body(c, carry):
            for u in range(DMA_UNROLL):
                r = c * DMA_UNROLL + u

                @pl.when(r < nv)
                def _():
                    scatter_row(islot, slot, r).start()
            return carry
        lax.fori_loop(0, (nv + DMA_UNROLL - 1) >> DMA_UNROLL_BITS, body, 0)

        @pl.when(i == nreal - 1)
        def _():
            @pl.when(i >= 1)
            def _():
                wait_scatter(i - 1)
            wait_scatter(i)


def _moe(h_rows, top_e, w):
    n = top_e.shape[0]
    flat_e = top_e.reshape(-1)
    n_flat = n * TOP_K
    n_blocks = -(-n_flat // MOE_ROWS) + N_EXPERTS
    order = jnp.argsort(flat_e).astype(jnp.int32)
    counts = jnp.sum((flat_e[:, None] == jnp.arange(N_EXPERTS, dtype=jnp.int32)[None, :]).astype(jnp.int32), axis=0)
    starts = jnp.cumsum(counts) - counts
    padded = (counts + MOE_ROWS - 1) // MOE_ROWS * MOE_ROWS
    pad_ends = jnp.cumsum(padded)
    pad_starts = pad_ends - padded
    blk_start = jnp.arange(n_blocks, dtype=jnp.int32) * MOE_ROWS
    blk_e = jnp.minimum(jnp.sum((pad_ends[None, :] <= blk_start[:, None]).astype(jnp.int32), axis=1),
                        N_EXPERTS - 1).astype(jnp.int32)
    nreal = (pad_ends[-1] // MOE_ROWS).astype(jnp.int32).reshape(1)
    blk_rank = blk_start - pad_starts[blk_e]
    nvalid = jnp.where(blk_start < pad_ends[-1], jnp.clip(counts[blk_e] - blk_rank, 0, MOE_ROWS), 0).astype(jnp.int32)
    r = jnp.arange(MOE_ROWS, dtype=jnp.int32)[None, :]
    src = jnp.clip((starts[blk_e] + blk_rank)[:, None] + r, 0, n_flat - 1)
    rows = jnp.where(r < nvalid[:, None], order[src], 0).astype(jnp.int32)
    rows = jnp.pad(rows, ((0, 0), (0, IDX_STRIDE - MOE_ROWS))).reshape(-1)
    out_rows = n_flat * SUBLANES
    wspec = pl.BlockSpec((1, D_MODEL, D_MODEL), lambda i, be, nv, nr: (be[i], 0, 0))
    bspec = pl.BlockSpec((1, 1, D_MODEL), lambda i, be, nv, nr: (be[i], 0, 0))
    any_spec = pl.BlockSpec(memory_space=pl.ANY)
    grid_spec = pltpu.PrefetchScalarGridSpec(
        num_scalar_prefetch=3,
        grid=(n_blocks,),
        in_specs=[any_spec, any_spec, wspec, bspec, wspec, bspec, wspec, bspec],
        out_specs=any_spec,
        scratch_shapes=[pltpu.SMEM((IDX_SLOTS * IDX_STRIDE,), jnp.int32),
                        pltpu.VMEM((2, MOE_ROWS * SUBLANES, LANES), F32),
                        pltpu.VMEM((2, MOE_ROWS * SUBLANES, LANES), F32),
                        pltpu.VMEM((D_MODEL, D_MODEL), BF), pltpu.VMEM((D_MODEL, D_MODEL), BF),
                        pltpu.VMEM((D_MODEL, D_MODEL), BF),
                        pltpu.SemaphoreType.DMA((IDX_SLOTS,)), pltpu.SemaphoreType.DMA((2,)),
                        pltpu.SemaphoreType.DMA((2,))])
    return pl.pallas_call(
        _moe_kernel,
        grid_spec=grid_spec,
        out_shape=jax.ShapeDtypeStruct((out_rows, LANES), F32),
        compiler_params=_params(56, 1, disable_bounds_checks=True),
    )(blk_e, nvalid, nreal, rows, h_rows, w["wgate"], w["bgate"], w["wup"], w["bup"], w["wdown"], w["bdown"])


def _combine_kernel(h_ref, y_ref, gate_ref, g2_ref, b2_ref, o_ref, *, tm):
    g = gate_ref[...]
    cols = []
    for j in range(D_MODEL // LANES):
        f = None
        for kk in range(TOP_K):
            term = g[:, kk:kk + 1] * y_ref[pl.ds(kk * SUBLANES + j, tm, stride=TOP_K * SUBLANES), :]
            f = term if f is None else f + term
        cols.append(DEEPNORM_ALPHA * h_ref[pl.ds(j, tm, stride=SUBLANES), :] + f)
    o_ref[...] = _layernorm(jnp.concatenate(cols, axis=1), g2_ref[...], b2_ref[...])


def _combine(h_rows, y_rows, gate, g2, b2, first_token):
    n = gate.shape[0]
    tm = min(256, n)
    first = first_token // tm
    return pl.pallas_call(
        functools.partial(_combine_kernel, tm=tm),
        grid=(n // tm,),
        in_specs=[pl.BlockSpec((tm * SUBLANES, LANES), lambda i: (i, 0)),
                  pl.BlockSpec((tm * TOP_K * SUBLANES, LANES), lambda i: (i + first, 0)),
                  pl.BlockSpec((tm, TOP_K), lambda i: (i, 0)), _full((1, D_MODEL)), _full((1, D_MODEL))],
        out_specs=pl.BlockSpec((tm, D_MODEL), lambda i: (i, 0)),
        out_shape=jax.ShapeDtypeStruct((n, D_MODEL), F32),
        compiler_params=_params(40, 1),
    )(h_rows, y_rows, gate, g2, b2)


def _row(v):
    return v.reshape(1, -1).astype(F32)


def _prep_weights(w_in, b_in, b_forget, conv_w, conv_b, conv_ln_g, conv_ln_b, w_conv_out, b_conv_out, w_fox_out,
                  w_mem_kv, w_mem_out, w_out, b_out, ln1_g, ln1_b, w_router, b_router, w_gate, b_gate, w_up, b_up,
                  w_down, b_down, ln2_g, ln2_b):
    o_q = 2 * CONV_CH
    o_f = o_q + 3 * FOX_WIDTH
    o_qm = o_f + FOX_HEADS
    o_g = o_qm + MEM_WIDTH
    pad_f = LANES - FOX_HEADS
    proj = dict(
        wglu=w_in[:, :o_q].astype(BF), bglu=_row(b_in[:o_q]),
        wqkv=w_in[:, o_q:o_f].astype(BF), bqkv=_row(b_in[o_q:o_f]),
        wf=jnp.pad(w_in[:, o_f:o_qm], ((0, 0), (0, pad_f))).astype(BF),
        bf=_row(jnp.pad(b_in[o_f:o_qm], (0, pad_f))), bfg=_row(jnp.pad(b_forget, (0, pad_f))),
        wqm=w_in[:, o_qm:o_g].astype(BF), bqm=_row(b_in[o_qm:o_g]))
    conv = dict(w=jnp.pad(conv_w, ((0, HIST_ROWS - CONV_WIDTH), (0, 0))).astype(F32), cb=_row(conv_b),
                g=_row(conv_ln_g), b=_row(conv_ln_b))
    pad_r = LANES - N_EXPERTS
    wr = jnp.pad(w_router, ((0, 0), (0, pad_r)))
    wr_hi = wr.astype(BF)
    merge = dict(
        wg=w_in[:, o_g:].astype(BF), bg=_row(b_in[o_g:]), wc=w_conv_out.astype(BF), bc=_row(b_conv_out),
        wfo=w_fox_out.astype(BF), wmo=w_mem_out.astype(BF), wo=w_out.astype(BF), bo=_row(b_out),
        g1=_row(ln1_g), b1=_row(ln1_b), wrh=wr_hi, wrl=(wr - wr_hi.astype(F32)).astype(BF),
        br=_row(jnp.pad(b_router, (0, pad_r))))
    moe = dict(wgate=w_gate, bgate=b_gate.reshape(N_EXPERTS, 1, D_MODEL), wup=w_up,
               bup=b_up.reshape(N_EXPERTS, 1, D_MODEL), wdown=w_down, bdown=b_down.reshape(N_EXPERTS, 1, D_MODEL))
    return proj, conv, merge, moe, w_mem_kv.astype(BF), _row(ln2_g), _row(ln2_b)


def _channel(groups, merge_w, moe_w, g2, b2):
    merged = [_merge(x2d, yc, yf, ym, merge_w) for x2d, yc, yf, ym in groups]
    y_rows = _moe(jnp.concatenate([m[0] for m in merged], axis=0), jnp.concatenate([m[1] for m in merged], axis=0),
                  moe_w)
    outs = []
    first = 0
    for h_rows, _, gate in merged:
        outs.append(_combine(h_rows, y_rows, gate, g2, b2, first))
        first += gate.shape[0]
    return outs


def kernel(x_prompt, x_sample, mem_prompt, cache_k, cache_v, cache_logf, page_table, cache_mem_k, cache_mem_v, state_conv, w_in, b_in, b_forget, conv_w, conv_b, conv_ln_g, conv_ln_b, w_conv_out, b_conv_out, w_fox_out, w_mem_kv, w_mem_out, w_out, b_out, ln1_g, ln1_b, w_router, b_router, w_gate, b_gate, w_up, b_up, w_down, b_down, ln2_g, ln2_b):
    proj_w, conv_w_, merge_w, moe_w, wkv, g2, b2 = _prep_weights(
        w_in, b_in, b_forget, conv_w, conv_b, conv_ln_g, conv_ln_b, w_conv_out, b_conv_out, w_fox_out, w_mem_kv,
        w_mem_out, w_out, b_out, ln1_g, ln1_b, w_router, b_router, w_gate, b_gate, w_up, b_up, w_down, b_down,
        ln2_g, ln2_b)
    b, l, d = x_prompt.shape
    db, t, _ = x_sample.shape
    hist_len = CONV_WIDTH - 1

    xp = x_prompt.reshape(b * l, d)
    u, q, k, v, kb, vb, logf, _, qm, aq, ak = _in_proj(xp, l, proj_w)
    u3 = u.reshape(b, l, CONV_CH)
    yc = _conv_prompt(u3, jnp.zeros((b, HIST_ROWS, CONV_CH), F32), conv_w_)
    mk, mv = _mem_kv(mem_prompt.reshape(b * MEM_TOKENS, d), wkv)
    mk3 = mk.reshape(b, MEM_TOKENS, MEM_WIDTH)
    mv3 = mv.reshape(b, MEM_TOKENS, MEM_WIDTH)
    ym = _mem_attend(qm.reshape(b, l, MEM_WIDTH), mk3, mv3, 1, min(512, l))

    xs = x_sample.reshape(db * t, d)
    us, qs, ks, vs, ksb, vsb, logfs, cums, qms, _, _ = _in_proj(xs, t, proj_w)
    us_ext = jnp.concatenate([state_conv.astype(F32), us.reshape(db, t, CONV_CH)], axis=1)
    ycs = _conv_sample(us_ext.transpose(1, 0, 2), conv_w_).transpose(1, 0, 2)
    n_phys = cache_logf.shape[0]
    page_gates = _page_suffix(cache_logf.transpose(0, 2, 1).reshape(n_phys * FOX_HEADS, PAGE_SIZE))

    yf, yfs = _fox_attention(
        q.reshape(b, l, FOX_WIDTH), kb.reshape(b, l, FOX_WIDTH), vb.reshape(b, l, FOX_WIDTH),
        aq.reshape(b, l, FOX_HEADS * LANES), ak.reshape(b, l, FOX_HEADS * LANES),
        qs.reshape(db, t, FOX_WIDTH), ksb.reshape(db, t, FOX_WIDTH), vsb.reshape(db, t, FOX_WIDTH),
        cums.reshape(db, t, FOX_HEADS), cache_k.transpose(0, 2, 3, 1), cache_v.transpose(0, 2, 3, 1),
        page_gates, page_table)
    t_pad = 2 * SUBLANES
    qms3 = jnp.pad(qms.reshape(db, t, MEM_WIDTH), ((0, 0), (0, t_pad - t), (0, 0)))
    yms = _mem_attend(qms3, cache_mem_k.reshape(db, MEM_TOKENS * MEM_HEADS, MEM_HEAD_DIM),
                      cache_mem_v.reshape(db, MEM_TOKENS * MEM_HEADS, MEM_HEAD_DIM), 8, t_pad)[:, :t]

    y_prompt, y_sample = _channel(
        [(xp, yc.reshape(b * l, CONV_CH), yf.reshape(b * l, FOX_WIDTH), ym.reshape(b * l, MEM_WIDTH)),
         (xs, ycs.reshape(db * t, CONV_CH), yfs.reshape(db * t, FOX_WIDTH), yms.reshape(db * t, MEM_WIDTH))],
        merge_w, moe_w, g2, b2)
    y_prompt = y_prompt.reshape(b, l, d)
    y_sample = y_sample.reshape(db, t, d)

    heads = lambda a, n, s: a.reshape(n, s, FOX_HEADS, FOX_HEAD_DIM)
    return (y_prompt, y_sample,
            heads(k, b, l), heads(v, b, l), logf.reshape(b, l, FOX_HEADS),
            mk.reshape(b, MEM_TOKENS, MEM_HEADS, MEM_HEAD_DIM), mv.reshape(b, MEM_TOKENS, MEM_HEADS, MEM_HEAD_DIM),
            u3[:, l - hist_len:, :],
            heads(ks, db, t), heads(vs, db, t), logfs.reshape(db, t, FOX_HEADS),
            us_ext[:, t:, :])
```

```python
import functools

import numpy as np
import jax
import jax.numpy as jnp
from jax import lax
from jax.experimental import pallas as pl
from jax.experimental.pallas import tpu as pltpu

D_MODEL = 1024
CONV_CH = 512
CONV_WIDTH = 31
FOX_HEADS = 8
FOX_HEAD_DIM = 64
FOX_WIDTH = FOX_HEADS * FOX_HEAD_DIM
MEM_HEADS = 4
MEM_HEAD_DIM = 128
MEM_WIDTH = MEM_HEADS * MEM_HEAD_DIM
MEM_TOKENS = 256
N_EXPERTS = 32
TOP_K = 4
PAGE_SIZE = 128
SWIGLU_LIMIT = 7.0
SWIGLU_ALPHA = 1.702
LN_EPS = 1e-5
DEEPNORM_ALPHA = 2.0 ** 0.25

LANES = 128
SUBLANES = 8
HIST_ROWS = 32
MIB = 1024 * 1024

BF = jnp.bfloat16
F32 = jnp.float32
NEG_INF = float("-inf")


def _dot(a, b):
    return jnp.dot(a, b, preferred_element_type=F32)


def _dot_nt(a, b):
    return lax.dot_general(a, b, (((1,), (1,)), ((), ())), preferred_element_type=F32)


def _params(vmem_mib, n_axes, **kw):
    return pltpu.CompilerParams(dimension_semantics=("arbitrary",) * n_axes,
                                vmem_limit_bytes=vmem_mib * MIB, **kw)


def _full(shape):
    nd = len(shape)
    return pl.BlockSpec(shape, lambda *_: (0,) * nd)


def _split3(x):
    hi = x.astype(BF)
    r1 = x - hi.astype(F32)
    mid = r1.astype(BF)
    lo = (r1 - mid.astype(F32)).astype(BF)
    return hi, mid, lo


def _layernorm(x, g, b):
    mu = jnp.mean(x, axis=-1, keepdims=True)
    xc = x - mu
    var = jnp.mean(xc * xc, axis=-1, keepdims=True)
    return xc * lax.rsqrt(var + LN_EPS) * g + b


def _in_proj_kernel(x_ref, tri_ref, wglu_ref, wqkv_ref, wf_ref, wqm_ref, bglu_ref, bqkv_ref, bf_ref, bfg_ref,
                    bqm_ref, pq_ref, pk_ref, cq_ref, ck_ref, u_ref, q_ref, k_ref, v_ref, kb_ref, vb_ref, logf_ref,
                    cum_ref, qm_ref, aq_ref, ak_ref, carry_ref, *, tm, seq_len):
    i = pl.program_id(0)
    xb = x_ref[...].astype(BF)
    glu = _dot(xb, wglu_ref[...]) + bglu_ref[...]
    u_ref[...] = glu[:, :CONV_CH] * jax.nn.sigmoid(glu[:, CONV_CH:])
    qkv = _dot(xb, wqkv_ref[...]) + bqkv_ref[...]
    q_ref[...] = (qkv[:, :FOX_WIDTH] * (FOX_HEAD_DIM ** -0.5)).astype(BF)
    k = qkv[:, FOX_WIDTH:2 * FOX_WIDTH]
    v = qkv[:, 2 * FOX_WIDTH:]
    if k_ref.shape == k.shape:
        k_ref[...] = k
        v_ref[...] = v
    else:
        k_ref[0] = k.T
        v_ref[0] = v.T
    kb_ref[...] = k.astype(BF)
    vb_ref[...] = v.astype(BF)
    qm_ref[...] = (_dot(xb, wqm_ref[...]) + bqm_ref[...]).astype(BF)
    f = (_dot(xb, wf_ref[...]) + bf_ref[...]) + bfg_ref[...]
    lf = jnp.minimum(f, 0.0) - jnp.log1p(jnp.exp(-jnp.abs(f)))
    logf_ref[...] = lf[:, :FOX_HEADS]
    hi, mid, lo = _split3(lf)
    tri = tri_ref[...]
    cum = _dot(tri, hi) + _dot(tri, mid) + _dot(tri, lo)
    if seq_len > tm:
        @pl.when(i % (seq_len // tm) == 0)
        def _():
            carry_ref[...] = jnp.zeros_like(carry_ref)
        cum = cum + carry_ref[...]
        carry_ref[...] = cum[tm - 1:tm, :]
    cum_ref[...] = cum[:, :FOX_HEADS]
    parts = _split3(cum)
    aq = cq_ref[...] + _dot(parts[0], pq_ref[0]) + _dot(parts[1], pq_ref[1]) + _dot(parts[2], pq_ref[2])
    ak = ck_ref[...] + _dot(parts[0], pk_ref[0]) + _dot(parts[1], pk_ref[1]) + _dot(parts[2], pk_ref[2])
    aq_ref[...] = aq.astype(BF)
    ak_ref[...] = ak.astype(BF)


def _aug_lane(h):
    return h * LANES + (FOX_HEAD_DIM if h % 2 == 0 else 0)


def _aug_constants():
    pq = np.zeros((3, LANES, FOX_HEADS * LANES), np.float32)
    pk = np.zeros((3, LANES, FOX_HEADS * LANES), np.float32)
    cq = np.zeros((1, FOX_HEADS * LANES), np.float32)
    ck = np.zeros((1, FOX_HEADS * LANES), np.float32)
    for h in range(FOX_HEADS):
        base = _aug_lane(h)
        for j in range(3):
            pq[j, h, base + j] = 1.0
            pk[j, h, base + 3 + j] = -1.0
            cq[0, base + 3 + j] = 1.0
            ck[0, base + j] = 1.0
    return jnp.asarray(pq, BF), jnp.asarray(pk, BF), jnp.asarray(cq), jnp.asarray(ck)


def _in_proj(x2d, seq_len, w):
    n = x2d.shape[0]
    tm = min(512, n)
    lc = min(seq_len, tm)
    r = np.arange(tm)
    tri = jnp.asarray(((r[None, :] <= r[:, None]) & (r[None, :] // lc == r[:, None] // lc)).astype(np.float32), BF)
    tok = lambda width: pl.BlockSpec((tm, width), lambda i: (i, 0))
    if seq_len % tm == 0:
        per_seq = seq_len // tm
        kv_shape = jax.ShapeDtypeStruct((n // seq_len, FOX_WIDTH, seq_len), F32)
        kv_spec = pl.BlockSpec((1, FOX_WIDTH, tm), lambda i: (i // per_seq, 0, i % per_seq))
    else:
        kv_shape = jax.ShapeDtypeStruct((n, FOX_WIDTH), F32)
        kv_spec = tok(FOX_WIDTH)
    out_shape = (jax.ShapeDtypeStruct((n, CONV_CH), F32), jax.ShapeDtypeStruct((n, FOX_WIDTH), BF),
                 kv_shape, kv_shape,
                 jax.ShapeDtypeStruct((n, FOX_WIDTH), BF), jax.ShapeDtypeStruct((n, FOX_WIDTH), BF),
                 jax.ShapeDtypeStruct((n, FOX_HEADS), F32), jax.ShapeDtypeStruct((n, FOX_HEADS), F32),
                 jax.ShapeDtypeStruct((n, MEM_WIDTH), BF),
                 jax.ShapeDtypeStruct((n, FOX_HEADS * LANES), BF), jax.ShapeDtypeStruct((n, FOX_HEADS * LANES), BF))
    ins = (x2d, tri, w["wglu"], w["wqkv"], w["wf"], w["wqm"], w["bglu"], w["bqkv"], w["bf"], w["bfg"], w["bqm"],
           *_aug_constants())
    return pl.pallas_call(
        functools.partial(_in_proj_kernel, tm=tm, seq_len=seq_len),
        grid=(n // tm,),
        in_specs=[tok(D_MODEL)] + [_full(a.shape) for a in ins[1:]],
        out_specs=(tok(CONV_CH), tok(FOX_WIDTH), kv_spec, kv_spec, tok(FOX_WIDTH), tok(FOX_WIDTH),
                   tok(FOX_HEADS), tok(FOX_HEADS), tok(MEM_WIDTH), tok(FOX_HEADS * LANES), tok(FOX_HEADS * LANES)),
        out_shape=out_shape,
        scratch_shapes=[pltpu.VMEM((1, LANES), F32)],
        compiler_params=_params(56, 1),
    )(*ins)


def _conv_post(y, cb_ref, g_ref, b_ref):
    y = _layernorm(y + cb_ref[...], g_ref[...], b_ref[...])
    return (y * jax.nn.sigmoid(y)).astype(BF)


def _conv_prompt_kernel(u_ref, prev_ref, hist_ref, w_ref, cb_ref, g_ref, b_ref, o_ref, win_ref, y_ref, slab_ref,
                        *, tm):
    i = pl.program_id(1)
    win_ref[0:HIST_ROWS, :] = jnp.where(i == 0, hist_ref[0], prev_ref[0])
    win_ref[HIST_ROWS:, :] = u_ref[0]
    first = HIST_ROWS - (CONV_WIDTH - 1)
    for c in range(CONV_CH // LANES):
        cs = slice(c * LANES, (c + 1) * LANES)
        rows = tm + HIST_ROWS - SUBLANES
        for shift in range(1, SUBLANES):
            slab_ref[shift, 0:rows, :] = win_ref[shift:shift + rows, cs]
        acc = jnp.zeros((tm, LANES), F32)
        for j in range(CONV_WIDTH):
            shift = (first + j) % SUBLANES
            base = first + j - shift
            src = win_ref[base:base + tm, cs] if shift == 0 else slab_ref[shift, base:base + tm, :]
            acc = acc + w_ref[j:j + 1, cs] * src
        y_ref[:, cs] = acc
    o_ref[0] = _conv_post(y_ref[...], cb_ref, g_ref, b_ref)


def _conv_prompt(u3, hist, cw):
    b, l, _ = u3.shape
    tm = 256
    per = tm // HIST_ROWS
    vec = _full((1, CONV_CH))
    return pl.pallas_call(
        functools.partial(_conv_prompt_kernel, tm=tm),
        grid=(b, l // tm),
        in_specs=[pl.BlockSpec((1, tm, CONV_CH), lambda bi, i: (bi, i, 0)),
                  pl.BlockSpec((1, HIST_ROWS, CONV_CH), lambda bi, i: (bi, jnp.maximum(i * per - 1, 0), 0)),
                  pl.BlockSpec((1, HIST_ROWS, CONV_CH), lambda bi, i: (bi, 0, 0)),
                  _full((HIST_ROWS, CONV_CH)), vec, vec, vec],
        out_specs=pl.BlockSpec((1, tm, CONV_CH), lambda bi, i: (bi, i, 0)),
        out_shape=jax.ShapeDtypeStruct((b, l, CONV_CH), BF),
        scratch_shapes=[pltpu.VMEM((tm + HIST_ROWS, CONV_CH), F32), pltpu.VMEM((tm, CONV_CH), F32),
                        pltpu.VMEM((SUBLANES, tm + HIST_ROWS, LANES), F32)],
        compiler_params=_params(32, 2),
    )(u3, u3, hist, cw["w"], cw["cb"], cw["g"], cw["b"])


def _conv_sample_kernel(x_ref, w_ref, cb_ref, g_ref, b_ref, o_ref, *, steps):
    for t in range(steps):
        acc = jnp.zeros(x_ref.shape[1:], F32)
        for j in range(CONV_WIDTH):
            acc = acc + w_ref[j:j + 1, :] * x_ref[t + j]
        o_ref[t] = _conv_post(acc, cb_ref, g_ref, b_ref)


def _conv_sample(u_ext_t, cw):
    rows, b, _ = u_ext_t.shape
    steps = rows - (CONV_WIDTH - 1)
    bb = min(64, b)
    vec = _full((1, CONV_CH))
    return pl.pallas_call(
        functools.partial(_conv_sample_kernel, steps=steps),
        grid=(b // bb,),
        in_specs=[pl.BlockSpec((rows, bb, CONV_CH), lambda i: (0, i, 0)), _full((HIST_ROWS, CONV_CH)), vec, vec, vec],
        out_specs=pl.BlockSpec((steps, bb, CONV_CH), lambda i: (0, i, 0)),
        out_shape=jax.ShapeDtypeStruct((steps, b, CONV_CH), BF),
        compiler_params=_params(32, 1),
    )(u_ext_t, cw["w"], cw["cb"], cw["g"], cw["b"])


def _prompt_attention(qi_ref, kj_ref, q_ref, k_ref, v_ref, aq_ref, ak_ref, o_ref, qa_ref, ka_ref, va_ref, m_ref,
                      acc_ref, *, tile, sub, also):
    t = pl.program_id(2)
    qi = qi_ref[t]
    kj = kj_ref[t]
    nsub = tile // sub
    lane = lax.broadcasted_iota(jnp.int32, (1, LANES), 1)
    in_head = [(lane >= hh * FOX_HEAD_DIM) & (lane < (hh + 1) * FOX_HEAD_DIM) for hh in range(2)]
    sum_lane = [_aug_lane(hh) % LANES for hh in range(2)]

    for hh in range(2):
        ka_ref[hh] = jnp.where(in_head[hh], k_ref[0], ak_ref[0, :, hh * LANES:(hh + 1) * LANES])
        va_ref[hh] = jnp.where(in_head[hh], v_ref[0], jnp.where(lane == sum_lane[hh], 1.0, 0.0).astype(BF))

    @pl.when(kj == 0)
    def _():
        m_ref[...] = jnp.full_like(m_ref, NEG_INF)
        acc_ref[...] = jnp.zeros_like(acc_ref)
        for hh in range(2):
            qa_ref[hh] = jnp.where(in_head[hh], q_ref[0], aq_ref[0, :, hh * LANES:(hh + 1) * LANES])

    def attend(hh, i2, j2, masked):
        rows = slice(i2 * sub, (i2 + 1) * sub)
        cols = slice(j2 * sub, (j2 + 1) * sub)
        s = _dot_nt(qa_ref[hh, rows, :], ka_ref[hh, cols, :])
        if masked:
            r_id = lax.broadcasted_iota(jnp.int32, (sub, sub), 0)
            c_id = lax.broadcasted_iota(jnp.int32, (sub, sub), 1)
            s = jnp.where(c_id <= r_id, s, NEG_INF)
        m_prev = m_ref[hh, rows, :]
        m_next = jnp.maximum(m_prev, jnp.max(s, axis=1, keepdims=True))
        alpha = jnp.exp(m_prev - m_next)
        p = jnp.exp(s - jnp.concatenate([m_next] * (sub // LANES), axis=1))
        acc_ref[hh, rows, :] = alpha * acc_ref[hh, rows, :] + _dot(p.astype(BF), va_ref[hh, cols, :])
        m_ref[hh, rows, :] = m_next

    @pl.when(kj < qi)
    def _():
        for j2 in range(nsub):
            for i2 in range(nsub):
                for hh in range(2):
                    attend(hh, i2, j2, False)
        also()

    @pl.when(kj == qi)
    def _():
        for j2 in range(nsub):
            for i2 in range(j2, nsub):
                for hh in range(2):
                    attend(hh, i2, j2, j2 == i2)
        also()
        outs = []
        for hh in range(2):
            acc = acc_ref[hh]
            outs.append(acc / acc[:, sum_lane[hh]:sum_lane[hh] + 1])
        o_ref[0] = jnp.where(in_head[0], outs[0], outs[1]).astype(BF)


def _page_suffix_kernel(x_ref, upper_ref, ones_ref, o_ref):
    hi, mid, lo = _split3(x_ref[...])
    up = upper_ref[...]
    on = ones_ref[...]
    n = o_ref.shape[0]
    o_ref[:, :FOX_HEADS, :] = (_dot(hi, up) + _dot(mid, up) + _dot(lo, up)).reshape(n, FOX_HEADS, PAGE_SIZE)
    o_ref[:, FOX_HEADS:, :] = (_dot(hi, on) + _dot(mid, on) + _dot(lo, on)).reshape(n, FOX_HEADS, PAGE_SIZE)


def _page_suffix(logf_t):
    rows = logf_t.shape[0]
    tr = min(2048, rows)
    kk = np.arange(PAGE_SIZE)
    upper = jnp.asarray((kk[:, None] > kk[None, :]).astype(np.float32), BF)
    ones = jnp.ones((PAGE_SIZE, PAGE_SIZE), BF)
    return pl.pallas_call(
        _page_suffix_kernel,
        grid=(rows // tr,),
        in_specs=[pl.BlockSpec((tr, PAGE_SIZE), lambda i: (i, 0)), _full((PAGE_SIZE, PAGE_SIZE)),
                  _full((PAGE_SIZE, PAGE_SIZE))],
        out_specs=pl.BlockSpec((tr // FOX_HEADS, 2 * FOX_HEADS, PAGE_SIZE), lambda i: (i, 0, 0)),
        out_shape=jax.ShapeDtypeStruct((rows // FOX_HEADS, 2 * FOX_HEADS, PAGE_SIZE), F32),
        compiler_params=_params(32, 1),
    )(logf_t, upper, ones)


def _sample_attention(j, is_last, qrep_ref, cq_ref, ct_ref, kn_ref, vn_ref, k_refs, v_refs, gate_refs,
                      o_ref, m_ref, l_ref, acc_ref, carry_ref, *, pages, steps):
    rows = steps * FOX_HEADS
    row_id = lax.broadcasted_iota(jnp.int32, (rows, FOX_WIDTH), 0)
    col_id = lax.broadcasted_iota(jnp.int32, (rows, FOX_WIDTH), 1)
    head_mask = (col_id // FOX_HEAD_DIM) == (row_id % FOX_HEADS)
    qrep = qrep_ref[0]
    qe = jnp.where(head_mask, qrep, jnp.zeros_like(qrep))
    cq = cq_ref[0]

    def update(s, pv_fn):
        m_prev = m_ref[...]
        m_next = jnp.maximum(m_prev, jnp.max(s, axis=1, keepdims=True))
        alpha = jnp.exp(m_prev - m_next)
        p = jnp.exp(s - jnp.concatenate([m_next] * (s.shape[1] // LANES), axis=1))
        l_ref[...] = alpha * l_ref[...] + jnp.sum(p, axis=1, keepdims=True)
        acc_ref[...] = jnp.concatenate([alpha] * (FOX_WIDTH // LANES), axis=1) * acc_ref[...] + pv_fn(p.astype(BF))
        m_ref[...] = m_next

    @pl.when(j == 0)
    def _():
        m_ref[...] = jnp.full_like(m_ref, NEG_INF)
        l_ref[...] = jnp.zeros_like(l_ref)
        acc_ref[...] = jnp.zeros_like(acc_ref)
        carry_ref[...] = jnp.zeros_like(carry_ref)
        s = _dot_nt(qe, kn_ref[0]) + (cq - jnp.concatenate([ct_ref[0]] * steps, axis=0))
        r_id = lax.broadcasted_iota(jnp.int32, (rows, LANES), 0)
        c_id = lax.broadcasted_iota(jnp.int32, (rows, LANES), 1)
        s = jnp.where(c_id <= r_id // FOX_HEADS, s, NEG_INF)
        update(s, lambda p: _dot(p, vn_ref[0]))

    def main():
        carry = carry_ref[...]
        scores = [None] * pages
        for r in reversed(range(pages)):
            bias = cq + carry + jnp.concatenate([gate_refs[r][0, :FOX_HEADS, :]] * steps, axis=0)
            scores[r] = _dot(qe, k_refs[r][0].reshape(FOX_WIDTH, PAGE_SIZE).astype(BF)) + bias
            carry = carry + jnp.concatenate([gate_refs[r][0, FOX_HEADS:, :]] * steps, axis=0)
        carry_ref[...] = carry

        def pv_pages(p):
            out = None
            for r in range(pages):
                term = _dot_nt(p[:, r * PAGE_SIZE:(r + 1) * PAGE_SIZE],
                               v_refs[r][0].reshape(FOX_WIDTH, PAGE_SIZE).astype(BF))
                out = term if out is None else out + term
            return out

        update(jnp.concatenate(scores, axis=1), pv_pages)

    def finish():
        @pl.when(is_last)
        def _():
            o = jnp.where(head_mask, acc_ref[...] / jnp.concatenate([l_ref[...]] * (FOX_WIDTH // LANES), axis=1),
                          0.0)
            o_ref[0] = jnp.sum(o.reshape(steps, FOX_HEADS, FOX_WIDTH), axis=1).astype(BF)

    return main, finish


N_PROMPT_IN = 5
N_SAMPLE_IN = 5


def _split_step(step, n_steps):
    if n_steps & (n_steps - 1) == 0:
        return step >> (n_steps.bit_length() - 1), step & (n_steps - 1)
    return lax.div(step, n_steps), lax.rem(step, n_steps)


def _fox_kernel(qi_ref, kj_ref, pt_ref, *refs, tile, sub, pages, steps, n_steps):
    del pt_ref
    prompt_in = refs[:N_PROMPT_IN]
    sample_in = refs[N_PROMPT_IN:N_PROMPT_IN + N_SAMPLE_IN]
    page_refs = refs[N_PROMPT_IN + N_SAMPLE_IN:N_PROMPT_IN + N_SAMPLE_IN + 3 * pages]
    op_ref, os_ref = refs[N_PROMPT_IN + N_SAMPLE_IN + 3 * pages:][:2]
    scratch = refs[N_PROMPT_IN + N_SAMPLE_IN + 3 * pages + 2:]
    step = (pl.program_id(0) * pl.num_programs(1) + pl.program_id(1)) * pl.num_programs(2) + pl.program_id(2)
    _, j = _split_step(step, n_steps)
    main, finish = _sample_attention(
        j, j == n_steps - 1, *sample_in, page_refs[:pages], page_refs[pages:2 * pages], page_refs[2 * pages:],
        os_ref, *scratch[5:], pages=pages, steps=steps)
    _prompt_attention(qi_ref, kj_ref, *prompt_in, op_ref, *scratch[:5], tile=tile, sub=sub, also=main)
    finish()


def _fox_attention(q3, k3, v3, aq3, ak3, qs3, kn3, vn3, cum3, cache_kt, cache_vt, page_gates, page_table):
    b, l, _ = q3.shape
    tile = min(1024, l)
    sub = min(512, tile)
    nq = l // tile
    pairs = FOX_HEADS // 2
    qi = np.concatenate([np.full(i + 1, i) for i in range(nq)]).astype(np.int32)
    kj = np.concatenate([np.arange(i + 1) for i in range(nq)]).astype(np.int32)
    n_tiles = len(qi)

    db, steps, _ = qs3.shape
    n_pages = page_table.shape[1]
    pages = min(16, n_pages)
    n_steps = n_pages // pages
    rows = steps * FOX_HEADS
    assert b * pairs * n_tiles >= db * n_steps
    qrep = jnp.repeat(qs3, FOX_HEADS, axis=1)
    cq = jnp.broadcast_to(cum3.reshape(db, rows, 1), (db, rows, LANES))
    ct = jnp.pad(cum3.transpose(0, 2, 1), ((0, 0), (0, 0), (0, LANES - steps)))
    kn = jnp.pad(kn3, ((0, 0), (0, LANES - steps), (0, 0)))
    vn = jnp.pad(vn3, ((0, 0), (0, LANES - steps), (0, 0)))

    def step_of(bi, p, t):
        return (bi * pairs + p) * n_tiles + t

    def seq_of(bi, p, t):
        return jnp.minimum(_split_step(step_of(bi, p, t), n_steps)[0], db - 1)

    def page_map(r, nd):
        def index(bi, p, t, qi_r, kj_r, pt):
            seq, j = _split_step(step_of(bi, p, t), n_steps)
            return (pt[jnp.minimum(seq, db - 1), (n_steps - 1 - j) * pages + r],) + (0,) * (nd - 1)
        return index

    seq = lambda shape: pl.BlockSpec((1,) + shape, lambda bi, p, t, qi_r, kj_r, pt: (seq_of(bi, p, t), 0, 0))
    in_specs = [pl.BlockSpec((1, tile, LANES), lambda bi, p, t, qi_r, kj_r, pt: (bi, qi_r[t], p)),
                pl.BlockSpec((1, tile, LANES), lambda bi, p, t, qi_r, kj_r, pt: (bi, kj_r[t], p)),
                pl.BlockSpec((1, tile, LANES), lambda bi, p, t, qi_r, kj_r, pt: (bi, kj_r[t], p)),
                pl.BlockSpec((1, tile, 2 * LANES), lambda bi, p, t, qi_r, kj_r, pt: (bi, qi_r[t], p)),
                pl.BlockSpec((1, tile, 2 * LANES), lambda bi, p, t, qi_r, kj_r, pt: (bi, kj_r[t], p))]
    in_specs += [seq((rows, FOX_WIDTH)), seq((rows, LANES)), seq((FOX_HEADS, LANES)),
                 seq((LANES, FOX_WIDTH)), seq((LANES, FOX_WIDTH))]
    in_specs += [pl.BlockSpec((1, FOX_HEADS, FOX_HEAD_DIM, PAGE_SIZE), page_map(r, 4)) for r in range(pages)] * 2
    in_specs += [pl.BlockSpec((1, 2 * FOX_HEADS, PAGE_SIZE), page_map(r, 3)) for r in range(pages)]
    grid_spec = pltpu.PrefetchScalarGridSpec(
        num_scalar_prefetch=3,
        grid=(b, pairs, n_tiles),
        in_specs=in_specs,
        out_specs=(pl.BlockSpec((1, tile, LANES), lambda bi, p, t, qi_r, kj_r, pt: (bi, qi_r[t], p)),
                   seq((steps, FOX_WIDTH))),
        scratch_shapes=[pltpu.VMEM((2, tile, LANES), BF)] * 3 + [pltpu.VMEM((2, tile, LANES), F32)] * 2 +
                       [pltpu.VMEM((rows, LANES), F32), pltpu.VMEM((rows, LANES), F32),
                        pltpu.VMEM((rows, FOX_WIDTH), F32), pltpu.VMEM((rows, LANES), F32)])
    return pl.pallas_call(
        functools.partial(_fox_kernel, tile=tile, sub=sub, pages=pages, steps=steps, n_steps=n_steps),
        grid_spec=grid_spec,
        out_shape=(jax.ShapeDtypeStruct((b, l, FOX_WIDTH), BF), jax.ShapeDtypeStruct((db, steps, FOX_WIDTH), BF)),
        compiler_params=_params(48, 3),
    )(jnp.asarray(qi), jnp.asarray(kj), page_table, q3, k3, v3, aq3, ak3, qrep, cq, ct, kn, vn,
      *([cache_kt] * pages), *([cache_vt] * pages), *([page_gates] * pages))


def _mem_kv_kernel(m_ref, w_ref, k_ref, v_ref):
    kv = _dot(m_ref[...].astype(BF), w_ref[...])
    k_ref[...] = kv[:, :MEM_WIDTH]
    v_ref[...] = kv[:, MEM_WIDTH:]


def _mem_kv(mem2d, w_bf):
    n = mem2d.shape[0]
    tm = min(256, n)
    out = pl.BlockSpec((tm, MEM_WIDTH), lambda i: (i, 0))
    return pl.pallas_call(
        _mem_kv_kernel,
        grid=(n // tm,),
        in_specs=[pl.BlockSpec((tm, D_MODEL), lambda i: (i, 0)), _full(w_bf.shape)],
        out_specs=(out, out),
        out_shape=(jax.ShapeDtypeStruct((n, MEM_WIDTH), F32),) * 2,
        compiler_params=_params(32, 1),
    )(mem2d, w_bf)


def _mem_attend_kernel(q_ref, k_ref, v_ref, o_ref, *, bb, rows_by_head):
    for b in range(bb):
        for h in range(MEM_HEADS):
            hs = slice(h * MEM_HEAD_DIM, (h + 1) * MEM_HEAD_DIM)
            if rows_by_head:
                k = k_ref[b, pl.ds(h, MEM_TOKENS, stride=MEM_HEADS), :]
                v = v_ref[b, pl.ds(h, MEM_TOKENS, stride=MEM_HEADS), :]
            else:
                k = k_ref[b, :, hs]
                v = v_ref[b, :, hs]
            s = _dot_nt(q_ref[b, :, hs], k.astype(BF)) * (MEM_HEAD_DIM ** -0.5)
            p = jnp.exp(s - jnp.max(s, axis=1, keepdims=True))
            den = jnp.sum(p, axis=1, keepdims=True)
            o_ref[b, :, hs] = (_dot(p.astype(BF), v.astype(BF)) / den).astype(BF)


def _mem_attend(qm3, mk3, mv3, bb, tq):
    b, l, _ = qm3.shape
    rows_by_head = mk3.shape[2] == MEM_HEAD_DIM
    kv = pl.BlockSpec((bb,) + mk3.shape[1:], lambda bi, i: (bi, 0, 0))
    qs = pl.BlockSpec((bb, tq, MEM_WIDTH), lambda bi, i: (bi, i, 0))
    return pl.pallas_call(
        functools.partial(_mem_attend_kernel, bb=bb, rows_by_head=rows_by_head),
        grid=(b // bb, l // tq),
        in_specs=[qs, kv, kv],
        out_specs=qs,
        out_shape=jax.ShapeDtypeStruct((b, l, MEM_WIDTH), BF),
        compiler_params=_params(40, 2),
    )(qm3, mk3, mv3)


def _merge_kernel(x_ref, yc_ref, yf_ref, ym_ref, wg_ref, bg_ref, wc_ref, bc_ref, wfo_ref, wmo_ref, wo_ref, bo_ref,
                  g1_ref, b1_ref, wrh_ref, wrl_ref, br_ref, h_ref, e_ref, gate_ref, *, tm):
    x = x_ref[...]
    xb = x.astype(BF)

    def gate(c):
        cs = slice(c * D_MODEL, (c + 1) * D_MODEL)
        return jax.nn.sigmoid(_dot(xb, wg_ref[:, cs]) + bg_ref[:, cs])

    mix = gate(0) * (_dot(yc_ref[...], wc_ref[...]) + bc_ref[...])
    mix = mix + gate(1) * _dot(yf_ref[...], wfo_ref[...])
    mix = mix + gate(2) * _dot(ym_ref[...], wmo_ref[...])
    pre = DEEPNORM_ALPHA * x + (_dot(mix.astype(BF), wo_ref[...]) + bo_ref[...])
    h = _layernorm(pre, g1_ref[...], b1_ref[...])
    for j in range(D_MODEL // LANES):
        h_ref[pl.ds(j, tm, stride=SUBLANES), :] = h[:, j * LANES:(j + 1) * LANES]

    h_hi = h.astype(BF)
    h_lo = (h - h_hi.astype(F32)).astype(BF)
    logits = _dot(h_hi, wrh_ref[...]) + _dot(h_hi, wrl_ref[...]) + _dot(h_lo, wrh_ref[...]) + br_ref[...]
    lane = lax.broadcasted_iota(jnp.int32, (tm, LANES), 1)
    lane_f = lane.astype(F32)
    work = jnp.where(lane < N_EXPERTS, logits, NEG_INF)
    vals, idxs = [], []
    for _ in range(TOP_K):
        mx = jnp.max(work, axis=1, keepdims=True)
        idx = jnp.min(jnp.where(work == mx, lane_f, float(LANES)), axis=1, keepdims=True)
        vals.append(mx)
        idxs.append(idx)
        work = jnp.where(lane_f == idx, NEG_INF, work)
    exps = [jnp.exp(v - vals[0]) for v in vals]
    den = exps[0] + exps[1] + exps[2] + exps[3]
    e_out = jnp.zeros((tm, LANES), F32)
    g_out = jnp.zeros((tm, LANES), F32)
    for kk in range(TOP_K):
        e_out = jnp.where(lane == kk, idxs[kk], e_out)
        g_out = jnp.where(lane == kk, exps[kk] / den, g_out)
    e_ref[...] = e_out[:, :TOP_K].astype(jnp.int32)
    gate_ref[...] = g_out[:, :TOP_K]


def _merge(x2d, yc, yf, ym, w):
    n = x2d.shape[0]
    tm = min(256, n)
    tok = lambda width: pl.BlockSpec((tm, width), lambda i: (i, 0))
    ws = (w["wg"], w["bg"], w["wc"], w["bc"], w["wfo"], w["wmo"], w["wo"], w["bo"], w["g1"], w["b1"],
          w["wrh"], w["wrl"], w["br"])
    return pl.pallas_call(
        functools.partial(_merge_kernel, tm=tm),
        grid=(n // tm,),
        in_specs=[tok(D_MODEL), tok(CONV_CH), tok(FOX_WIDTH), tok(MEM_WIDTH)] + [_full(a.shape) for a in ws],
        out_specs=(pl.BlockSpec((tm * SUBLANES, LANES), lambda i: (i, 0)), tok(TOP_K), tok(TOP_K)),
        out_shape=(jax.ShapeDtypeStruct((n * SUBLANES, LANES), F32), jax.ShapeDtypeStruct((n, TOP_K), jnp.int32),
                   jax.ShapeDtypeStruct((n, TOP_K), F32)),
        compiler_params=_params(56, 1),
    )(x2d, yc, yf, ym, *ws)


MOE_ROWS = 256
IDX_SLOTS = 4
IDX_STRIDE = 1024
DMA_UNROLL_BITS = 3
DMA_UNROLL = 1 << DMA_UNROLL_BITS
TOP_K_BITS = TOP_K.bit_length() - 1


def _moe_kernel(blk_e_ref, nvalid_ref, nreal_ref, rows_hbm, h_hbm, wg_ref, bg_ref, wu_ref, bu_ref, wd_ref, bd_ref,
                y_hbm, idx_ref, xbuf, ybuf, wgb, wub, wdb, idx_sem, in_sem, out_sem):
    i = pl.program_id(0)
    nreal = nreal_ref[0]
    tile = SUBLANES

    def idx_slot(blk):
        return blk & (IDX_SLOTS - 1)

    def idx_copy(blk):
        slot = idx_slot(blk)
        return pltpu.make_async_copy(rows_hbm.at[pl.ds(pl.multiple_of(blk * IDX_STRIDE, IDX_STRIDE), IDX_STRIDE)],
                                     idx_ref.at[pl.ds(pl.multiple_of(slot * IDX_STRIDE, IDX_STRIDE), IDX_STRIDE)],
                                     idx_sem.at[slot])

    def row_index(islot, r):
        return idx_ref[islot * IDX_STRIDE + r]

    def gather_row(islot, slot, r):
        tok = row_index(islot, r) >> TOP_K_BITS
        return pltpu.make_async_copy(h_hbm.at[pl.ds(pl.multiple_of(tok * tile, tile), tile)],
                                     xbuf.at[slot, pl.ds(pl.multiple_of(r * tile, tile), tile)],
                                     in_sem.at[slot])

    def scatter_row(islot, slot, r):
        dst = row_index(islot, r)
        return pltpu.make_async_copy(ybuf.at[slot, pl.ds(pl.multiple_of(r * tile, tile), tile)],
                                     y_hbm.at[pl.ds(pl.multiple_of(dst * tile, tile), tile)],
                                     out_sem.at[slot])

    def start_gather(blk):
        islot = idx_slot(blk)
        slot = blk & 1

        def body(c, carry):
            for u in range(DMA_UNROLL):
                gather_row(islot, slot, c * DMA_UNROLL + u).start()
            return carry
        lax.fori_loop(0, MOE_ROWS // DMA_UNROLL, body, 0)

    def wait_rows(sem, slot):
        pltpu.make_async_copy(xbuf.at[slot], ybuf.at[slot], sem.at[slot]).wait()

    def wait_scatter(blk):
        slot = blk & 1
        nv = nvalid_ref[blk]

        @pl.when(nv == MOE_ROWS)
        def _():
            wait_rows(out_sem, slot)

        @pl.when(nv < MOE_ROWS)
        def _():
            def body(r, c):
                pltpu.make_async_copy(ybuf.at[slot, pl.ds(0, tile)], y_hbm.at[pl.ds(0, tile)],
                                      out_sem.at[slot]).wait()
                return c
            lax.fori_loop(0, nv, body, 0)

    @pl.when((i == 0) & (nreal > 0))
    def _():
        idx_copy(0).start()
        idx_copy(0).wait()
        start_gather(0)

        @pl.when(nreal > 1)
        def _():
            idx_copy(1).start()

    @pl.when(i + 1 < nreal)
    def _():
        idx_copy(i + 1).wait()
        start_gather(i + 1)

        @pl.when(i + 2 < nreal)
        def _():
            idx_copy(i + 2).start()

    changed = (i == 0) | (blk_e_ref[i] != blk_e_ref[jnp.maximum(i - 1, 0)])

    @pl.when(changed & (i < nreal))
    def _():
        wgb[...] = wg_ref[0].astype(BF)
        wub[...] = wu_ref[0].astype(BF)
        wdb[...] = wd_ref[0].astype(BF)

    @pl.when(i < nreal)
    def _():
        slot = i & 1
        islot = idx_slot(i)
        wait_rows(in_sem, slot)

        @pl.when(i >= 2)
        def _():
            wait_scatter(i - 2)

        x = jnp.concatenate([xbuf[slot, pl.ds(j, MOE_ROWS, stride=SUBLANES), :] for j in range(D_MODEL // LANES)],
                            axis=1).astype(BF)

        a = jnp.minimum(_dot(x, wgb[...]) + bg_ref[0], SWIGLU_LIMIT)
        u = jnp.clip(_dot(x, wub[...]) + bu_ref[0], -SWIGLU_LIMIT, SWIGLU_LIMIT)
        hid = (u + 1.0) * a * jax.nn.sigmoid(SWIGLU_ALPHA * a)
        y = _dot(hid.astype(BF), wdb[...]) + bd_ref[0]
        for j in range(D_MODEL // LANES):
            ybuf[slot, pl.ds(j, MOE_ROWS, stride=SUBLANES), :] = y[:, j * LANES:(j + 1) * LANES]

        nv = nvalid_ref[i]

        @pl.when(nv == MOE_ROWS)
        def _():
            def body(c, carry):
                for u in range(DMA_UNROLL):
                    scatter_row(islot, slot, c * DMA_UNROLL + u).start()
                return carry
            lax.fori_loop(0, MOE_ROWS // DMA_UNROLL, body, 0)

        @pl.when(nv < MOE_ROWS)
        def _():
            def body(c, carry):
                for u in range(DMA_UNROLL):
                    r = c * DMA_UNROLL + u

                    @pl.when(r < nv)
                    def _():
                        scatter_row(islot, slot, r).start()
                return carry
            lax.fori_loop(0, (nv + DMA_UNROLL - 1) >> DMA_UNROLL_BITS, body, 0)

        @pl.when(i == nreal - 1)
        def _():
            @pl.when(i >= 1)
            def _():
                wait_scatter(i - 1)
            wait_scatter(i)


def _moe(h_rows, top_e, w):
    n = top_e.shape[0]
    flat_e = top_e.reshape(-1)
    n_flat = n * TOP_K
    n_blocks = -(-n_flat // MOE_ROWS) + N_EXPERTS
    order = jnp.argsort(flat_e).astype(jnp.int32)
    counts = jnp.sum((flat_e[:, None] == jnp.arange(N_EXPERTS, dtype=jnp.int32)[None, :]).astype(jnp.int32), axis=0)
    starts = jnp.cumsum(counts) - counts
    padded = (counts + MOE_ROWS - 1) // MOE_ROWS * MOE_ROWS
    pad_ends = jnp.cumsum(padded)
    pad_starts = pad_ends - padded
    blk_start = jnp.arange(n_blocks, dtype=jnp.int32) * MOE_ROWS
    blk_e = jnp.minimum(jnp.sum((pad_ends[None, :] <= blk_start[:, None]).astype(jnp.int32), axis=1),
                        N_EXPERTS - 1).astype(jnp.int32)
    nreal = (pad_ends[-1] // MOE_ROWS).astype(jnp.int32).reshape(1)
    blk_rank = blk_start - pad_starts[blk_e]
    nvalid = jnp.where(blk_start < pad_ends[-1], jnp.clip(counts[blk_e] - blk_rank, 0, MOE_ROWS), 0).astype(jnp.int32)
    r = jnp.arange(MOE_ROWS, dtype=jnp.int32)[None, :]
    src = jnp.clip((starts[blk_e] + blk_rank)[:, None] + r, 0, n_flat - 1)
    rows = jnp.where(r < nvalid[:, None], order[src], 0).astype(jnp.int32)
    rows = jnp.pad(rows, ((0, 0), (0, IDX_STRIDE - MOE_ROWS))).reshape(-1)
    out_rows = n_flat * SUBLANES
    wspec = pl.BlockSpec((1, D_MODEL, D_MODEL), lambda i, be, nv, nr: (be[i], 0, 0))
    bspec = pl.BlockSpec((1, 1, D_MODEL), lambda i, be, nv, nr: (be[i], 0, 0))
    any_spec = pl.BlockSpec(memory_space=pl.ANY)
    grid_spec = pltpu.PrefetchScalarGridSpec(
        num_scalar_prefetch=3,
        grid=(n_blocks,),
        in_specs=[any_spec, any_spec, wspec, bspec, wspec, bspec, wspec, bspec],
        out_specs=any_spec,
        scratch_shapes=[pltpu.SMEM((IDX_SLOTS * IDX_STRIDE,), jnp.int32),
                        pltpu.VMEM((2, MOE_ROWS * SUBLANES, LANES), F32),
                        pltpu.VMEM((2, MOE_ROWS * SUBLANES, LANES), F32),
                        pltpu.VMEM((D_MODEL, D_MODEL), BF), pltpu.VMEM((D_MODEL, D_MODEL), BF),
                        pltpu.VMEM((D_MODEL, D_MODEL), BF),
                        pltpu.SemaphoreType.DMA((IDX_SLOTS,)), pltpu.SemaphoreType.DMA((2,)),
                        pltpu.SemaphoreType.DMA((2,))])
    return pl.pallas_call(
        _moe_kernel,
        grid_spec=grid_spec,
        out_shape=jax.ShapeDtypeStruct((out_rows, LANES), F32),
        compiler_params=_params(56, 1, disable_bounds_checks=True),
    )(blk_e, nvalid, nreal, rows, h_rows, w["wgate"], w["bgate"], w["wup"], w["bup"], w["wdown"], w["bdown"])


def _combine_kernel(h_ref, y_ref, gate_ref, g2_ref, b2_ref, o_ref, *, tm):
    g = gate_ref[...]
    cols = []
    for j in range(D_MODEL // LANES):
        f = None
        for kk in range(TOP_K):
            term = g[:, kk:kk + 1] * y_ref[pl.ds(kk * SUBLANES + j, tm, stride=TOP_K * SUBLANES), :]
            f = term if f is None else f + term
        cols.append(DEEPNORM_ALPHA * h_ref[pl.ds(j, tm, stride=SUBLANES), :] + f)
    o_ref[...] = _layernorm(jnp.concatenate(cols, axis=1), g2_ref[...], b2_ref[...])


def _combine(h_rows, y_rows, gate, g2, b2, first_token):
    n = gate.shape[0]
    tm = min(256, n)
    first = first_token // tm
    return pl.pallas_call(
        functools.partial(_combine_kernel, tm=tm),
        grid=(n // tm,),
        in_specs=[pl.BlockSpec((tm * SUBLANES, LANES), lambda i: (i, 0)),
                  pl.BlockSpec((tm * TOP_K * SUBLANES, LANES), lambda i: (i + first, 0)),
                  pl.BlockSpec((tm, TOP_K), lambda i: (i, 0)), _full((1, D_MODEL)), _full((1, D_MODEL))],
        out_specs=pl.BlockSpec((tm, D_MODEL), lambda i: (i, 0)),
        out_shape=jax.ShapeDtypeStruct((n, D_MODEL), F32),
        compiler_params=_params(40, 1),
    )(h_rows, y_rows, gate, g2, b2)


def _row(v):
    return v.reshape(1, -1).astype(F32)


def _prep_weights(w_in, b_in, b_forget, conv_w, conv_b, conv_ln_g, conv_ln_b, w_conv_out, b_conv_out, w_fox_out,
                  w_mem_kv, w_mem_out, w_out, b_out, ln1_g, ln1_b, w_router, b_router, w_gate, b_gate, w_up, b_up,
                  w_down, b_down, ln2_g, ln2_b):
    o_q = 2 * CONV_CH
    o_f = o_q + 3 * FOX_WIDTH
    o_qm = o_f + FOX_HEADS
    o_g = o_qm + MEM_WIDTH
    pad_f = LANES - FOX_HEADS
    proj = dict(
        wglu=w_in[:, :o_q].astype(BF), bglu=_row(b_in[:o_q]),
        wqkv=w_in[:, o_q:o_f].astype(BF), bqkv=_row(b_in[o_q:o_f]),
        wf=jnp.pad(w_in[:, o_f:o_qm], ((0, 0), (0, pad_f))).astype(BF),
        bf=_row(jnp.pad(b_in[o_f:o_qm], (0, pad_f))), bfg=_row(jnp.pad(b_forget, (0, pad_f))),
        wqm=w_in[:, o_qm:o_g].astype(BF), bqm=_row(b_in[o_qm:o_g]))
    conv = dict(w=jnp.pad(conv_w, ((0, HIST_ROWS - CONV_WIDTH), (0, 0))).astype(F32), cb=_row(conv_b),
                g=_row(conv_ln_g), b=_row(conv_ln_b))
    pad_r = LANES - N_EXPERTS
    wr = jnp.pad(w_router, ((0, 0), (0, pad_r)))
    wr_hi = wr.astype(BF)
    merge = dict(
        wg=w_in[:, o_g:].astype(BF), bg=_row(b_in[o_g:]), wc=w_conv_out.astype(BF), bc=_row(b_conv_out),
        wfo=w_fox_out.astype(BF), wmo=w_mem_out.astype(BF), wo=w_out.astype(BF), bo=_row(b_out),
        g1=_row(ln1_g), b1=_row(ln1_b), wrh=wr_hi, wrl=(wr - wr_hi.astype(F32)).astype(BF),
        br=_row(jnp.pad(b_router, (0, pad_r))))
    moe = dict(wgate=w_gate, bgate=b_gate.reshape(N_EXPERTS, 1, D_MODEL), wup=w_up,
               bup=b_up.reshape(N_EXPERTS, 1, D_MODEL), wdown=w_down, bdown=b_down.reshape(N_EXPERTS, 1, D_MODEL))
    return proj, conv, merge, moe, w_mem_kv.astype(BF), _row(ln2_g), _row(ln2_b)


def _channel(groups, merge_w, moe_w, g2, b2):
    merged = [_merge(x2d, yc, yf, ym, merge_w) for x2d, yc, yf, ym in groups]
    y_rows = _moe(jnp.concatenate([m[0] for m in merged], axis=0), jnp.concatenate([m[1] for m in merged], axis=0),
                  moe_w)
    outs = []
    first = 0
    for h_rows, _, gate in merged:
        outs.append(_combine(h_rows, y_rows, gate, g2, b2, first))
        first += gate.shape[0]
    return outs


def kernel(x_prompt, x_sample, mem_prompt, cache_k, cache_v, cache_logf, page_table, cache_mem_k, cache_mem_v, state_conv, w_in, b_in, b_forget, conv_w, conv_b, conv_ln_g, conv_ln_b, w_conv_out, b_conv_out, w_fox_out, w_mem_kv, w_mem_out, w_out, b_out, ln1_g, ln1_b, w_router, b_router, w_gate, b_gate, w_up, b_up, w_down, b_down, ln2_g, ln2_b):
    proj_w, conv_w_, merge_w, moe_w, wkv, g2, b2 = _prep_weights(
        w_in, b_in, b_forget, conv_w, conv_b, conv_ln_g, conv_ln_b, w_conv_out, b_conv_out, w_fox_out, w_mem_kv,
        w_mem_out, w_out, b_out, ln1_g, ln1_b, w_router, b_router, w_gate, b_gate, w_up, b_up, w_down, b_down,
        ln2_g, ln2_b)
    b, l, d = x_prompt.shape
    db, t, _ = x_sample.shape
    hist_len = CONV_WIDTH - 1

    xp = x_prompt.reshape(b * l, d)
    u, q, k, v, kb, vb, logf, _, qm, aq, ak = _in_proj(xp, l, proj_w)
    u3 = u.reshape(b, l, CONV_CH)
    yc = _conv_prompt(u3, jnp.zeros((b, HIST_ROWS, CONV_CH), F32), conv_w_)
    mk, mv = _mem_kv(mem_prompt.reshape(b * MEM_TOKENS, d), wkv)
    mk3 = mk.reshape(b, MEM_TOKENS, MEM_WIDTH)
    mv3 = mv.reshape(b, MEM_TOKENS, MEM_WIDTH)
    ym = _mem_attend(qm.reshape(b, l, MEM_WIDTH), mk3, mv3, 1, min(512, l))

    xs = x_sample.reshape(db * t, d)
    us, qs, ks, vs, ksb, vsb, logfs, cums, qms, _, _ = _in_proj(xs, t, proj_w)
    us_ext = jnp.concatenate([state_conv.astype(F32), us.reshape(db, t, CONV_CH)], axis=1)
    ycs = _conv_sample(us_ext.transpose(1, 0, 2), conv_w_).transpose(1, 0, 2)
    n_phys = cache_logf.shape[0]
    page_gates = _page_suffix(cache_logf.transpose(0, 2, 1).reshape(n_phys * FOX_HEADS, PAGE_SIZE))

    yf, yfs = _fox_attention(
        q.reshape(b, l, FOX_WIDTH), kb.reshape(b, l, FOX_WIDTH), vb.reshape(b, l, FOX_WIDTH),
        aq.reshape(b, l, FOX_HEADS * LANES), ak.reshape(b, l, FOX_HEADS * LANES),
        qs.reshape(db, t, FOX_WIDTH), ksb.reshape(db, t, FOX_WIDTH), vsb.reshape(db, t, FOX_WIDTH),
        cums.reshape(db, t, FOX_HEADS), cache_k.transpose(0, 2, 3, 1), cache_v.transpose(0, 2, 3, 1),
        page_gates, page_table)
    t_pad = 2 * SUBLANES
    qms3 = jnp.pad(qms.reshape(db, t, MEM_WIDTH), ((0, 0), (0, t_pad - t), (0, 0)))
    yms = _mem_attend(qms3, cache_mem_k.reshape(db, MEM_TOKENS * MEM_HEADS, MEM_HEAD_DIM),
                      cache_mem_v.reshape(db, MEM_TOKENS * MEM_HEADS, MEM_HEAD_DIM), 8, t_pad)[:, :t]

    y_prompt, y_sample = _channel(
        [(xp, yc.reshape(b * l, CONV_CH), yf.reshape(b * l, FOX_WIDTH), ym.reshape(b * l, MEM_WIDTH)),
         (xs, ycs.reshape(db * t, CONV_CH), yfs.reshape(db * t, FOX_WIDTH), yms.reshape(db * t, MEM_WIDTH))],
        merge_w, moe_w, g2, b2)
    y_prompt = y_prompt.reshape(b, l, d)
    y_sample = y_sample.reshape(db, t, d)

    heads = lambda a, n, s: a.reshape(n, s, FOX_HEADS, FOX_HEAD_DIM)
    heads_t = lambda a: a.reshape(b, FOX_HEADS, FOX_HEAD_DIM, l).transpose(0, 3, 1, 2)
    return (y_prompt, y_sample,
            heads_t(k), heads_t(v), logf.reshape(b, l, FOX_HEADS),
            mk.reshape(b, MEM_TOKENS, MEM_HEADS, MEM_HEAD_DIM), mv.reshape(b, MEM_TOKENS, MEM_HEADS, MEM_HEAD_DIM),
            u3[:, l - hist_len:, :],
            heads(ks, db, t), heads(vs, db, t), logfs.reshape(db, t, FOX_HEADS),
            us_ext[:, t:, :])
```

```python
import functools

import numpy as np
import jax
import jax.numpy as jnp
from jax import lax
from jax.experimental import pallas as pl
from jax.experimental.pallas import tpu as pltpu

D_MODEL = 1024
CONV_CH = 512
CONV_WIDTH = 31
FOX_HEADS = 8
FOX_HEAD_DIM = 64
FOX_WIDTH = FOX_HEADS * FOX_HEAD_DIM
MEM_HEADS = 4
MEM_HEAD_DIM = 128
MEM_WIDTH = MEM_HEADS * MEM_HEAD_DIM
MEM_TOKENS = 256
N_EXPERTS = 32
TOP_K = 4
PAGE_SIZE = 128
SWIGLU_LIMIT = 7.0
SWIGLU_ALPHA = 1.702
LN_EPS = 1e-5
DEEPNORM_ALPHA = 2.0 ** 0.25

LANES = 128
SUBLANES = 8
HIST_ROWS = 32
MIB = 1024 * 1024

BF = jnp.bfloat16
F32 = jnp.float32
NEG_INF = float("-inf")


def _dot(a, b):
    return jnp.dot(a, b, preferred_element_type=F32)


def _dot_nt(a, b):
    return lax.dot_general(a, b, (((1,), (1,)), ((), ())), preferred_element_type=F32)


def _params(vmem_mib, n_axes, **kw):
    return pltpu.CompilerParams(dimension_semantics=("arbitrary",) * n_axes,
                                vmem_limit_bytes=vmem_mib * MIB, **kw)


def _full(shape):
    nd = len(shape)
    return pl.BlockSpec(shape, lambda *_: (0,) * nd)


def _split3(x):
    hi = x.astype(BF)
    r1 = x - hi.astype(F32)
    mid = r1.astype(BF)
    lo = (r1 - mid.astype(F32)).astype(BF)
    return hi, mid, lo


def _layernorm(x, g, b):
    mu = jnp.mean(x, axis=-1, keepdims=True)
    xc = x - mu
    var = jnp.mean(xc * xc, axis=-1, keepdims=True)
    return xc * lax.rsqrt(var + LN_EPS) * g + b


def _in_proj_kernel(x_ref, tri_ref, wglu_ref, wqkv_ref, wf_ref, wqm_ref, bglu_ref, bqkv_ref, bf_ref, bfg_ref,
                    bqm_ref, pq_ref, pk_ref, cq_ref, ck_ref, u_ref, q_ref, k_ref, v_ref, kb_ref, vb_ref, logf_ref,
                    cum_ref, qm_ref, aq_ref, ak_ref, carry_ref, *, tm, seq_len):
    i = pl.program_id(0)
    xb = x_ref[...].astype(BF)
    glu = _dot(xb, wglu_ref[...]) + bglu_ref[...]
    u_ref[...] = glu[:, :CONV_CH] * jax.nn.sigmoid(glu[:, CONV_CH:])
    qkv = _dot(xb, wqkv_ref[...]) + bqkv_ref[...]
    q_ref[...] = (qkv[:, :FOX_WIDTH] * (FOX_HEAD_DIM ** -0.5)).astype(BF)
    k = qkv[:, FOX_WIDTH:2 * FOX_WIDTH]
    v = qkv[:, 2 * FOX_WIDTH:]
    if k_ref.shape == k.shape:
        k_ref[...] = k
        v_ref[...] = v
    else:
        k_ref[0] = k.T
        v_ref[0] = v.T
    kb_ref[...] = k.astype(BF)
    vb_ref[...] = v.astype(BF)
    qm_ref[...] = (_dot(xb, wqm_ref[...]) + bqm_ref[...]).astype(BF)
    f = (_dot(xb, wf_ref[...]) + bf_ref[...]) + bfg_ref[...]
    lf = jnp.minimum(f, 0.0) - jnp.log1p(jnp.exp(-jnp.abs(f)))
    logf_ref[...] = lf[:, :FOX_HEADS]
    hi, mid, lo = _split3(lf)
    tri = tri_ref[...]
    cum = _dot(tri, hi) + _dot(tri, mid) + _dot(tri, lo)
    if seq_len > tm:
        @pl.when(i % (seq_len // tm) == 0)
        def _():
            carry_ref[...] = jnp.zeros_like(carry_ref)
        cum = cum + carry_ref[...]
        carry_ref[...] = cum[tm - 1:tm, :]
    cum_ref[...] = cum[:, :FOX_HEADS]
    parts = _split3(cum)
    aq = cq_ref[...] + _dot(parts[0], pq_ref[0]) + _dot(parts[1], pq_ref[1]) + _dot(parts[2], pq_ref[2])
    ak = ck_ref[...] + _dot(parts[0], pk_ref[0]) + _dot(parts[1], pk_ref[1]) + _dot(parts[2], pk_ref[2])
    aq_ref[...] = aq.astype(BF)
    ak_ref[...] = ak.astype(BF)


def _aug_lane(h):
    return h * LANES + (FOX_HEAD_DIM if h % 2 == 0 else 0)


def _aug_constants():
    pq = np.zeros((3, LANES, FOX_HEADS * LANES), np.float32)
    pk = np.zeros((3, LANES, FOX_HEADS * LANES), np.float32)
    cq = np.zeros((1, FOX_HEADS * LANES), np.float32)
    ck = np.zeros((1, FOX_HEADS * LANES), np.float32)
    for h in range(FOX_HEADS):
        base = _aug_lane(h)
        for j in range(3):
            pq[j, h, base + j] = 1.0
            pk[j, h, base + 3 + j] = -1.0
            cq[0, base + 3 + j] = 1.0
            ck[0, base + j] = 1.0
    return jnp.asarray(pq, BF), jnp.asarray(pk, BF), jnp.asarray(cq), jnp.asarray(ck)


def _in_proj(x2d, seq_len, w):
    n = x2d.shape[0]
    tm = min(512, n)
    lc = min(seq_len, tm)
    r = np.arange(tm)
    tri = jnp.asarray(((r[None, :] <= r[:, None]) & (r[None, :] // lc == r[:, None] // lc)).astype(np.float32), BF)
    tok = lambda width: pl.BlockSpec((tm, width), lambda i: (i, 0))
    if seq_len % tm == 0:
        per_seq = seq_len // tm
        kv_shape = jax.ShapeDtypeStruct((n // seq_len, FOX_WIDTH, seq_len), F32)
        kv_spec = pl.BlockSpec((1, FOX_WIDTH, tm), lambda i: (i // per_seq, 0, i % per_seq))
    else:
        kv_shape = jax.ShapeDtypeStruct((n, FOX_WIDTH), F32)
        kv_spec = tok(FOX_WIDTH)
    out_shape = (jax.ShapeDtypeStruct((n, CONV_CH), F32), jax.ShapeDtypeStruct((n, FOX_WIDTH), BF),
                 kv_shape, kv_shape,
                 jax.ShapeDtypeStruct((n, FOX_WIDTH), BF), jax.ShapeDtypeStruct((n, FOX_WIDTH), BF),
                 jax.ShapeDtypeStruct((n, FOX_HEADS), F32), jax.ShapeDtypeStruct((n, FOX_HEADS), F32),
                 jax.ShapeDtypeStruct((n, MEM_WIDTH), BF),
                 jax.ShapeDtypeStruct((n, FOX_HEADS * LANES), BF), jax.ShapeDtypeStruct((n, FOX_HEADS * LANES), BF))
    ins = (x2d, tri, w["wglu"], w["wqkv"], w["wf"], w["wqm"], w["bglu"], w["bqkv"], w["bf"], w["bfg"], w["bqm"],
           *_aug_constants())
    return pl.pallas_call(
        functools.partial(_in_proj_kernel, tm=tm, seq_len=seq_len),
        grid=(n // tm,),
        in_specs=[tok(D_MODEL)] + [_full(a.shape) for a in ins[1:]],
        out_specs=(tok(CONV_CH), tok(FOX_WIDTH), kv_spec, kv_spec, tok(FOX_WIDTH), tok(FOX_WIDTH),
                   tok(FOX_HEADS), tok(FOX_HEADS), tok(MEM_WIDTH), tok(FOX_HEADS * LANES), tok(FOX_HEADS * LANES)),
        out_shape=out_shape,
        scratch_shapes=[pltpu.VMEM((1, LANES), F32)],
        compiler_params=_params(56, 1),
    )(*ins)


def _conv_post(y, cb_ref, g_ref, b_ref):
    y = _layernorm(y + cb_ref[...], g_ref[...], b_ref[...])
    return (y * jax.nn.sigmoid(y)).astype(BF)


def _conv_prompt_kernel(u_ref, prev_ref, hist_ref, w_ref, cb_ref, g_ref, b_ref, o_ref, win_ref, y_ref, slab_ref,
                        *, tm):
    i = pl.program_id(1)
    win_ref[0:HIST_ROWS, :] = jnp.where(i == 0, hist_ref[0], prev_ref[0])
    win_ref[HIST_ROWS:, :] = u_ref[0]
    first = HIST_ROWS - (CONV_WIDTH - 1)
    for c in range(CONV_CH // LANES):
        cs = slice(c * LANES, (c + 1) * LANES)
        rows = tm + HIST_ROWS - SUBLANES
        for shift in range(1, SUBLANES):
            slab_ref[shift, 0:rows, :] = win_ref[shift:shift + rows, cs]
        acc = jnp.zeros((tm, LANES), F32)
        for j in range(CONV_WIDTH):
            shift = (first + j) % SUBLANES
            base = first + j - shift
            src = win_ref[base:base + tm, cs] if shift == 0 else slab_ref[shift, base:base + tm, :]
            acc = acc + w_ref[j:j + 1, cs] * src
        y_ref[:, cs] = acc
    o_ref[0] = _conv_post(y_ref[...], cb_ref, g_ref, b_ref)


def _conv_prompt(u3, hist, cw):
    b, l, _ = u3.shape
    tm = 256
    per = tm // HIST_ROWS
    vec = _full((1, CONV_CH))
    return pl.pallas_call(
        functools.partial(_conv_prompt_kernel, tm=tm),
        grid=(b, l // tm),
        in_specs=[pl.BlockSpec((1, tm, CONV_CH), lambda bi, i: (bi, i, 0)),
                  pl.BlockSpec((1, HIST_ROWS, CONV_CH), lambda bi, i: (bi, jnp.maximum(i * per - 1, 0), 0)),
                  pl.BlockSpec((1, HIST_ROWS, CONV_CH), lambda bi, i: (bi, 0, 0)),
                  _full((HIST_ROWS, CONV_CH)), vec, vec, vec],
        out_specs=pl.BlockSpec((1, tm, CONV_CH), lambda bi, i: (bi, i, 0)),
        out_shape=jax.ShapeDtypeStruct((b, l, CONV_CH), BF),
        scratch_shapes=[pltpu.VMEM((tm + HIST_ROWS, CONV_CH), F32), pltpu.VMEM((tm, CONV_CH), F32),
                        pltpu.VMEM((SUBLANES, tm + HIST_ROWS, LANES), F32)],
        compiler_params=_params(32, 2),
    )(u3, u3, hist, cw["w"], cw["cb"], cw["g"], cw["b"])


def _conv_sample_kernel(x_ref, w_ref, cb_ref, g_ref, b_ref, o_ref, *, steps):
    for t in range(steps):
        acc = jnp.zeros(x_ref.shape[1:], F32)
        for j in range(CONV_WIDTH):
            acc = acc + w_ref[j:j + 1, :] * x_ref[t + j]
        o_ref[t] = _conv_post(acc, cb_ref, g_ref, b_ref)


def _conv_sample(u_ext_t, cw):
    rows, b, _ = u_ext_t.shape
    steps = rows - (CONV_WIDTH - 1)
    bb = min(64, b)
    vec = _full((1, CONV_CH))
    return pl.pallas_call(
        functools.partial(_conv_sample_kernel, steps=steps),
        grid=(b // bb,),
        in_specs=[pl.BlockSpec((rows, bb, CONV_CH), lambda i: (0, i, 0)), _full((HIST_ROWS, CONV_CH)), vec, vec, vec],
        out_specs=pl.BlockSpec((steps, bb, CONV_CH), lambda i: (0, i, 0)),
        out_shape=jax.ShapeDtypeStruct((steps, b, CONV_CH), BF),
        compiler_params=_params(32, 1),
    )(u_ext_t, cw["w"], cw["cb"], cw["g"], cw["b"])


def _prompt_attention(qi_ref, kj_ref, q_ref, k_ref, v_ref, aq_ref, ak_ref, o_ref, qa_ref, ka_ref, va_ref, m_ref,
                      acc_ref, *, tile, sub, also):
    t = pl.program_id(2)
    qi = qi_ref[t]
    kj = kj_ref[t]
    nsub = tile // sub
    lane = lax.broadcasted_iota(jnp.int32, (1, LANES), 1)
    in_head = [(lane >= hh * FOX_HEAD_DIM) & (lane < (hh + 1) * FOX_HEAD_DIM) for hh in range(2)]
    sum_lane = [_aug_lane(hh) % LANES for hh in range(2)]

    for hh in range(2):
        ka_ref[hh] = jnp.where(in_head[hh], k_ref[0], ak_ref[0, :, hh * LANES:(hh + 1) * LANES])
        va_ref[hh] = jnp.where(in_head[hh], v_ref[0], jnp.where(lane == sum_lane[hh], 1.0, 0.0).astype(BF))

    @pl.when(kj == 0)
    def _():
        m_ref[...] = jnp.full_like(m_ref, NEG_INF)
        acc_ref[...] = jnp.zeros_like(acc_ref)
        for hh in range(2):
            qa_ref[hh] = jnp.where(in_head[hh], q_ref[0], aq_ref[0, :, hh * LANES:(hh + 1) * LANES])

    def attend(hh, i2, j2, masked):
        rows = slice(i2 * sub, (i2 + 1) * sub)
        cols = slice(j2 * sub, (j2 + 1) * sub)
        s = _dot_nt(qa_ref[hh, rows, :], ka_ref[hh, cols, :])
        if masked:
            r_id = lax.broadcasted_iota(jnp.int32, (sub, sub), 0)
            c_id = lax.broadcasted_iota(jnp.int32, (sub, sub), 1)
            s = jnp.where(c_id <= r_id, s, NEG_INF)
        m_prev = m_ref[hh, rows, :]
        m_next = jnp.maximum(m_prev, jnp.max(s, axis=1, keepdims=True))
        alpha = jnp.exp(m_prev - m_next)
        p = jnp.exp(s - jnp.concatenate([m_next] * (sub // LANES), axis=1))
        acc_ref[hh, rows, :] = alpha * acc_ref[hh, rows, :] + _dot(p.astype(BF), va_ref[hh, cols, :])
        m_ref[hh, rows, :] = m_next

    @pl.when(kj < qi)
    def _():
        for j2 in range(nsub):
            for i2 in range(nsub):
                for hh in range(2):
                    attend(hh, i2, j2, False)
        also()

    @pl.when(kj == qi)
    def _():
        for j2 in range(nsub):
            for i2 in range(j2, nsub):
                for hh in range(2):
                    attend(hh, i2, j2, j2 == i2)
        also()
        outs = []
        for hh in range(2):
            acc = acc_ref[hh]
            outs.append(acc / acc[:, sum_lane[hh]:sum_lane[hh] + 1])
        o_ref[0] = jnp.where(in_head[0], outs[0], outs[1]).astype(BF)


def _page_suffix_kernel(x_ref, upper_ref, ones_ref, o_ref):
    hi, mid, lo = _split3(x_ref[...])
    up = upper_ref[...]
    on = ones_ref[...]
    n = o_ref.shape[0]
    o_ref[:, :FOX_HEADS, :] = (_dot(hi, up) + _dot(mid, up) + _dot(lo, up)).reshape(n, FOX_HEADS, PAGE_SIZE)
    o_ref[:, FOX_HEADS:, :] = (_dot(hi, on) + _dot(mid, on) + _dot(lo, on)).reshape(n, FOX_HEADS, PAGE_SIZE)


def _page_suffix(logf_t):
    rows = logf_t.shape[0]
    tr = min(2048, rows)
    kk = np.arange(PAGE_SIZE)
    upper = jnp.asarray((kk[:, None] > kk[None, :]).astype(np.float32), BF)
    ones = jnp.ones((PAGE_SIZE, PAGE_SIZE), BF)
    return pl.pallas_call(
        _page_suffix_kernel,
        grid=(rows // tr,),
        in_specs=[pl.BlockSpec((tr, PAGE_SIZE), lambda i: (i, 0)), _full((PAGE_SIZE, PAGE_SIZE)),
                  _full((PAGE_SIZE, PAGE_SIZE))],
        out_specs=pl.BlockSpec((tr // FOX_HEADS, 2 * FOX_HEADS, PAGE_SIZE), lambda i: (i, 0, 0)),
        out_shape=jax.ShapeDtypeStruct((rows // FOX_HEADS, 2 * FOX_HEADS, PAGE_SIZE), F32),
        compiler_params=_params(32, 1),
    )(logf_t, upper, ones)


def _sample_attention(j, is_last, qrep_ref, cq_ref, ct_ref, kn_ref, vn_ref, k_refs, v_refs, gate_refs,
                      o_ref, m_ref, l_ref, acc_ref, carry_ref, *, pages, steps):
    rows = steps * FOX_HEADS
    row_id = lax.broadcasted_iota(jnp.int32, (rows, FOX_WIDTH), 0)
    col_id = lax.broadcasted_iota(jnp.int32, (rows, FOX_WIDTH), 1)
    head_mask = (col_id // FOX_HEAD_DIM) == (row_id % FOX_HEADS)
    qrep = qrep_ref[0]
    qe = jnp.where(head_mask, qrep, jnp.zeros_like(qrep))
    cq = cq_ref[0]

    def update(s, pv_fn):
        m_prev = m_ref[...]
        m_next = jnp.maximum(m_prev, jnp.max(s, axis=1, keepdims=True))
        alpha = jnp.exp(m_prev - m_next)
        p = jnp.exp(s - jnp.concatenate([m_next] * (s.shape[1] // LANES), axis=1))
        l_ref[...] = alpha * l_ref[...] + jnp.sum(p, axis=1, keepdims=True)
        acc_ref[...] = jnp.concatenate([alpha] * (FOX_WIDTH // LANES), axis=1) * acc_ref[...] + pv_fn(p.astype(BF))
        m_ref[...] = m_next

    @pl.when(j == 0)
    def _():
        m_ref[...] = jnp.full_like(m_ref, NEG_INF)
        l_ref[...] = jnp.zeros_like(l_ref)
        acc_ref[...] = jnp.zeros_like(acc_ref)
        carry_ref[...] = jnp.zeros_like(carry_ref)
        s = _dot_nt(qe, kn_ref[0]) + (cq - jnp.concatenate([ct_ref[0]] * steps, axis=0))
        r_id = lax.broadcasted_iota(jnp.int32, (rows, LANES), 0)
        c_id = lax.broadcasted_iota(jnp.int32, (rows, LANES), 1)
        s = jnp.where(c_id <= r_id // FOX_HEADS, s, NEG_INF)
        update(s, lambda p: _dot(p, vn_ref[0]))

    def main():
        carry = carry_ref[...]
        scores = [None] * pages
        for r in reversed(range(pages)):
            bias = cq + carry + jnp.concatenate([gate_refs[r][0, :FOX_HEADS, :]] * steps, axis=0)
            scores[r] = _dot(qe, k_refs[r][0].reshape(FOX_WIDTH, PAGE_SIZE).astype(BF)) + bias
            carry = carry + jnp.concatenate([gate_refs[r][0, FOX_HEADS:, :]] * steps, axis=0)
        carry_ref[...] = carry

        def pv_pages(p):
            out = None
            for r in range(pages):
                term = _dot_nt(p[:, r * PAGE_SIZE:(r + 1) * PAGE_SIZE],
                               v_refs[r][0].reshape(FOX_WIDTH, PAGE_SIZE).astype(BF))
                out = term if out is None else out + term
            return out

        update(jnp.concatenate(scores, axis=1), pv_pages)

    def finish():
        @pl.when(is_last)
        def _():
            o = jnp.where(head_mask, acc_ref[...] / jnp.concatenate([l_ref[...]] * (FOX_WIDTH // LANES), axis=1),
                          0.0)
            o_ref[0] = jnp.sum(o.reshape(steps, FOX_HEADS, FOX_WIDTH), axis=1).astype(BF)

    return main, finish


N_PROMPT_IN = 5
N_SAMPLE_IN = 5


def _split_step(step, n_steps):
    if n_steps & (n_steps - 1) == 0:
        return step >> (n_steps.bit_length() - 1), step & (n_steps - 1)
    return lax.div(step, n_steps), lax.rem(step, n_steps)


def _fox_kernel(qi_ref, kj_ref, pt_ref, *refs, tile, sub, pages, steps, n_steps):
    del pt_ref
    prompt_in = refs[:N_PROMPT_IN]
    sample_in = refs[N_PROMPT_IN:N_PROMPT_IN + N_SAMPLE_IN]
    page_refs = refs[N_PROMPT_IN + N_SAMPLE_IN:N_PROMPT_IN + N_SAMPLE_IN + 3 * pages]
    op_ref, os_ref = refs[N_PROMPT_IN + N_SAMPLE_IN + 3 * pages:][:2]
    scratch = refs[N_PROMPT_IN + N_SAMPLE_IN + 3 * pages + 2:]
    step = (pl.program_id(0) * pl.num_programs(1) + pl.program_id(1)) * pl.num_programs(2) + pl.program_id(2)
    _, j = _split_step(step, n_steps)
    main, finish = _sample_attention(
        j, j == n_steps - 1, *sample_in, page_refs[:pages], page_refs[pages:2 * pages], page_refs[2 * pages:],
        os_ref, *scratch[5:], pages=pages, steps=steps)
    _prompt_attention(qi_ref, kj_ref, *prompt_in, op_ref, *scratch[:5], tile=tile, sub=sub, also=main)
    finish()


def _fox_attention(q3, k3, v3, aq3, ak3, qs3, kn3, vn3, cum3, cache_kt, cache_vt, page_gates, page_table):
    b, l, _ = q3.shape
    tile = min(1024, l)
    sub = min(512, tile)
    nq = l // tile
    pairs = FOX_HEADS // 2
    qi = np.concatenate([np.full(i + 1, i) for i in range(nq)]).astype(np.int32)
    kj = np.concatenate([np.arange(i + 1) for i in range(nq)]).astype(np.int32)
    n_tiles = len(qi)

    db, steps, _ = qs3.shape
    n_pages = page_table.shape[1]
    pages = min(16, n_pages)
    n_steps = n_pages // pages
    rows = steps * FOX_HEADS
    assert b * pairs * n_tiles >= db * n_steps
    qrep = jnp.repeat(qs3, FOX_HEADS, axis=1)
    cq = jnp.broadcast_to(cum3.reshape(db, rows, 1), (db, rows, LANES))
    ct = jnp.pad(cum3.transpose(0, 2, 1), ((0, 0), (0, 0), (0, LANES - steps)))
    kn = jnp.pad(kn3, ((0, 0), (0, LANES - steps), (0, 0)))
    vn = jnp.pad(vn3, ((0, 0), (0, LANES - steps), (0, 0)))

    def step_of(bi, p, t):
        return (bi * pairs + p) * n_tiles + t

    def seq_of(bi, p, t):
        return jnp.minimum(_split_step(step_of(bi, p, t), n_steps)[0], db - 1)

    def page_map(r, nd):
        def index(bi, p, t, qi_r, kj_r, pt):
            seq, j = _split_step(step_of(bi, p, t), n_steps)
            return (pt[jnp.minimum(seq, db - 1), (n_steps - 1 - j) * pages + r],) + (0,) * (nd - 1)
        return index

    seq = lambda shape: pl.BlockSpec((1,) + shape, lambda bi, p, t, qi_r, kj_r, pt: (seq_of(bi, p, t), 0, 0))
    in_specs = [pl.BlockSpec((1, tile, LANES), lambda bi, p, t, qi_r, kj_r, pt: (bi, qi_r[t], p)),
                pl.BlockSpec((1, tile, LANES), lambda bi, p, t, qi_r, kj_r, pt: (bi, kj_r[t], p)),
                pl.BlockSpec((1, tile, LANES), lambda bi, p, t, qi_r, kj_r, pt: (bi, kj_r[t], p)),
                pl.BlockSpec((1, tile, 2 * LANES), lambda bi, p, t, qi_r, kj_r, pt: (bi, qi_r[t], p)),
                pl.BlockSpec((1, tile, 2 * LANES), lambda bi, p, t, qi_r, kj_r, pt: (bi, kj_r[t], p))]
    in_specs += [seq((rows, FOX_WIDTH)), seq((rows, LANES)), seq((FOX_HEADS, LANES)),
                 seq((LANES, FOX_WIDTH)), seq((LANES, FOX_WIDTH))]
    in_specs += [pl.BlockSpec((1, FOX_HEADS, FOX_HEAD_DIM, PAGE_SIZE), page_map(r, 4)) for r in range(pages)] * 2
    in_specs += [pl.BlockSpec((1, 2 * FOX_HEADS, PAGE_SIZE), page_map(r, 3)) for r in range(pages)]
    grid_spec = pltpu.PrefetchScalarGridSpec(
        num_scalar_prefetch=3,
        grid=(b, pairs, n_tiles),
        in_specs=in_specs,
        out_specs=(pl.BlockSpec((1, tile, LANES), lambda bi, p, t, qi_r, kj_r, pt: (bi, qi_r[t], p)),
                   seq((steps, FOX_WIDTH))),
        scratch_shapes=[pltpu.VMEM((2, tile, LANES), BF)] * 3 + [pltpu.VMEM((2, tile, LANES), F32)] * 2 +
                       [pltpu.VMEM((rows, LANES), F32), pltpu.VMEM((rows, LANES), F32),
                        pltpu.VMEM((rows, FOX_WIDTH), F32), pltpu.VMEM((rows, LANES), F32)])
    return pl.pallas_call(
        functools.partial(_fox_kernel, tile=tile, sub=sub, pages=pages, steps=steps, n_steps=n_steps),
        grid_spec=grid_spec,
        out_shape=(jax.ShapeDtypeStruct((b, l, FOX_WIDTH), BF), jax.ShapeDtypeStruct((db, steps, FOX_WIDTH), BF)),
        compiler_params=_params(48, 3),
    )(jnp.asarray(qi), jnp.asarray(kj), page_table, q3, k3, v3, aq3, ak3, qrep, cq, ct, kn, vn,
      *([cache_kt] * pages), *([cache_vt] * pages), *([page_gates] * pages))


def _mem_kv_kernel(m_ref, w_ref, k_ref, v_ref):
    kv = _dot(m_ref[...].astype(BF), w_ref[...])
    k_ref[...] = kv[:, :MEM_WIDTH]
    v_ref[...] = kv[:, MEM_WIDTH:]


def _mem_kv(mem2d, w_bf):
    n = mem2d.shape[0]
    tm = min(256, n)
    out = pl.BlockSpec((tm, MEM_WIDTH), lambda i: (i, 0))
    return pl.pallas_call(
        _mem_kv_kernel,
        grid=(n // tm,),
        in_specs=[pl.BlockSpec((tm, D_MODEL), lambda i: (i, 0)), _full(w_bf.shape)],
        out_specs=(out, out),
        out_shape=(jax.ShapeDtypeStruct((n, MEM_WIDTH), F32),) * 2,
        compiler_params=_params(32, 1),
    )(mem2d, w_bf)


def _mem_attend_kernel(q_ref, k_ref, v_ref, o_ref, *, bb, rows_by_head):
    for b in range(bb):
        for h in range(MEM_HEADS):
            hs = slice(h * MEM_HEAD_DIM, (h + 1) * MEM_HEAD_DIM)
            if rows_by_head:
                k = k_ref[b, pl.ds(h, MEM_TOKENS, stride=MEM_HEADS), :]
                v = v_ref[b, pl.ds(h, MEM_TOKENS, stride=MEM_HEADS), :]
            else:
                k = k_ref[b, :, hs]
                v = v_ref[b, :, hs]
            s = _dot_nt(q_ref[b, :, hs], k.astype(BF)) * (MEM_HEAD_DIM ** -0.5)
            p = jnp.exp(s - jnp.max(s, axis=1, keepdims=True))
            den = jnp.sum(p, axis=1, keepdims=True)
            o_ref[b, :, hs] = (_dot(p.astype(BF), v.astype(BF)) / den).astype(BF)


def _mem_attend(qm3, mk3, mv3, bb, tq):
    b, l, _ = qm3.shape
    rows_by_head = mk3.shape[2] == MEM_HEAD_DIM
    kv = pl.BlockSpec((bb,) + mk3.shape[1:], lambda bi, i: (bi, 0, 0))
    qs = pl.BlockSpec((bb, tq, MEM_WIDTH), lambda bi, i: (bi, i, 0))
    return pl.pallas_call(
        functools.partial(_mem_attend_kernel, bb=bb, rows_by_head=rows_by_head),
        grid=(b // bb, l // tq),
        in_specs=[qs, kv, kv],
        out_specs=qs,
        out_shape=jax.ShapeDtypeStruct((b, l, MEM_WIDTH), BF),
        compiler_params=_params(40, 2),
    )(qm3, mk3, mv3)


def _merge_kernel(x_ref, yc_ref, yf_ref, ym_ref, wg_ref, bg_ref, wc_ref, bc_ref, wfo_ref, wmo_ref, wo_ref, bo_ref,
                  g1_ref, b1_ref, wrh_ref, wrl_ref, br_ref, h_ref, e_ref, gate_ref, *, tm):
    x = x_ref[...]
    xb = x.astype(BF)

    def gate(c):
        cs = slice(c * D_MODEL, (c + 1) * D_MODEL)
        return jax.nn.sigmoid(_dot(xb, wg_ref[:, cs]) + bg_ref[:, cs])

    mix = gate(0) * (_dot(yc_ref[...], wc_ref[...]) + bc_ref[...])
    mix = mix + gate(1) * _dot(yf_ref[...], wfo_ref[...])
    mix = mix + gate(2) * _dot(ym_ref[...], wmo_ref[...])
    pre = DEEPNORM_ALPHA * x + (_dot(mix.astype(BF), wo_ref[...]) + bo_ref[...])
    h = _layernorm(pre, g1_ref[...], b1_ref[...])
    for j in range(D_MODEL // LANES):
        h_ref[pl.ds(j, tm, stride=SUBLANES), :] = h[:, j * LANES:(j + 1) * LANES]

    h_hi = h.astype(BF)
    h_lo = (h - h_hi.astype(F32)).astype(BF)
    logits = _dot(h_hi, wrh_ref[...]) + _dot(h_hi, wrl_ref[...]) + _dot(h_lo, wrh_ref[...]) + br_ref[...]
    lane = lax.broadcasted_iota(jnp.int32, (tm, LANES), 1)
    lane_f = lane.astype(F32)
    work = jnp.where(lane < N_EXPERTS, logits, NEG_INF)
    vals, idxs = [], []
    for _ in range(TOP_K):
        mx = jnp.max(work, axis=1, keepdims=True)
        idx = jnp.min(jnp.where(work == mx, lane_f, float(LANES)), axis=1, keepdims=True)
        vals.append(mx)
        idxs.append(idx)
        work = jnp.where(lane_f == idx, NEG_INF, work)
    exps = [jnp.exp(v - vals[0]) for v in vals]
    den = exps[0] + exps[1] + exps[2] + exps[3]
    e_out = jnp.zeros((tm, LANES), F32)
    g_out = jnp.zeros((tm, LANES), F32)
    for kk in range(TOP_K):
        e_out = jnp.where(lane == kk, idxs[kk], e_out)
        g_out = jnp.where(lane == kk, exps[kk] / den, g_out)
    e_ref[...] = e_out[:, :TOP_K].astype(jnp.int32)
    gate_ref[...] = g_out[:, :TOP_K]


def _merge(x2d, yc, yf, ym, w):
    n = x2d.shape[0]
    tm = min(256, n)
    tok = lambda width: pl.BlockSpec((tm, width), lambda i: (i, 0))
    ws = (w["wg"], w["bg"], w["wc"], w["bc"], w["wfo"], w["wmo"], w["wo"], w["bo"], w["g1"], w["b1"],
          w["wrh"], w["wrl"], w["br"])
    return pl.pallas_call(
        functools.partial(_merge_kernel, tm=tm),
        grid=(n // tm,),
        in_specs=[tok(D_MODEL), tok(CONV_CH), tok(FOX_WIDTH), tok(MEM_WIDTH)] + [_full(a.shape) for a in ws],
        out_specs=(pl.BlockSpec((tm * SUBLANES, LANES), lambda i: (i, 0)), tok(TOP_K), tok(TOP_K)),
        out_shape=(jax.ShapeDtypeStruct((n * SUBLANES, LANES), F32), jax.ShapeDtypeStruct((n, TOP_K), jnp.int32),
                   jax.ShapeDtypeStruct((n, TOP_K), F32)),
        compiler_params=_params(56, 1),
    )(x2d, yc, yf, ym, *ws)


MOE_ROWS = 256
IDX_SLOTS = 4
IDX_STRIDE = 1024
DMA_UNROLL_BITS = 4
DMA_UNROLL = 1 << DMA_UNROLL_BITS
TOP_K_BITS = TOP_K.bit_length() - 1


def _moe_kernel(blk_e_ref, nvalid_ref, nreal_ref, rows_hbm, h_hbm, wg_ref, bg_ref, wu_ref, bu_ref, wd_ref, bd_ref,
                y_hbm, idx_ref, xbuf, ybuf, wgb, wub, wdb, idx_sem, in_sem, out_sem):
    i = pl.program_id(0)
    nreal = nreal_ref[0]
    tile = SUBLANES

    def idx_slot(blk):
        return blk & (IDX_SLOTS - 1)

    def idx_copy(blk):
        slot = idx_slot(blk)
        return pltpu.make_async_copy(rows_hbm.at[pl.ds(pl.multiple_of(blk * IDX_STRIDE, IDX_STRIDE), IDX_STRIDE)],
                                     idx_ref.at[pl.ds(pl.multiple_of(slot * IDX_STRIDE, IDX_STRIDE), IDX_STRIDE)],
                                     idx_sem.at[slot])

    def row_index(islot, r):
        return idx_ref[islot * IDX_STRIDE + r]

    def gather_row(islot, slot, r):
        tok = row_index(islot, r) >> TOP_K_BITS
        return pltpu.make_async_copy(h_hbm.at[pl.ds(pl.multiple_of(tok * tile, tile), tile)],
                                     xbuf.at[slot, pl.ds(pl.multiple_of(r * tile, tile), tile)],
                                     in_sem.at[slot])

    def scatter_row(islot, slot, r):
        dst = row_index(islot, r)
        return pltpu.make_async_copy(ybuf.at[slot, pl.ds(pl.multiple_of(r * tile, tile), tile)],
                                     y_hbm.at[pl.ds(pl.multiple_of(dst * tile, tile), tile)],
                                     out_sem.at[slot])

    def start_gather(blk):
        islot = idx_slot(blk)
        slot = blk & 1

        def body(c, carry):
            for u in range(DMA_UNROLL):
                gather_row(islot, slot, c * DMA_UNROLL + u).start()
            return carry
        lax.fori_loop(0, MOE_ROWS // DMA_UNROLL, body, 0)

    def wait_rows(sem, slot):
        pltpu.make_async_copy(xbuf.at[slot], ybuf.at[slot], sem.at[slot]).wait()

    def wait_scatter(blk):
        slot = blk & 1
        nv = nvalid_ref[blk]

        @pl.when(nv == MOE_ROWS)
        def _():
            wait_rows(out_sem, slot)

        @pl.when(nv < MOE_ROWS)
        def _():
            def body(r, c):
                pltpu.make_async_copy(ybuf.at[slot, pl.ds(0, tile)], y_hbm.at[pl.ds(0, tile)],
                                      out_sem.at[slot]).wait()
                return c
            lax.fori_loop(0, nv, body, 0)

    @pl.when((i == 0) & (nreal > 0))
    def _():
        idx_copy(0).start()
        idx_copy(0).wait()
        start_gather(0)

        @pl.when(nreal > 1)
        def _():
            idx_copy(1).start()

    @pl.when(i + 1 < nreal)
    def _():
        idx_copy(i + 1).wait()
        start_gather(i + 1)

        @pl.when(i + 2 < nreal)
        def _():
            idx_copy(i + 2).start()

    changed = (i == 0) | (blk_e_ref[i] != blk_e_ref[jnp.maximum(i - 1, 0)])

    @pl.when(changed & (i < nreal))
    def _():
        wgb[...] = wg_ref[0].astype(BF)
        wub[...] = wu_ref[0].astype(BF)
        wdb[...] = wd_ref[0].astype(BF)

    @pl.when(i < nreal)
    def _():
        slot = i & 1
        islot = idx_slot(i)
        wait_rows(in_sem, slot)

        @pl.when(i >= 2)
        def _():
            wait_scatter(i - 2)

        x = jnp.concatenate([xbuf[slot, pl.ds(j, MOE_ROWS, stride=SUBLANES), :] for j in range(D_MODEL // LANES)],
                            axis=1).astype(BF)

        a = jnp.minimum(_dot(x, wgb[...]) + bg_ref[0], SWIGLU_LIMIT)
        u = jnp.clip(_dot(x, wub[...]) + bu_ref[0], -SWIGLU_LIMIT, SWIGLU_LIMIT)
        hid = (u + 1.0) * a * jax.nn.sigmoid(SWIGLU_ALPHA * a)
        y = _dot(hid.astype(BF), wdb[...]) + bd_ref[0]
        for j in range(D_MODEL // LANES):
            ybuf[slot, pl.ds(j, MOE_ROWS, stride=SUBLANES), :] = y[:, j * LANES:(j + 1) * LANES]

        nv = nvalid_ref[i]

        @pl.when(nv == MOE_ROWS)
        def _():
            def body(c, carry):
                for u in range(DMA_UNROLL):
                    scatter_row(islot, slot, c * DMA_UNROLL + u).start()
                return carry
            lax.fori_loop(0, MOE_ROWS // DMA_UNROLL, body, 0)

        @pl.when(nv < MOE_ROWS)
        def _():
            def body(c, carry):
                for u in range(DMA_UNROLL):
                    r = c * DMA_UNROLL + u

                    @pl.when(r < nv)
                    def _():
                        scatter_row(islot, slot, r).start()
                return carry
            lax.fori_loop(0, (nv + DMA_UNROLL - 1) >> DMA_UNROLL_BITS, body, 0)

        @pl.when(i == nreal - 1)
        def _():
            @pl.when(i >= 1)
            def _():
                wait_scatter(i - 1)
            wait_scatter(i)


def _moe(h_rows, top_e, w):
    n = top_e.shape[0]
    flat_e = top_e.reshape(-1)
    n_flat = n * TOP_K
    n_blocks = -(-n_flat // MOE_ROWS) + N_EXPERTS
    order = jnp.argsort(flat_e).astype(jnp.int32)
    counts = jnp.sum((flat_e[:, None] == jnp.arange(N_EXPERTS, dtype=jnp.int32)[None, :]).astype(jnp.int32), axis=0)
    starts = jnp.cumsum(counts) - counts
    padded = (counts + MOE_ROWS - 1) // MOE_ROWS * MOE_ROWS
    pad_ends = jnp.cumsum(padded)
    pad_starts = pad_ends - padded
    blk_start = jnp.arange(n_blocks, dtype=jnp.int32) * MOE_ROWS
    blk_e = jnp.minimum(jnp.sum((pad_ends[None, :] <= blk_start[:, None]).astype(jnp.int32), axis=1),
                        N_EXPERTS - 1).astype(jnp.int32)
    nreal = (pad_ends[-1] // MOE_ROWS).astype(jnp.int32).reshape(1)
    blk_rank = blk_start - pad_starts[blk_e]
    nvalid = jnp.where(blk_start < pad_ends[-1], jnp.clip(counts[blk_e] - blk_rank, 0, MOE_ROWS), 0).astype(jnp.int32)
    r = jnp.arange(MOE_ROWS, dtype=jnp.int32)[None, :]
    src = jnp.clip((starts[blk_e] + blk_rank)[:, None] + r, 0, n_flat - 1)
    rows = jnp.where(r < nvalid[:, None], order[src], 0).astype(jnp.int32)
    rows = jnp.pad(rows, ((0, 0), (0, IDX_STRIDE - MOE_ROWS))).reshape(-1)
    out_rows = n_flat * SUBLANES
    wspec = pl.BlockSpec((1, D_MODEL, D_MODEL), lambda i, be, nv, nr: (be[i], 0, 0))
    bspec = pl.BlockSpec((1, 1, D_MODEL), lambda i, be, nv, nr: (be[i], 0, 0))
    any_spec = pl.BlockSpec(memory_space=pl.ANY)
    grid_spec = pltpu.PrefetchScalarGridSpec(
        num_scalar_prefetch=3,
        grid=(n_blocks,),
        in_specs=[any_spec, any_spec, wspec, bspec, wspec, bspec, wspec, bspec],
        out_specs=any_spec,
        scratch_shapes=[pltpu.SMEM((IDX_SLOTS * IDX_STRIDE,), jnp.int32),
                        pltpu.VMEM((2, MOE_ROWS * SUBLANES, LANES), F32),
                        pltpu.VMEM((2, MOE_ROWS * SUBLANES, LANES), F32),
                        pltpu.VMEM((D_MODEL, D_MODEL), BF), pltpu.VMEM((D_MODEL, D_MODEL), BF),
                        pltpu.VMEM((D_MODEL, D_MODEL), BF),
                        pltpu.SemaphoreType.DMA((IDX_SLOTS,)), pltpu.SemaphoreType.DMA((2,)),
                        pltpu.SemaphoreType.DMA((2,))])
    return pl.pallas_call(
        _moe_kernel,
        grid_spec=grid_spec,
        out_shape=jax.ShapeDtypeStruct((out_rows, LANES), F32),
        compiler_params=_params(56, 1, disable_bounds_checks=True),
    )(blk_e, nvalid, nreal, rows, h_rows, w["wgate"], w["bgate"], w["wup"], w["bup"], w["wdown"], w["bdown"])


def _combine_kernel(h_ref, y_ref, gate_ref, g2_ref, b2_ref, o_ref, *, tm):
    g = gate_ref[...]
    cols = []
    for j in range(D_MODEL // LANES):
        f = None
        for kk in range(TOP_K):
            term = g[:, kk:kk + 1] * y_ref[pl.ds(kk * SUBLANES + j, tm, stride=TOP_K * SUBLANES), :]
            f = term if f is None else f + term
        cols.append(DEEPNORM_ALPHA * h_ref[pl.ds(j, tm, stride=SUBLANES), :] + f)
    o_ref[...] = _layernorm(jnp.concatenate(cols, axis=1), g2_ref[...], b2_ref[...])


def _combine(h_rows, y_rows, gate, g2, b2, first_token):
    n = gate.shape[0]
    tm = min(512, n)
    first = first_token // tm
    return pl.pallas_call(
        functools.partial(_combine_kernel, tm=tm),
        grid=(n // tm,),
        in_specs=[pl.BlockSpec((tm * SUBLANES, LANES), lambda i: (i, 0)),
                  pl.BlockSpec((tm * TOP_K * SUBLANES, LANES), lambda i: (i + first, 0)),
                  pl.BlockSpec((tm, TOP_K), lambda i: (i, 0)), _full((1, D_MODEL)), _full((1, D_MODEL))],
        out_specs=pl.BlockSpec((tm, D_MODEL), lambda i: (i, 0)),
        out_shape=jax.ShapeDtypeStruct((n, D_MODEL), F32),
        compiler_params=_params(48, 1),
    )(h_rows, y_rows, gate, g2, b2)


def _row(v):
    return v.reshape(1, -1).astype(F32)


def _prep_weights(w_in, b_in, b_forget, conv_w, conv_b, conv_ln_g, conv_ln_b, w_conv_out, b_conv_out, w_fox_out,
                  w_mem_kv, w_mem_out, w_out, b_out, ln1_g, ln1_b, w_router, b_router, w_gate, b_gate, w_up, b_up,
                  w_down, b_down, ln2_g, ln2_b):
    o_q = 2 * CONV_CH
    o_f = o_q + 3 * FOX_WIDTH
    o_qm = o_f + FOX_HEADS
    o_g = o_qm + MEM_WIDTH
    pad_f = LANES - FOX_HEADS
    proj = dict(
        wglu=w_in[:, :o_q].astype(BF), bglu=_row(b_in[:o_q]),
        wqkv=w_in[:, o_q:o_f].astype(BF), bqkv=_row(b_in[o_q:o_f]),
        wf=jnp.pad(w_in[:, o_f:o_qm], ((0, 0), (0, pad_f))).astype(BF),
        bf=_row(jnp.pad(b_in[o_f:o_qm], (0, pad_f))), bfg=_row(jnp.pad(b_forget, (0, pad_f))),
        wqm=w_in[:, o_qm:o_g].astype(BF), bqm=_row(b_in[o_qm:o_g]))
    conv = dict(w=jnp.pad(conv_w, ((0, HIST_ROWS - CONV_WIDTH), (0, 0))).astype(F32), cb=_row(conv_b),
                g=_row(conv_ln_g), b=_row(conv_ln_b))
    pad_r = LANES - N_EXPERTS
    wr = jnp.pad(w_router, ((0, 0), (0, pad_r)))
    wr_hi = wr.astype(BF)
    merge = dict(
        wg=w_in[:, o_g:].astype(BF), bg=_row(b_in[o_g:]), wc=w_conv_out.astype(BF), bc=_row(b_conv_out),
        wfo=w_fox_out.astype(BF), wmo=w_mem_out.astype(BF), wo=w_out.astype(BF), bo=_row(b_out),
        g1=_row(ln1_g), b1=_row(ln1_b), wrh=wr_hi, wrl=(wr - wr_hi.astype(F32)).astype(BF),
        br=_row(jnp.pad(b_router, (0, pad_r))))
    moe = dict(wgate=w_gate, bgate=b_gate.reshape(N_EXPERTS, 1, D_MODEL), wup=w_up,
               bup=b_up.reshape(N_EXPERTS, 1, D_MODEL), wdown=w_down, bdown=b_down.reshape(N_EXPERTS, 1, D_MODEL))
    return proj, conv, merge, moe, w_mem_kv.astype(BF), _row(ln2_g), _row(ln2_b)


def _channel(groups, merge_w, moe_w, g2, b2):
    merged = [_merge(x2d, yc, yf, ym, merge_w) for x2d, yc, yf, ym in groups]
    y_rows = _moe(jnp.concatenate([m[0] for m in merged], axis=0), jnp.concatenate([m[1] for m in merged], axis=0),
                  moe_w)
    outs = []
    first = 0
    for h_rows, _, gate in merged:
        outs.append(_combine(h_rows, y_rows, gate, g2, b2, first))
        first += gate.shape[0]
    return outs


def kernel(x_prompt, x_sample, mem_prompt, cache_k, cache_v, cache_logf, page_table, cache_mem_k, cache_mem_v, state_conv, w_in, b_in, b_forget, conv_w, conv_b, conv_ln_g, conv_ln_b, w_conv_out, b_conv_out, w_fox_out, w_mem_kv, w_mem_out, w_out, b_out, ln1_g, ln1_b, w_router, b_router, w_gate, b_gate, w_up, b_up, w_down, b_down, ln2_g, ln2_b):
    proj_w, conv_w_, merge_w, moe_w, wkv, g2, b2 = _prep_weights(
        w_in, b_in, b_forget, conv_w, conv_b, conv_ln_g, conv_ln_b, w_conv_out, b_conv_out, w_fox_out, w_mem_kv,
        w_mem_out, w_out, b_out, ln1_g, ln1_b, w_router, b_router, w_gate, b_gate, w_up, b_up, w_down, b_down,
        ln2_g, ln2_b)
    b, l, d = x_prompt.shape
    db, t, _ = x_sample.shape
    hist_len = CONV_WIDTH - 1

    xp = x_prompt.reshape(b * l, d)
    u, q, k, v, kb, vb, logf, _, qm, aq, ak = _in_proj(xp, l, proj_w)
    u3 = u.reshape(b, l, CONV_CH)
    yc = _conv_prompt(u3, jnp.zeros((b, HIST_ROWS, CONV_CH), F32), conv_w_)
    mk, mv = _mem_kv(mem_prompt.reshape(b * MEM_TOKENS, d), wkv)
    mk3 = mk.reshape(b, MEM_TOKENS, MEM_WIDTH)
    mv3 = mv.reshape(b, MEM_TOKENS, MEM_WIDTH)
    ym = _mem_attend(qm.reshape(b, l, MEM_WIDTH), mk3, mv3, 1, min(512, l))

    xs = x_sample.reshape(db * t, d)
    us, qs, ks, vs, ksb, vsb, logfs, cums, qms, _, _ = _in_proj(xs, t, proj_w)
    us_ext = jnp.concatenate([state_conv.astype(F32), us.reshape(db, t, CONV_CH)], axis=1)
    ycs = _conv_sample(us_ext.transpose(1, 0, 2), conv_w_).transpose(1, 0, 2)
    n_phys = cache_logf.shape[0]
    page_gates = _page_suffix(cache_logf.transpose(0, 2, 1).reshape(n_phys * FOX_HEADS, PAGE_SIZE))

    yf, yfs = _fox_attention(
        q.reshape(b, l, FOX_WIDTH), kb.reshape(b, l, FOX_WIDTH), vb.reshape(b, l, FOX_WIDTH),
        aq.reshape(b, l, FOX_HEADS * LANES), ak.reshape(b, l, FOX_HEADS * LANES),
        qs.reshape(db, t, FOX_WIDTH), ksb.reshape(db, t, FOX_WIDTH), vsb.reshape(db, t, FOX_WIDTH),
        cums.reshape(db, t, FOX_HEADS), cache_k.transpose(0, 2, 3, 1), cache_v.transpose(0, 2, 3, 1),
        page_gates, page_table)
    t_pad = 2 * SUBLANES
    qms3 = jnp.pad(qms.reshape(db, t, MEM_WIDTH), ((0, 0), (0, t_pad - t), (0, 0)))
    yms = _mem_attend(qms3, cache_mem_k.reshape(db, MEM_TOKENS * MEM_HEADS, MEM_HEAD_DIM),
                      cache_mem_v.reshape(db, MEM_TOKENS * MEM_HEADS, MEM_HEAD_DIM), 8, t_pad)[:, :t]

    y_prompt, y_sample = _channel(
        [(xp, yc.reshape(b * l, CONV_CH), yf.reshape(b * l, FOX_WIDTH), ym.reshape(b * l, MEM_WIDTH)),
         (xs, ycs.reshape(db * t, CONV_CH), yfs.reshape(db * t, FOX_WIDTH), yms.reshape(db * t, MEM_WIDTH))],
        merge_w, moe_w, g2, b2)
    y_prompt = y_prompt.reshape(b, l, d)
    y_sample = y_sample.reshape(db, t, d)

    heads = lambda a, n, s: a.reshape(n, s, FOX_HEADS, FOX_HEAD_DIM)
    heads_t = lambda a: a.reshape(b, FOX_HEADS, FOX_HEAD_DIM, l).transpose(0, 3, 1, 2)
    return (y_prompt, y_sample,
            heads_t(k), heads_t(v), logf.reshape(b, l, FOX_HEADS),
            mk.reshape(b, MEM_TOKENS, MEM_HEADS, MEM_HEAD_DIM), mv.reshape(b, MEM_TOKENS, MEM_HEADS, MEM_HEAD_DIM),
            u3[:, l - hist_len:, :],
            heads(ks, db, t), heads(vs, db, t), logfs.reshape(db, t, FOX_HEADS),
            us_ext[:, t:, :])
```

```python
import functools

import numpy as np
import jax
import jax.numpy as jnp
from jax import lax
from jax.experimental import pallas as pl
from jax.experimental.pallas import tpu as pltpu

D_MODEL = 1024
CONV_CH = 512
CONV_WIDTH = 31
FOX_HEADS = 8
FOX_HEAD_DIM = 64
FOX_WIDTH = FOX_HEADS * FOX_HEAD_DIM
MEM_HEADS = 4
MEM_HEAD_DIM = 128
MEM_WIDTH = MEM_HEADS * MEM_HEAD_DIM
MEM_TOKENS = 256
N_EXPERTS = 32
TOP_K = 4
PAGE_SIZE = 128
SWIGLU_LIMIT = 7.0
SWIGLU_ALPHA = 1.702
LN_EPS = 1e-5
DEEPNORM_ALPHA = 2.0 ** 0.25

LANES = 128
SUBLANES = 8
HIST_ROWS = 32
MIB = 1024 * 1024

BF = jnp.bfloat16
F32 = jnp.float32
NEG_INF = float("-inf")


def _dot(a, b):
    return jnp.dot(a, b, preferred_element_type=F32)


def _dot_nt(a, b):
    return lax.dot_general(a, b, (((1,), (1,)), ((), ())), preferred_element_type=F32)


def _params(vmem_mib, n_axes, **kw):
    return pltpu.CompilerParams(dimension_semantics=("arbitrary",) * n_axes,
                                vmem_limit_bytes=vmem_mib * MIB, **kw)


def _full(shape):
    nd = len(shape)
    return pl.BlockSpec(shape, lambda *_: (0,) * nd)


def _split3(x):
    hi = x.astype(BF)
    r1 = x - hi.astype(F32)
    mid = r1.astype(BF)
    lo = (r1 - mid.astype(F32)).astype(BF)
    return hi, mid, lo


def _layernorm(x, g, b):
    mu = jnp.mean(x, axis=-1, keepdims=True)
    xc = x - mu
    var = jnp.mean(xc * xc, axis=-1, keepdims=True)
    return xc * lax.rsqrt(var + LN_EPS) * g + b


def _in_proj_kernel(x_ref, tri_ref, wglu_ref, wqkv_ref, wf_ref, wqm_ref, bglu_ref, bqkv_ref, bf_ref, bfg_ref,
                    bqm_ref, pq_ref, pk_ref, cq_ref, ck_ref, u_ref, q_ref, k_ref, v_ref, kb_ref, vb_ref, logf_ref,
                    cum_ref, qm_ref, aq_ref, ak_ref, carry_ref, *, tm, seq_len):
    i = pl.program_id(0)
    xb = x_ref[...].astype(BF)
    glu = _dot(xb, wglu_ref[...]) + bglu_ref[...]
    u_ref[...] = glu[:, :CONV_CH] * jax.nn.sigmoid(glu[:, CONV_CH:])
    qkv = _dot(xb, wqkv_ref[...]) + bqkv_ref[...]
    q_ref[...] = (qkv[:, :FOX_WIDTH] * (FOX_HEAD_DIM ** -0.5)).astype(BF)
    k = qkv[:, FOX_WIDTH:2 * FOX_WIDTH]
    v = qkv[:, 2 * FOX_WIDTH:]
    if k_ref.shape == k.shape:
        k_ref[...] = k
        v_ref[...] = v
    else:
        k_ref[0] = k.T
        v_ref[0] = v.T
    kb_ref[...] = k.astype(BF)
    vb_ref[...] = v.astype(BF)
    qm_ref[...] = (_dot(xb, wqm_ref[...]) + bqm_ref[...]).astype(BF)
    f = (_dot(xb, wf_ref[...]) + bf_ref[...]) + bfg_ref[...]
    lf = jnp.minimum(f, 0.0) - jnp.log1p(jnp.exp(-jnp.abs(f)))
    logf_ref[...] = lf[:, :FOX_HEADS]
    hi, mid, lo = _split3(lf)
    tri = tri_ref[...]
    cum = _dot(tri, hi) + _dot(tri, mid) + _dot(tri, lo)
    if seq_len > tm:
        @pl.when(i % (seq_len // tm) == 0)
        def _():
            carry_ref[...] = jnp.zeros_like(carry_ref)
        cum = cum + carry_ref[...]
        carry_ref[...] = cum[tm - 1:tm, :]
    cum_ref[...] = cum[:, :FOX_HEADS]
    parts = _split3(cum)
    aq = cq_ref[...] + _dot(parts[0], pq_ref[0]) + _dot(parts[1], pq_ref[1]) + _dot(parts[2], pq_ref[2])
    ak = ck_ref[...] + _dot(parts[0], pk_ref[0]) + _dot(parts[1], pk_ref[1]) + _dot(parts[2], pk_ref[2])
    aq_ref[...] = aq.astype(BF)
    ak_ref[...] = ak.astype(BF)


def _aug_lane(h):
    return h * LANES + (FOX_HEAD_DIM if h % 2 == 0 else 0)


def _aug_constants():
    pq = np.zeros((3, LANES, FOX_HEADS * LANES), np.float32)
    pk = np.zeros((3, LANES, FOX_HEADS * LANES), np.float32)
    cq = np.zeros((1, FOX_HEADS * LANES), np.float32)
    ck = np.zeros((1, FOX_HEADS * LANES), np.float32)
    for h in range(FOX_HEADS):
        base = _aug_lane(h)
        for j in range(3):
            pq[j, h, base + j] = 1.0
            pk[j, h, base + 3 + j] = -1.0
            cq[0, base + 3 + j] = 1.0
            ck[0, base + j] = 1.0
    return jnp.asarray(pq, BF), jnp.asarray(pk, BF), jnp.asarray(cq), jnp.asarray(ck)


def _in_proj(x2d, seq_len, w):
    n = x2d.shape[0]
    tm = min(512, n)
    lc = min(seq_len, tm)
    r = np.arange(tm)
    tri = jnp.asarray(((r[None, :] <= r[:, None]) & (r[None, :] // lc == r[:, None] // lc)).astype(np.float32), BF)
    tok = lambda width: pl.BlockSpec((tm, width), lambda i: (i, 0))
    if seq_len % tm == 0:
        per_seq = seq_len // tm
        kv_shape = jax.ShapeDtypeStruct((n // seq_len, FOX_WIDTH, seq_len), F32)
        kv_spec = pl.BlockSpec((1, FOX_WIDTH, tm), lambda i: (i // per_seq, 0, i % per_seq))
    else:
        kv_shape = jax.ShapeDtypeStruct((n, FOX_WIDTH), F32)
        kv_spec = tok(FOX_WIDTH)
    out_shape = (jax.ShapeDtypeStruct((n, CONV_CH), F32), jax.ShapeDtypeStruct((n, FOX_WIDTH), BF),
                 kv_shape, kv_shape,
                 jax.ShapeDtypeStruct((n, FOX_WIDTH), BF), jax.ShapeDtypeStruct((n, FOX_WIDTH), BF),
                 jax.ShapeDtypeStruct((n, FOX_HEADS), F32), jax.ShapeDtypeStruct((n, FOX_HEADS), F32),
                 jax.ShapeDtypeStruct((n, MEM_WIDTH), BF),
                 jax.ShapeDtypeStruct((n, FOX_HEADS * LANES), BF), jax.ShapeDtypeStruct((n, FOX_HEADS * LANES), BF))
    ins = (x2d, tri, w["wglu"], w["wqkv"], w["wf"], w["wqm"], w["bglu"], w["bqkv"], w["bf"], w["bfg"], w["bqm"],
           *_aug_constants())
    return pl.pallas_call(
        functools.partial(_in_proj_kernel, tm=tm, seq_len=seq_len),
        grid=(n // tm,),
        in_specs=[tok(D_MODEL)] + [_full(a.shape) for a in ins[1:]],
        out_specs=(tok(CONV_CH), tok(FOX_WIDTH), kv_spec, kv_spec, tok(FOX_WIDTH), tok(FOX_WIDTH),
                   tok(FOX_HEADS), tok(FOX_HEADS), tok(MEM_WIDTH), tok(FOX_HEADS * LANES), tok(FOX_HEADS * LANES)),
        out_shape=out_shape,
        scratch_shapes=[pltpu.VMEM((1, LANES), F32)],
        compiler_params=_params(56, 1),
    )(*ins)


def _conv_post(y, cb_ref, g_ref, b_ref):
    y = _layernorm(y + cb_ref[...], g_ref[...], b_ref[...])
    return (y * jax.nn.sigmoid(y)).astype(BF)


def _conv_prompt_kernel(u_ref, prev_ref, hist_ref, w_ref, cb_ref, g_ref, b_ref, o_ref, win_ref, y_ref, slab_ref,
                        *, tm):
    i = pl.program_id(1)
    win_ref[0:HIST_ROWS, :] = jnp.where(i == 0, hist_ref[0], prev_ref[0])
    win_ref[HIST_ROWS:, :] = u_ref[0]
    first = HIST_ROWS - (CONV_WIDTH - 1)
    for c in range(CONV_CH // LANES):
        cs = slice(c * LANES, (c + 1) * LANES)
        rows = tm + HIST_ROWS - SUBLANES
        for shift in range(1, SUBLANES):
            slab_ref[shift, 0:rows, :] = win_ref[shift:shift + rows, cs]
        acc = jnp.zeros((tm, LANES), F32)
        for j in range(CONV_WIDTH):
            shift = (first + j) % SUBLANES
            base = first + j - shift
            src = win_ref[base:base + tm, cs] if shift == 0 else slab_ref[shift, base:base + tm, :]
            acc = acc + w_ref[j:j + 1, cs] * src
        y_ref[:, cs] = acc
    o_ref[0] = _conv_post(y_ref[...], cb_ref, g_ref, b_ref)


def _conv_prompt(u3, hist, cw):
    b, l, _ = u3.shape
    tm = 256
    per = tm // HIST_ROWS
    vec = _full((1, CONV_CH))
    return pl.pallas_call(
        functools.partial(_conv_prompt_kernel, tm=tm),
        grid=(b, l // tm),
        in_specs=[pl.BlockSpec((1, tm, CONV_CH), lambda bi, i: (bi, i, 0)),
                  pl.BlockSpec((1, HIST_ROWS, CONV_CH), lambda bi, i: (bi, jnp.maximum(i * per - 1, 0), 0)),
                  pl.BlockSpec((1, HIST_ROWS, CONV_CH), lambda bi, i: (bi, 0, 0)),
                  _full((HIST_ROWS, CONV_CH)), vec, vec, vec],
        out_specs=pl.BlockSpec((1, tm, CONV_CH), lambda bi, i: (bi, i, 0)),
        out_shape=jax.ShapeDtypeStruct((b, l, CONV_CH), BF),
        scratch_shapes=[pltpu.VMEM((tm + HIST_ROWS, CONV_CH), F32), pltpu.VMEM((tm, CONV_CH), F32),
                        pltpu.VMEM((SUBLANES, tm + HIST_ROWS, LANES), F32)],
        compiler_params=_params(32, 2),
    )(u3, u3, hist, cw["w"], cw["cb"], cw["g"], cw["b"])


def _conv_sample_kernel(x_ref, w_ref, cb_ref, g_ref, b_ref, o_ref, *, steps):
    for t in range(steps):
        acc = jnp.zeros(x_ref.shape[1:], F32)
        for j in range(CONV_WIDTH):
            acc = acc + w_ref[j:j + 1, :] * x_ref[t + j]
        o_ref[t] = _conv_post(acc, cb_ref, g_ref, b_ref)


def _conv_sample(u_ext_t, cw):
    rows, b, _ = u_ext_t.shape
    steps = rows - (CONV_WIDTH - 1)
    bb = min(64, b)
    vec = _full((1, CONV_CH))
    return pl.pallas_call(
        functools.partial(_conv_sample_kernel, steps=steps),
        grid=(b // bb,),
        in_specs=[pl.BlockSpec((rows, bb, CONV_CH), lambda i: (0, i, 0)), _full((HIST_ROWS, CONV_CH)), vec, vec, vec],
        out_specs=pl.BlockSpec((steps, bb, CONV_CH), lambda i: (0, i, 0)),
        out_shape=jax.ShapeDtypeStruct((steps, b, CONV_CH), BF),
        compiler_params=_params(32, 1),
    )(u_ext_t, cw["w"], cw["cb"], cw["g"], cw["b"])


def _prompt_attention(qi_ref, kj_ref, q_ref, k_ref, v_ref, aq_ref, ak_ref, o_ref, qa_ref, ka_ref, va_ref, m_ref,
                      acc_ref, *, tile, sub, also):
    t = pl.program_id(2)
    qi = qi_ref[t]
    kj = kj_ref[t]
    nsub = tile // sub
    lane = lax.broadcasted_iota(jnp.int32, (1, LANES), 1)
    in_head = [(lane >= hh * FOX_HEAD_DIM) & (lane < (hh + 1) * FOX_HEAD_DIM) for hh in range(2)]
    sum_lane = [_aug_lane(hh) % LANES for hh in range(2)]

    for hh in range(2):
        ka_ref[hh] = jnp.where(in_head[hh], k_ref[0], ak_ref[0, :, hh * LANES:(hh + 1) * LANES])
        va_ref[hh] = jnp.where(in_head[hh], v_ref[0], jnp.where(lane == sum_lane[hh], 1.0, 0.0).astype(BF))

    @pl.when(kj == 0)
    def _():
        m_ref[...] = jnp.full_like(m_ref, NEG_INF)
        acc_ref[...] = jnp.zeros_like(acc_ref)
        for hh in range(2):
            qa_ref[hh] = jnp.where(in_head[hh], q_ref[0], aq_ref[0, :, hh * LANES:(hh + 1) * LANES])

    def attend(hh, i2, j2, masked):
        rows = slice(i2 * sub, (i2 + 1) * sub)
        cols = slice(j2 * sub, (j2 + 1) * sub)
        s = _dot_nt(qa_ref[hh, rows, :], ka_ref[hh, cols, :])
        if masked:
            r_id = lax.broadcasted_iota(jnp.int32, (sub, sub), 0)
            c_id = lax.broadcasted_iota(jnp.int32, (sub, sub), 1)
            s = jnp.where(c_id <= r_id, s, NEG_INF)
        m_prev = m_ref[hh, rows, :]
        m_next = jnp.maximum(m_prev, jnp.max(s, axis=1, keepdims=True))
        alpha = jnp.exp(m_prev - m_next)
        p = jnp.exp(s - jnp.concatenate([m_next] * (sub // LANES), axis=1))
        acc_ref[hh, rows, :] = alpha * acc_ref[hh, rows, :] + _dot(p.astype(BF), va_ref[hh, cols, :])
        m_ref[hh, rows, :] = m_next

    @pl.when(kj < qi)
    def _():
        for j2 in range(nsub):
            for i2 in range(nsub):
                for hh in range(2):
                    attend(hh, i2, j2, False)
        also()

    @pl.when(kj == qi)
    def _():
        for j2 in range(nsub):
            for i2 in range(j2, nsub):
                for hh in range(2):
                    attend(hh, i2, j2, j2 == i2)
        also()
        outs = []
        for hh in range(2):
            acc = acc_ref[hh]
            outs.append(acc / acc[:, sum_lane[hh]:sum_lane[hh] + 1])
        o_ref[0] = jnp.where(in_head[0], outs[0], outs[1]).astype(BF)


def _page_suffix_kernel(x_ref, upper_ref, ones_ref, o_ref):
    hi, mid, lo = _split3(x_ref[...])
    up = upper_ref[...]
    on = ones_ref[...]
    n = o_ref.shape[0]
    o_ref[:, :FOX_HEADS, :] = (_dot(hi, up) + _dot(mid, up) + _dot(lo, up)).reshape(n, FOX_HEADS, PAGE_SIZE)
    o_ref[:, FOX_HEADS:, :] = (_dot(hi, on) + _dot(mid, on) + _dot(lo, on)).reshape(n, FOX_HEADS, PAGE_SIZE)


def _page_suffix(logf_t):
    rows = logf_t.shape[0]
    tr = min(2048, rows)
    kk = np.arange(PAGE_SIZE)
    upper = jnp.asarray((kk[:, None] > kk[None, :]).astype(np.float32), BF)
    ones = jnp.ones((PAGE_SIZE, PAGE_SIZE), BF)
    return pl.pallas_call(
        _page_suffix_kernel,
        grid=(rows // tr,),
        in_specs=[pl.BlockSpec((tr, PAGE_SIZE), lambda i: (i, 0)), _full((PAGE_SIZE, PAGE_SIZE)),
                  _full((PAGE_SIZE, PAGE_SIZE))],
        out_specs=pl.BlockSpec((tr // FOX_HEADS, 2 * FOX_HEADS, PAGE_SIZE), lambda i: (i, 0, 0)),
        out_shape=jax.ShapeDtypeStruct((rows // FOX_HEADS, 2 * FOX_HEADS, PAGE_SIZE), F32),
        compiler_params=_params(32, 1),
    )(logf_t, upper, ones)


def _sample_attention(j, is_last, qrep_ref, cq_ref, ct_ref, kn_ref, vn_ref, k_refs, v_refs, gate_refs,
                      o_ref, m_ref, l_ref, acc_ref, carry_ref, *, pages, steps):
    rows = steps * FOX_HEADS
    row_id = lax.broadcasted_iota(jnp.int32, (rows, FOX_WIDTH), 0)
    col_id = lax.broadcasted_iota(jnp.int32, (rows, FOX_WIDTH), 1)
    head_mask = (col_id // FOX_HEAD_DIM) == (row_id % FOX_HEADS)
    qrep = qrep_ref[0]
    qe = jnp.where(head_mask, qrep, jnp.zeros_like(qrep))
    cq = cq_ref[0]

    def update(s, pv_fn):
        m_prev = m_ref[...]
        m_next = jnp.maximum(m_prev, jnp.max(s, axis=1, keepdims=True))
        alpha = jnp.exp(m_prev - m_next)
        p = jnp.exp(s - jnp.concatenate([m_next] * (s.shape[1] // LANES), axis=1))
        l_ref[...] = alpha * l_ref[...] + jnp.sum(p, axis=1, keepdims=True)
        acc_ref[...] = jnp.concatenate([alpha] * (FOX_WIDTH // LANES), axis=1) * acc_ref[...] + pv_fn(p.astype(BF))
        m_ref[...] = m_next

    @pl.when(j == 0)
    def _():
        m_ref[...] = jnp.full_like(m_ref, NEG_INF)
        l_ref[...] = jnp.zeros_like(l_ref)
        acc_ref[...] = jnp.zeros_like(acc_ref)
        carry_ref[...] = jnp.zeros_like(carry_ref)
        s = _dot_nt(qe, kn_ref[0]) + (cq - jnp.concatenate([ct_ref[0]] * steps, axis=0))
        r_id = lax.broadcasted_iota(jnp.int32, (rows, LANES), 0)
        c_id = lax.broadcasted_iota(jnp.int32, (rows, LANES), 1)
        s = jnp.where(c_id <= r_id // FOX_HEADS, s, NEG_INF)
        update(s, lambda p: _dot(p, vn_ref[0]))

    def main():
        carry = carry_ref[...]
        scores = [None] * pages
        for r in reversed(range(pages)):
            bias = cq + carry + jnp.concatenate([gate_refs[r][0, :FOX_HEADS, :]] * steps, axis=0)
            scores[r] = _dot(qe, k_refs[r][0].reshape(FOX_WIDTH, PAGE_SIZE).astype(BF)) + bias
            carry = carry + jnp.concatenate([gate_refs[r][0, FOX_HEADS:, :]] * steps, axis=0)
        carry_ref[...] = carry

        def pv_pages(p):
            out = None
            for r in range(pages):
                term = _dot_nt(p[:, r * PAGE_SIZE:(r + 1) * PAGE_SIZE],
                               v_refs[r][0].reshape(FOX_WIDTH, PAGE_SIZE).astype(BF))
                out = term if out is None else out + term
            return out

        update(jnp.concatenate(scores, axis=1), pv_pages)

    def finish():
        @pl.when(is_last)
        def _():
            o = jnp.where(head_mask, acc_ref[...] / jnp.concatenate([l_ref[...]] * (FOX_WIDTH // LANES), axis=1),
                          0.0)
            o_ref[0] = jnp.sum(o.reshape(steps, FOX_HEADS, FOX_WIDTH), axis=1).astype(BF)

    return main, finish


N_PROMPT_IN = 5
N_SAMPLE_IN = 5


def _split_step(step, n_steps):
    if n_steps & (n_steps - 1) == 0:
        return step >> (n_steps.bit_length() - 1), step & (n_steps - 1)
    return lax.div(step, n_steps), lax.rem(step, n_steps)


def _fox_kernel(qi_ref, kj_ref, pt_ref, *refs, tile, sub, pages, steps, n_steps):
    del pt_ref
    prompt_in = refs[:N_PROMPT_IN]
    sample_in = refs[N_PROMPT_IN:N_PROMPT_IN + N_SAMPLE_IN]
    page_refs = refs[N_PROMPT_IN + N_SAMPLE_IN:N_PROMPT_IN + N_SAMPLE_IN + 3 * pages]
    op_ref, os_ref = refs[N_PROMPT_IN + N_SAMPLE_IN + 3 * pages:][:2]
    scratch = refs[N_PROMPT_IN + N_SAMPLE_IN + 3 * pages + 2:]
    step = (pl.program_id(0) * pl.num_programs(1) + pl.program_id(1)) * pl.num_programs(2) + pl.program_id(2)
    _, j = _split_step(step, n_steps)
    main, finish = _sample_attention(
        j, j == n_steps - 1, *sample_in, page_refs[:pages], page_refs[pages:2 * pages], page_refs[2 * pages:],
        os_ref, *scratch[5:], pages=pages, steps=steps)
    _prompt_attention(qi_ref, kj_ref, *prompt_in, op_ref, *scratch[:5], tile=tile, sub=sub, also=main)
    finish()


def _fox_attention(q3, k3, v3, aq3, ak3, qs3, kn3, vn3, cum3, cache_kt, cache_vt, page_gates, page_table):
    b, l, _ = q3.shape
    tile = min(1024, l)
    sub = min(512, tile)
    nq = l // tile
    pairs = FOX_HEADS // 2
    qi = np.concatenate([np.full(i + 1, i) for i in range(nq)]).astype(np.int32)
    kj = np.concatenate([np.arange(i + 1) for i in range(nq)]).astype(np.int32)
    n_tiles = len(qi)

    db, steps, _ = qs3.shape
    n_pages = page_table.shape[1]
    pages = min(16, n_pages)
    n_steps = n_pages // pages
    rows = steps * FOX_HEADS
    assert b * pairs * n_tiles >= db * n_steps
    qrep = jnp.repeat(qs3, FOX_HEADS, axis=1)
    cq = jnp.broadcast_to(cum3.reshape(db, rows, 1), (db, rows, LANES))
    ct = jnp.pad(cum3.transpose(0, 2, 1), ((0, 0), (0, 0), (0, LANES - steps)))
    kn = jnp.pad(kn3, ((0, 0), (0, LANES - steps), (0, 0)))
    vn = jnp.pad(vn3, ((0, 0), (0, LANES - steps), (0, 0)))

    def step_of(bi, p, t):
        return (bi * pairs + p) * n_tiles + t

    def seq_of(bi, p, t):
        return jnp.minimum(_split_step(step_of(bi, p, t), n_steps)[0], db - 1)

    def page_map(r, nd):
        def index(bi, p, t, qi_r, kj_r, pt):
            seq, j = _split_step(step_of(bi, p, t), n_steps)
            return (pt[jnp.minimum(seq, db - 1), (n_steps - 1 - j) * pages + r],) + (0,) * (nd - 1)
        return index

    seq = lambda shape: pl.BlockSpec((1,) + shape, lambda bi, p, t, qi_r, kj_r, pt: (seq_of(bi, p, t), 0, 0))
    in_specs = [pl.BlockSpec((1, tile, LANES), lambda bi, p, t, qi_r, kj_r, pt: (bi, qi_r[t], p)),
                pl.BlockSpec((1, tile, LANES), lambda bi, p, t, qi_r, kj_r, pt: (bi, kj_r[t], p)),
                pl.BlockSpec((1, tile, LANES), lambda bi, p, t, qi_r, kj_r, pt: (bi, kj_r[t], p)),
                pl.BlockSpec((1, tile, 2 * LANES), lambda bi, p, t, qi_r, kj_r, pt: (bi, qi_r[t], p)),
                pl.BlockSpec((1, tile, 2 * LANES), lambda bi, p, t, qi_r, kj_r, pt: (bi, kj_r[t], p))]
    in_specs += [seq((rows, FOX_WIDTH)), seq((rows, LANES)), seq((FOX_HEADS, LANES)),
                 seq((LANES, FOX_WIDTH)), seq((LANES, FOX_WIDTH))]
    in_specs += [pl.BlockSpec((1, FOX_HEADS, FOX_HEAD_DIM, PAGE_SIZE), page_map(r, 4)) for r in range(pages)] * 2
    in_specs += [pl.BlockSpec((1, 2 * FOX_HEADS, PAGE_SIZE), page_map(r, 3)) for r in range(pages)]
    grid_spec = pltpu.PrefetchScalarGridSpec(
        num_scalar_prefetch=3,
        grid=(b, pairs, n_tiles),
        in_specs=in_specs,
        out_specs=(pl.BlockSpec((1, tile, LANES), lambda bi, p, t, qi_r, kj_r, pt: (bi, qi_r[t], p)),
                   seq((steps, FOX_WIDTH))),
        scratch_shapes=[pltpu.VMEM((2, tile, LANES), BF)] * 3 + [pltpu.VMEM((2, tile, LANES), F32)] * 2 +
                       [pltpu.VMEM((rows, LANES), F32), pltpu.VMEM((rows, LANES), F32),
                        pltpu.VMEM((rows, FOX_WIDTH), F32), pltpu.VMEM((rows, LANES), F32)])
    return pl.pallas_call(
        functools.partial(_fox_kernel, tile=tile, sub=sub, pages=pages, steps=steps, n_steps=n_steps),
        grid_spec=grid_spec,
        out_shape=(jax.ShapeDtypeStruct((b, l, FOX_WIDTH), BF), jax.ShapeDtypeStruct((db, steps, FOX_WIDTH), BF)),
        compiler_params=_params(48, 3),
    )(jnp.asarray(qi), jnp.asarray(kj), page_table, q3, k3, v3, aq3, ak3, qrep, cq, ct, kn, vn,
      *([cache_kt] * pages), *([cache_vt] * pages), *([page_gates] * pages))


def _mem_kv_kernel(m_ref, w_ref, k_ref, v_ref):
    kv = _dot(m_ref[...].astype(BF), w_ref[...])
    k_ref[...] = kv[:, :MEM_WIDTH]
    v_ref[...] = kv[:, MEM_WIDTH:]


def _mem_kv(mem2d, w_bf):
    n = mem2d.shape[0]
    tm = min(256, n)
    out = pl.BlockSpec((tm, MEM_WIDTH), lambda i: (i, 0))
    return pl.pallas_call(
        _mem_kv_kernel,
        grid=(n // tm,),
        in_specs=[pl.BlockSpec((tm, D_MODEL), lambda i: (i, 0)), _full(w_bf.shape)],
        out_specs=(out, out),
        out_shape=(jax.ShapeDtypeStruct((n, MEM_WIDTH), F32),) * 2,
        compiler_params=_params(32, 1),
    )(mem2d, w_bf)


def _mem_attend_kernel(q_ref, k_ref, v_ref, o_ref, *, bb, rows_by_head):
    for b in range(bb):
        for h in range(MEM_HEADS):
            hs = slice(h * MEM_HEAD_DIM, (h + 1) * MEM_HEAD_DIM)
            if rows_by_head:
                k = k_ref[b, pl.ds(h, MEM_TOKENS, stride=MEM_HEADS), :]
                v = v_ref[b, pl.ds(h, MEM_TOKENS, stride=MEM_HEADS), :]
            else:
                k = k_ref[b, :, hs]
                v = v_ref[b, :, hs]
            s = _dot_nt(q_ref[b, :, hs], k.astype(BF)) * (MEM_HEAD_DIM ** -0.5)
            p = jnp.exp(s - jnp.max(s, axis=1, keepdims=True))
            den = jnp.sum(p, axis=1, keepdims=True)
            o_ref[b, :, hs] = (_dot(p.astype(BF), v.astype(BF)) / den).astype(BF)


def _mem_attend(qm3, mk3, mv3, bb, tq):
    b, l, _ = qm3.shape
    rows_by_head = mk3.shape[2] == MEM_HEAD_DIM
    kv = pl.BlockSpec((bb,) + mk3.shape[1:], lambda bi, i: (bi, 0, 0))
    qs = pl.BlockSpec((bb, tq, MEM_WIDTH), lambda bi, i: (bi, i, 0))
    return pl.pallas_call(
        functools.partial(_mem_attend_kernel, bb=bb, rows_by_head=rows_by_head),
        grid=(b // bb, l // tq),
        in_specs=[qs, kv, kv],
        out_specs=qs,
        out_shape=jax.ShapeDtypeStruct((b, l, MEM_WIDTH), BF),
        compiler_params=_params(40, 2),
    )(qm3, mk3, mv3)


def _merge_kernel(x_ref, yc_ref, yf_ref, ym_ref, wg_ref, bg_ref, wc_ref, bc_ref, wfo_ref, wmo_ref, wo_ref, bo_ref,
                  g1_ref, b1_ref, wrh_ref, wrl_ref, br_ref, h_ref, e_ref, gate_ref, *, tm):
    x = x_ref[...]
    xb = x.astype(BF)

    def gate(c):
        cs = slice(c * D_MODEL, (c + 1) * D_MODEL)
        return jax.nn.sigmoid(_dot(xb, wg_ref[:, cs]) + bg_ref[:, cs])

    mix = gate(0) * (_dot(yc_ref[...], wc_ref[...]) + bc_ref[...])
    mix = mix + gate(1) * _dot(yf_ref[...], wfo_ref[...])
    mix = mix + gate(2) * _dot(ym_ref[...], wmo_ref[...])
    pre = DEEPNORM_ALPHA * x + (_dot(mix.astype(BF), wo_ref[...]) + bo_ref[...])
    h = _layernorm(pre, g1_ref[...], b1_ref[...])
    for j in range(D_MODEL // LANES):
        h_ref[pl.ds(j, tm, stride=SUBLANES), :] = h[:, j * LANES:(j + 1) * LANES]

    h_hi = h.astype(BF)
    h_lo = (h - h_hi.astype(F32)).astype(BF)
    logits = _dot(h_hi, wrh_ref[...]) + _dot(h_hi, wrl_ref[...]) + _dot(h_lo, wrh_ref[...]) + br_ref[...]
    lane = lax.broadcasted_iota(jnp.int32, (tm, LANES), 1)
    lane_f = lane.astype(F32)
    work = jnp.where(lane < N_EXPERTS, logits, NEG_INF)
    vals, idxs = [], []
    for _ in range(TOP_K):
        mx = jnp.max(work, axis=1, keepdims=True)
        idx = jnp.min(jnp.where(work == mx, lane_f, float(LANES)), axis=1, keepdims=True)
        vals.append(mx)
        idxs.append(idx)
        work = jnp.where(lane_f == idx, NEG_INF, work)
    exps = [jnp.exp(v - vals[0]) for v in vals]
    den = exps[0] + exps[1] + exps[2] + exps[3]
    e_out = jnp.zeros((tm, LANES), F32)
    g_out = jnp.zeros((tm, LANES), F32)
    for kk in range(TOP_K):
        e_out = jnp.where(lane == kk, idxs[kk], e_out)
        g_out = jnp.where(lane == kk, exps[kk] / den, g_out)
    e_ref[...] = e_out[:, :TOP_K].astype(jnp.int32)
    gate_ref[...] = g_out[:, :TOP_K]


def _merge(x2d, yc, yf, ym, w):
    n = x2d.shape[0]
    tm = min(512, n)
    tok = lambda width: pl.BlockSpec((tm, width), lambda i: (i, 0))
    ws = (w["wg"], w["bg"], w["wc"], w["bc"], w["wfo"], w["wmo"], w["wo"], w["bo"], w["g1"], w["b1"],
          w["wrh"], w["wrl"], w["br"])
    return pl.pallas_call(
        functools.partial(_merge_kernel, tm=tm),
        grid=(n // tm,),
        in_specs=[tok(D_MODEL), tok(CONV_CH), tok(FOX_WIDTH), tok(MEM_WIDTH)] + [_full(a.shape) for a in ws],
        out_specs=(pl.BlockSpec((tm * SUBLANES, LANES), lambda i: (i, 0)), tok(TOP_K), tok(TOP_K)),
        out_shape=(jax.ShapeDtypeStruct((n * SUBLANES, LANES), F32), jax.ShapeDtypeStruct((n, TOP_K), jnp.int32),
                   jax.ShapeDtypeStruct((n, TOP_K), F32)),
        compiler_params=_params(56, 1),
    )(x2d, yc, yf, ym, *ws)


MOE_ROWS = 256
IDX_SLOTS = 4
IDX_STRIDE = 1024
DMA_UNROLL_BITS = 4
DMA_UNROLL = 1 << DMA_UNROLL_BITS
TOP_K_BITS = TOP_K.bit_length() - 1


def _moe_kernel(blk_e_ref, nvalid_ref, nreal_ref, rows_hbm, h_hbm, wg_ref, bg_ref, wu_ref, bu_ref, wd_ref, bd_ref,
                y_hbm, idx_ref, xbuf, ybuf, wgb, wub, wdb, idx_sem, in_sem, out_sem):
    i = pl.program_id(0)
    nreal = nreal_ref[0]
    tile = SUBLANES

    def idx_slot(blk):
        return blk & (IDX_SLOTS - 1)

    def idx_copy(blk):
        slot = idx_slot(blk)
        return pltpu.make_async_copy(rows_hbm.at[pl.ds(pl.multiple_of(blk * IDX_STRIDE, IDX_STRIDE), IDX_STRIDE)],
                                     idx_ref.at[pl.ds(pl.multiple_of(slot * IDX_STRIDE, IDX_STRIDE), IDX_STRIDE)],
                                     idx_sem.at[slot])

    def row_index(islot, r):
        return idx_ref[islot * IDX_STRIDE + r]

    def gather_row(islot, slot, r):
        tok = row_index(islot, r) >> TOP_K_BITS
        return pltpu.make_async_copy(h_hbm.at[pl.ds(pl.multiple_of(tok * tile, tile), tile)],
                                     xbuf.at[slot, pl.ds(pl.multiple_of(r * tile, tile), tile)],
                                     in_sem.at[slot])

    def scatter_row(islot, slot, r):
        dst = row_index(islot, r)
        return pltpu.make_async_copy(ybuf.at[slot, pl.ds(pl.multiple_of(r * tile, tile), tile)],
                                     y_hbm.at[pl.ds(pl.multiple_of(dst * tile, tile), tile)],
                                     out_sem.at[slot])

    def start_gather(blk):
        islot = idx_slot(blk)
        slot = blk & 1

        def body(c, carry):
            for u in range(DMA_UNROLL):
                gather_row(islot, slot, c * DMA_UNROLL + u).start()
            return carry
        lax.fori_loop(0, MOE_ROWS // DMA_UNROLL, body, 0)

    def wait_rows(sem, slot):
        pltpu.make_async_copy(xbuf.at[slot], ybuf.at[slot], sem.at[slot]).wait()

    def wait_scatter(blk):
        slot = blk & 1
        nv = nvalid_ref[blk]

        @pl.when(nv == MOE_ROWS)
        def _():
            wait_rows(out_sem, slot)

        @pl.when(nv < MOE_ROWS)
        def _():
            def body(r, c):
                pltpu.make_async_copy(ybuf.at[slot, pl.ds(0, tile)], y_hbm.at[pl.ds(0, tile)],
                                      out_sem.at[slot]).wait()
                return c
            lax.fori_loop(0, nv, body, 0)

    @pl.when((i == 0) & (nreal > 0))
    def _():
        idx_copy(0).start()
        idx_copy(0).wait()
        start_gather(0)

        @pl.when(nreal > 1)
        def _():
            idx_copy(1).start()

    @pl.when(i + 1 < nreal)
    def _():
        idx_copy(i + 1).wait()
        start_gather(i + 1)

        @pl.when(i + 2 < nreal)
        def _():
            idx_copy(i + 2).start()

    changed = (i == 0) | (blk_e_ref[i] != blk_e_ref[jnp.maximum(i - 1, 0)])

    @pl.when(changed & (i < nreal))
    def _():
        wgb[...] = wg_ref[0].astype(BF)
        wub[...] = wu_ref[0].astype(BF)
        wdb[...] = wd_ref[0].astype(BF)

    @pl.when(i < nreal)
    def _():
        slot = i & 1
        islot = idx_slot(i)
        wait_rows(in_sem, slot)

        @pl.when(i >= 2)
        def _():
            wait_scatter(i - 2)

        x = jnp.concatenate([xbuf[slot, pl.ds(j, MOE_ROWS, stride=SUBLANES), :] for j in range(D_MODEL // LANES)],
                            axis=1).astype(BF)

        a = jnp.minimum(_dot(x, wgb[...]) + bg_ref[0], SWIGLU_LIMIT)
        u = jnp.clip(_dot(x, wub[...]) + bu_ref[0], -SWIGLU_LIMIT, SWIGLU_LIMIT)
        hid = (u + 1.0) * a * jax.nn.sigmoid(SWIGLU_ALPHA * a)
        y = _dot(hid.astype(BF), wdb[...]) + bd_ref[0]
        for j in range(D_MODEL // LANES):
            ybuf[slot, pl.ds(j, MOE_ROWS, stride=SUBLANES), :] = y[:, j * LANES:(j + 1) * LANES]

        nv = nvalid_ref[i]

        @pl.when(nv == MOE_ROWS)
        def _():
            def body(c, carry):
                for u in range(DMA_UNROLL):
                    scatter_row(islot, slot, c * DMA_UNROLL + u).start()
                return carry
            lax.fori_loop(0, MOE_ROWS // DMA_UNROLL, body, 0)

        @pl.when(nv < MOE_ROWS)
        def _():
            def body(c, carry):
                for u in range(DMA_UNROLL):
                    r = c * DMA_UNROLL + u

                    @pl.when(r < nv)
                    def _():
                        scatter_row(islot, slot, r).start()
                return carry
            lax.fori_loop(0, (nv + DMA_UNROLL - 1) >> DMA_UNROLL_BITS, body, 0)

        @pl.when(i == nreal - 1)
        def _():
            @pl.when(i >= 1)
            def _():
                wait_scatter(i - 1)
            wait_scatter(i)


def _moe(h_rows, top_e, w):
    n = top_e.shape[0]
    flat_e = top_e.reshape(-1)
    n_flat = n * TOP_K
    n_blocks = -(-n_flat // MOE_ROWS) + N_EXPERTS
    order = jnp.argsort(flat_e).astype(jnp.int32)
    counts = jnp.sum((flat_e[:, None] == jnp.arange(N_EXPERTS, dtype=jnp.int32)[None, :]).astype(jnp.int32), axis=0)
    starts = jnp.cumsum(counts) - counts
    padded = (counts + MOE_ROWS - 1) // MOE_ROWS * MOE_ROWS
    pad_ends = jnp.cumsum(padded)
    pad_starts = pad_ends - padded
    blk_start = jnp.arange(n_blocks, dtype=jnp.int32) * MOE_ROWS
    blk_e = jnp.minimum(jnp.sum((pad_ends[None, :] <= blk_start[:, None]).astype(jnp.int32), axis=1),
                        N_EXPERTS - 1).astype(jnp.int32)
    nreal = (pad_ends[-1] // MOE_ROWS).astype(jnp.int32).reshape(1)
    blk_rank = blk_start - pad_starts[blk_e]
    nvalid = jnp.where(blk_start < pad_ends[-1], jnp.clip(counts[blk_e] - blk_rank, 0, MOE_ROWS), 0).astype(jnp.int32)
    r = jnp.arange(MOE_ROWS, dtype=jnp.int32)[None, :]
    src = jnp.clip((starts[blk_e] + blk_rank)[:, None] + r, 0, n_flat - 1)
    rows = jnp.where(r < nvalid[:, None], order[src], 0).astype(jnp.int32)
    rows = jnp.pad(rows, ((0, 0), (0, IDX_STRIDE - MOE_ROWS))).reshape(-1)
    out_rows = n_flat * SUBLANES
    wspec = pl.BlockSpec((1, D_MODEL, D_MODEL), lambda i, be, nv, nr: (be[i], 0, 0))
    bspec = pl.BlockSpec((1, 1, D_MODEL), lambda i, be, nv, nr: (be[i], 0, 0))
    any_spec = pl.BlockSpec(memory_space=pl.ANY)
    grid_spec = pltpu.PrefetchScalarGridSpec(
        num_scalar_prefetch=3,
        grid=(n_blocks,),
        in_specs=[any_spec, any_spec, wspec, bspec, wspec, bspec, wspec, bspec],
        out_specs=any_spec,
        scratch_shapes=[pltpu.SMEM((IDX_SLOTS * IDX_STRIDE,), jnp.int32),
                        pltpu.VMEM((2, MOE_ROWS * SUBLANES, LANES), F32),
                        pltpu.VMEM((2, MOE_ROWS * SUBLANES, LANES), F32),
                        pltpu.VMEM((D_MODEL, D_MODEL), BF), pltpu.VMEM((D_MODEL, D_MODEL), BF),
                        pltpu.VMEM((D_MODEL, D_MODEL), BF),
                        pltpu.SemaphoreType.DMA((IDX_SLOTS,)), pltpu.SemaphoreType.DMA((2,)),
                        pltpu.SemaphoreType.DMA((2,))])
    return pl.pallas_call(
        _moe_kernel,
        grid_spec=grid_spec,
        out_shape=jax.ShapeDtypeStruct((out_rows, LANES), F32),
        compiler_params=_params(56, 1, disable_bounds_checks=True),
    )(blk_e, nvalid, nreal, rows, h_rows, w["wgate"], w["bgate"], w["wup"], w["bup"], w["wdown"], w["bdown"])


def _combine_kernel(h_ref, y_ref, gate_ref, g2_ref, b2_ref, o_ref, *, tm):
    g = gate_ref[...]
    cols = []
    for j in range(D_MODEL // LANES):
        f = None
        for kk in range(TOP_K):
            term = g[:, kk:kk + 1] * y_ref[pl.ds(kk * SUBLANES + j, tm, stride=TOP_K * SUBLANES), :]
            f = term if f is None else f + term
        cols.append(DEEPNORM_ALPHA * h_ref[pl.ds(j, tm, stride=SUBLANES), :] + f)
    o_ref[...] = _layernorm(jnp.concatenate(cols, axis=1), g2_ref[...], b2_ref[...])


def _combine(h_rows, y_rows, gate, g2, b2, first_token):
    n = gate.shape[0]
    tm = min(512, n)
    first = first_token // tm
    return pl.pallas_call(
        functools.partial(_combine_kernel, tm=tm),
        grid=(n // tm,),
        in_specs=[pl.BlockSpec((tm * SUBLANES, LANES), lambda i: (i, 0)),
                  pl.BlockSpec((tm * TOP_K * SUBLANES, LANES), lambda i: (i + first, 0)),
                  pl.BlockSpec((tm, TOP_K), lambda i: (i, 0)), _full((1, D_MODEL)), _full((1, D_MODEL))],
        out_specs=pl.BlockSpec((tm, D_MODEL), lambda i: (i, 0)),
        out_shape=jax.ShapeDtypeStruct((n, D_MODEL), F32),
        compiler_params=_params(48, 1),
    )(h_rows, y_rows, gate, g2, b2)


def _row(v):
    return v.reshape(1, -1).astype(F32)


def _prep_weights(w_in, b_in, b_forget, conv_w, conv_b, conv_ln_g, conv_ln_b, w_conv_out, b_conv_out, w_fox_out,
                  w_mem_kv, w_mem_out, w_out, b_out, ln1_g, ln1_b, w_router, b_router, w_gate, b_gate, w_up, b_up,
                  w_down, b_down, ln2_g, ln2_b):
    o_q = 2 * CONV_CH
    o_f = o_q + 3 * FOX_WIDTH
    o_qm = o_f + FOX_HEADS
    o_g = o_qm + MEM_WIDTH
    pad_f = LANES - FOX_HEADS
    proj = dict(
        wglu=w_in[:, :o_q].astype(BF), bglu=_row(b_in[:o_q]),
        wqkv=w_in[:, o_q:o_f].astype(BF), bqkv=_row(b_in[o_q:o_f]),
        wf=jnp.pad(w_in[:, o_f:o_qm], ((0, 0), (0, pad_f))).astype(BF),
        bf=_row(jnp.pad(b_in[o_f:o_qm], (0, pad_f))), bfg=_row(jnp.pad(b_forget, (0, pad_f))),
        wqm=w_in[:, o_qm:o_g].astype(BF), bqm=_row(b_in[o_qm:o_g]))
    conv = dict(w=jnp.pad(conv_w, ((0, HIST_ROWS - CONV_WIDTH), (0, 0))).astype(F32), cb=_row(conv_b),
                g=_row(conv_ln_g), b=_row(conv_ln_b))
    pad_r = LANES - N_EXPERTS
    wr = jnp.pad(w_router, ((0, 0), (0, pad_r)))
    wr_hi = wr.astype(BF)
    merge = dict(
        wg=w_in[:, o_g:].astype(BF), bg=_row(b_in[o_g:]), wc=w_conv_out.astype(BF), bc=_row(b_conv_out),
        wfo=w_fox_out.astype(BF), wmo=w_mem_out.astype(BF), wo=w_out.astype(BF), bo=_row(b_out),
        g1=_row(ln1_g), b1=_row(ln1_b), wrh=wr_hi, wrl=(wr - wr_hi.astype(F32)).astype(BF),
        br=_row(jnp.pad(b_router, (0, pad_r))))
    moe = dict(wgate=w_gate, bgate=b_gate.reshape(N_EXPERTS, 1, D_MODEL), wup=w_up,
               bup=b_up.reshape(N_EXPERTS, 1, D_MODEL), wdown=w_down, bdown=b_down.reshape(N_EXPERTS, 1, D_MODEL))
    return proj, conv, merge, moe, w_mem_kv.astype(BF), _row(ln2_g), _row(ln2_b)


def _channel(groups, merge_w, moe_w, g2, b2):
    merged = [_merge(x2d, yc, yf, ym, merge_w) for x2d, yc, yf, ym in groups]
    y_rows = _moe(jnp.concatenate([m[0] for m in merged], axis=0), jnp.concatenate([m[1] for m in merged], axis=0),
                  moe_w)
    outs = []
    first = 0
    for h_rows, _, gate in merged:
        outs.append(_combine(h_rows, y_rows, gate, g2, b2, first))
        first += gate.shape[0]
    return outs


def kernel(x_prompt, x_sample, mem_prompt, cache_k, cache_v, cache_logf, page_table, cache_mem_k, cache_mem_v, state_conv, w_in, b_in, b_forget, conv_w, conv_b, conv_ln_g, conv_ln_b, w_conv_out, b_conv_out, w_fox_out, w_mem_kv, w_mem_out, w_out, b_out, ln1_g, ln1_b, w_router, b_router, w_gate, b_gate, w_up, b_up, w_down, b_down, ln2_g, ln2_b):
    proj_w, conv_w_, merge_w, moe_w, wkv, g2, b2 = _prep_weights(
        w_in, b_in, b_forget, conv_w, conv_b, conv_ln_g, conv_ln_b, w_conv_out, b_conv_out, w_fox_out, w_mem_kv,
        w_mem_out, w_out, b_out, ln1_g, ln1_b, w_router, b_router, w_gate, b_gate, w_up, b_up, w_down, b_down,
        ln2_g, ln2_b)
    b, l, d = x_prompt.shape
    db, t, _ = x_sample.shape
    hist_len = CONV_WIDTH - 1

    xp = x_prompt.reshape(b * l, d)
    u, q, k, v, kb, vb, logf, _, qm, aq, ak = _in_proj(xp, l, proj_w)
    u3 = u.reshape(b, l, CONV_CH)
    yc = _conv_prompt(u3, jnp.zeros((b, HIST_ROWS, CONV_CH), F32), conv_w_)
    mk, mv = _mem_kv(mem_prompt.reshape(b * MEM_TOKENS, d), wkv)
    mk3 = mk.reshape(b, MEM_TOKENS, MEM_WIDTH)
    mv3 = mv.reshape(b, MEM_TOKENS, MEM_WIDTH)
    ym = _mem_attend(qm.reshape(b, l, MEM_WIDTH), mk3, mv3, 1, min(512, l))

    xs = x_sample.reshape(db * t, d)
    us, qs, ks, vs, ksb, vsb, logfs, cums, qms, _, _ = _in_proj(xs, t, proj_w)
    us_ext = jnp.concatenate([state_conv.astype(F32), us.reshape(db, t, CONV_CH)], axis=1)
    ycs = _conv_sample(us_ext.transpose(1, 0, 2), conv_w_).transpose(1, 0, 2)
    n_phys = cache_logf.shape[0]
    page_gates = _page_suffix(cache_logf.transpose(0, 2, 1).reshape(n_phys * FOX_HEADS, PAGE_SIZE))

    yf, yfs = _fox_attention(
        q.reshape(b, l, FOX_WIDTH), kb.reshape(b, l, FOX_WIDTH), vb.reshape(b, l, FOX_WIDTH),
        aq.reshape(b, l, FOX_HEADS * LANES), ak.reshape(b, l, FOX_HEADS * LANES),
        qs.reshape(db, t, FOX_WIDTH), ksb.reshape(db, t, FOX_WIDTH), vsb.reshape(db, t, FOX_WIDTH),
        cums.reshape(db, t, FOX_HEADS), cache_k.transpose(0, 2, 3, 1), cache_v.transpose(0, 2, 3, 1),
        page_gates, page_table)
    t_pad = 2 * SUBLANES
    qms3 = jnp.pad(qms.reshape(db, t, MEM_WIDTH), ((0, 0), (0, t_pad - t), (0, 0)))
    yms = _mem_attend(qms3, cache_mem_k.reshape(db, MEM_TOKENS * MEM_HEADS, MEM_HEAD_DIM),
                      cache_mem_v.reshape(db, MEM_TOKENS * MEM_HEADS, MEM_HEAD_DIM), 8, t_pad)[:, :t]

    y_prompt, y_sample = _channel(
        [(xp, yc.reshape(b * l, CONV_CH), yf.reshape(b * l, FOX_WIDTH), ym.reshape(b * l, MEM_WIDTH)),
         (xs, ycs.reshape(db * t, CONV_CH), yfs.reshape(db * t, FOX_WIDTH), yms.reshape(db * t, MEM_WIDTH))],
        merge_w, moe_w, g2, b2)
    y_prompt = y_prompt.reshape(b, l, d)
    y_sample = y_sample.reshape(db, t, d)

    heads = lambda a, n, s: a.reshape(n, s, FOX_HEADS, FOX_HEAD_DIM)
    heads_t = lambda a: a.reshape(b, FOX_HEADS, FOX_HEAD_DIM, l).transpose(0, 3, 1, 2)
    return (y_prompt, y_sample,
            heads_t(k), heads_t(v), logf.reshape(b, l, FOX_HEADS),
            mk.reshape(b, MEM_TOKENS, MEM_HEADS, MEM_HEAD_DIM), mv.reshape(b, MEM_TOKENS, MEM_HEADS, MEM_HEAD_DIM),
            u3[:, l - hist_len:, :],
            heads(ks, db, t), heads(vs, db, t), logfs.reshape(db, t, FOX_HEADS),
            us_ext[:, t:, :])
```
